```python
import jax, jax.numpy as jnp
from jax import lax
import numpy as np

D_MODEL = 1024
BATCH = 32
SEQ = 2048
DEPTH = 1

MLA_HEADS = 8
MLA_NOPE = 64
MLA_ROPE = 32
MLA_V = 64
Q_LORA = 384
KV_LORA = 256
MLA_WIDTH = MLA_HEADS * MLA_V

SWA_HEADS = 8
SWA_KV_HEADS = 2
SWA_HEAD_DIM = 64
SWA_GROUP = SWA_HEADS // SWA_KV_HEADS
SWA_WIDTH = SWA_HEADS * SWA_HEAD_DIM
SWA_KV_WIDTH = SWA_KV_HEADS * SWA_HEAD_DIM
WINDOW = 128

D_MIX = MLA_WIDTH + SWA_WIDTH
Q_BLOCK = 128
ROPE_THETA = 10000.0
EPS = 1e-6
ALIBI_MAX_EXP = 8.0

IN_SPLITS = (Q_LORA, KV_LORA, MLA_ROPE, MLA_WIDTH,
             SWA_WIDTH, SWA_KV_WIDTH, SWA_KV_WIDTH, SWA_WIDTH)
D_IN = int(sum(IN_SPLITS))
SPLIT_IDX = [int(v) for v in np.cumsum(IN_SPLITS)[:-1]]

kernel_name = 'hymba_mla_swa_adaln_block'


def rmsnorm(t, gain):
    tf = t.astype(jnp.float32)
    y = tf * lax.rsqrt(jnp.mean(tf * tf, axis=-1, keepdims=True) + EPS)
    return (y * gain.astype(jnp.float32)).astype(t.dtype)


def rope_cos_sin(positions):
    inv = ROPE_THETA ** (-jnp.arange(0, MLA_ROPE, 2, dtype=jnp.float32) / MLA_ROPE)
    ang = positions.astype(jnp.float32)[..., None] * inv
    return jnp.cos(ang), jnp.sin(ang)


def apply_rope(t, cos, sin):
    tf = t.astype(jnp.float32)
    t1, t2 = jnp.split(tf, 2, axis=-1)
    return jnp.concatenate([t1 * cos - t2 * sin, t2 * cos + t1 * sin], axis=-1).astype(t.dtype)


def alibi_slopes(n_heads):
    h = jnp.arange(1, n_heads + 1, dtype=jnp.float32)
    return 2.0 ** (-ALIBI_MAX_EXP * h / n_heads)


def mla_attention(q_nope, q_pe, k_nope, k_pe, v):
    B, S = q_nope.shape[0], q_nope.shape[1]
    nb = S // Q_BLOCK
    scale = (MLA_NOPE + MLA_ROPE) ** -0.5
    key_idx = jnp.arange(S)

    def to_blocks(t):
        return jnp.moveaxis(t.reshape(B, nb, Q_BLOCK, *t.shape[2:]), 1, 0)

    def one_block(args):
        qn, qp, blk = args
        s = (jnp.einsum('bqhd,bshd->bhqs', qn, k_nope, preferred_element_type=jnp.float32)
             + jnp.einsum('bqhr,bsr->bhqs', qp, k_pe, preferred_element_type=jnp.float32)) * scale
        q_idx = blk * Q_BLOCK + jnp.arange(Q_BLOCK)
        s = jnp.where(key_idx[None, :] <= q_idx[:, None], s, -jnp.inf)
        p = jax.nn.softmax(s, axis=-1).astype(v.dtype)
        return jnp.einsum('bhqs,bshd->bqhd', p, v)

    o = lax.map(one_block, (to_blocks(q_nope), to_blocks(q_pe), jnp.arange(nb)))
    return jnp.moveaxis(o, 0, 1).reshape(B, S, MLA_WIDTH)


def swa_attention(q, k, v, positions, slopes, sinks):
    B, S = q.shape[0], q.shape[1]
    nb = S // WINDOW
    scale = SWA_HEAD_DIM ** -0.5

    def band(t):
        tb = t.reshape(B, nb, WINDOW, *t.shape[2:])
        prev = jnp.concatenate([jnp.zeros_like(tb[:, :1]), tb[:, :-1]], axis=1)
        return jnp.concatenate([prev, tb], axis=2)

    qb = q.reshape(B, nb, WINDOW, SWA_KV_HEADS, SWA_GROUP, SWA_HEAD_DIM)
    kb = band(k.reshape(B, S, SWA_KV_HEADS, SWA_HEAD_DIM))
    vb = band(v.reshape(B, S, SWA_KV_HEADS, SWA_HEAD_DIM))
    s = jnp.einsum('bnqkgd,bnskd->bnkgqs', qb, kb, preferred_element_type=jnp.float32) * scale
    pq = positions.reshape(B, nb, WINDOW)
    pk = band(positions)
    dist = (pq[..., :, None] - pk[..., None, :]).astype(jnp.float32)
    s = s - slopes.reshape(SWA_KV_HEADS, SWA_GROUP)[None, None, :, :, None, None] * dist[:, :, None, None]
    i = jnp.arange(WINDOW)[:, None]
    j = jnp.arange(2 * WINDOW)[None, :]
    rel = WINDOW + i - j
    blk = jnp.arange(nb)[:, None, None]
    valid = (rel >= 0) & (rel < WINDOW) & ((blk > 0) | (j >= WINDOW))
    s = jnp.where(valid[None, :, None, None], s, -jnp.inf)
    sink = jnp.broadcast_to(sinks.astype(jnp.float32).reshape(SWA_KV_HEADS, SWA_GROUP)[None, None, :, :, None, None],
                            s.shape[:-1] + (1,))
    p = jax.nn.softmax(jnp.concatenate([s, sink], axis=-1), axis=-1)[..., :-1].astype(v.dtype)
    o = jnp.einsum('bnkgqs,bnskd->bnqkgd', p, vb)
    return o.reshape(B, S, SWA_WIDTH)


def setup_inputs(seed: int = 0) -> dict:
    key = jax.random.key(seed)
    ks = jax.random.split(key, 16)
    f32 = jnp.float32
    nrm = lambda k, shape, s: jax.random.normal(k, shape, f32) * s
    x = jax.random.normal(ks[0], (BATCH, SEQ, D_MODEL), f32)
    c = jax.random.normal(ks[1], (BATCH, D_MODEL), f32)
    offs = jax.random.randint(ks[2], (BATCH, 1), 0, 1024, dtype=jnp.int32)
    positions = offs + jnp.arange(SEQ, dtype=jnp.int32)[None, :]
    return {
        'x': x,
        'c': c,
        'positions': positions,
        'w_ada': nrm(ks[3], (DEPTH, D_MODEL, 3 * D_MODEL), D_MODEL ** -0.5),
        'b_ada': nrm(ks[4], (DEPTH, 3 * D_MODEL), 0.02),
        'norm_gain': 1.0 + nrm(ks[5], (DEPTH, D_MODEL), 0.02),
        'w_in': nrm(ks[6], (DEPTH, D_MODEL, D_IN), D_MODEL ** -0.5),
        'q_norm_gain': 1.0 + nrm(ks[7], (DEPTH, Q_LORA), 0.02),
        'kv_norm_gain': 1.0 + nrm(ks[8], (DEPTH, KV_LORA), 0.02),
        'w_uq': nrm(ks[9], (DEPTH, Q_LORA, MLA_HEADS * (MLA_NOPE + MLA_ROPE)), Q_LORA ** -0.5),
        'w_ukv': nrm(ks[10], (DEPTH, KV_LORA, MLA_HEADS * (MLA_NOPE + MLA_V)), KV_LORA ** -0.5),
        'swa_sinks': nrm(ks[11], (DEPTH, SWA_HEADS), 1.0),
        'w_out': nrm(ks[12], (DEPTH, D_MIX, D_MODEL), D_MIX ** -0.5),
        'final_gain': 1.0 + nrm(ks[13], (D_MODEL,), 0.02),
    }


def reference(x, c, positions, w_ada, b_ada, norm_gain, w_in, q_norm_gain, kv_norm_gain,
              w_uq, w_ukv, swa_sinks, w_out, final_gain):
    B, S, _ = x.shape
    cos, sin = rope_cos_sin(positions)
    slopes = alibi_slopes(SWA_HEADS)
    c_act = jax.nn.silu(c)
    for l in range(DEPTH):
        mod = c_act @ w_ada[l] + b_ada[l]
        shift, scale, gate = jnp.split(mod, 3, axis=-1)
        h = rmsnorm(x, norm_gain[l]) * (1.0 + scale[:, None, :]) + shift[:, None, :]
        z = h @ w_in[l]
        zq, zkv, kr, g_mla, q_s, k_s, v_s, g_swa = jnp.split(z, SPLIT_IDX, axis=-1)
        q = (rmsnorm(zq, q_norm_gain[l]) @ w_uq[l]).reshape(B, S, MLA_HEADS, MLA_NOPE + MLA_ROPE)
        q_nope, q_pe = q[..., :MLA_NOPE], apply_rope(q[..., MLA_NOPE:], cos[:, :, None, :], sin[:, :, None, :])
        kv = (rmsnorm(zkv, kv_norm_gain[l]) @ w_ukv[l]).reshape(B, S, MLA_HEADS, MLA_NOPE + MLA_V)
        k_nope, v_mla = kv[..., :MLA_NOPE], kv[..., MLA_NOPE:]
        k_pe = apply_rope(kr, cos, sin)
        o_mla = mla_attention(q_nope, q_pe, k_nope, k_pe, v_mla).astype(x.dtype)
        o_swa = swa_attention(q_s, k_s, v_s, positions, slopes, swa_sinks[l]).astype(x.dtype)
        y = jnp.concatenate([o_mla * jax.nn.silu(g_mla), o_swa * jax.nn.silu(g_swa)], axis=-1) @ w_out[l]
        x = x + gate[:, None, :] * y
    return rmsnorm(x, final_gain)
```

```python
import functools
import math

import jax
import jax.numpy as jnp
import numpy as np
from jax import lax
from jax.experimental import pallas as pl
from jax.experimental.pallas import tpu as pltpu

F32 = jnp.float32
BF16 = jnp.bfloat16

D_MODEL = 1024
MLA_HEADS = 8
MLA_NOPE = 64
MLA_ROPE = 32
MLA_V = 64
Q_LORA = 384
KV_LORA = 256
MLA_WIDTH = MLA_HEADS * MLA_V
SWA_HEADS = 8
SWA_KV_HEADS = 2
SWA_HEAD_DIM = 64
SWA_GROUP = SWA_HEADS // SWA_KV_HEADS
SWA_WIDTH = SWA_HEADS * SWA_HEAD_DIM
SWA_KV_WIDTH = SWA_KV_HEADS * SWA_HEAD_DIM
WINDOW = 128
ROPE_THETA = 10000.0
EPS = 1e-6
ALIBI_MAX_EXP = 8.0

LANES = 128
HEAD_PAD = LANES
LOG2E = math.log2(math.e)
NEG_BIG = -1e30

C_ZQ = 0
C_ZKV = C_ZQ + Q_LORA
C_KR = C_ZKV + KV_LORA
C_GM = C_KR + LANES
C_QS = C_GM + MLA_WIDTH
C_KS = C_QS + SWA_WIDTH
C_VS = C_KS + SWA_KV_WIDTH
C_GS = C_VS + SWA_KV_WIDTH
D_IN_PACKED = C_GS + SWA_WIDTH

VMEM_LIMIT = 56 * 1024 * 1024


def _swap_halves(w):
    half = w.shape[-1] // 2
    return jnp.concatenate([w[..., half:], w[..., :half]], axis=-1)


def _pack_weights(w_in, w_uq, w_ukv):
    d = w_in.shape[0]
    s = np.cumsum([0, Q_LORA, KV_LORA, MLA_ROPE, MLA_WIDTH, SWA_WIDTH, SWA_KV_WIDTH, SWA_KV_WIDTH, SWA_WIDTH])
    zq, zkv, kr, gm, qs, ks, vs, gs = [w_in[:, s[i]:s[i + 1]] for i in range(8)]
    kr_blk = jnp.concatenate([jnp.zeros((d, MLA_NOPE), w_in.dtype), kr, _swap_halves(kr)], axis=1)
    w_in_p = jnp.concatenate([zq, zkv, kr_blk, gm, qs, ks, vs, gs], axis=1).astype(BF16)

    uq = w_uq.reshape(Q_LORA, MLA_HEADS, MLA_NOPE + MLA_ROPE)
    uq_p = jnp.concatenate([uq, _swap_halves(uq[..., MLA_NOPE:])], axis=-1)
    w_uq_p = uq_p.reshape(Q_LORA, MLA_HEADS * HEAD_PAD).astype(BF16)

    ukv = w_ukv.reshape(KV_LORA, MLA_HEADS, MLA_NOPE + MLA_V)
    uk = jnp.concatenate([ukv[..., :MLA_NOPE], jnp.zeros((KV_LORA, MLA_HEADS, HEAD_PAD - MLA_NOPE), w_ukv.dtype)], axis=-1)
    w_uk_p = uk.reshape(KV_LORA, MLA_HEADS * HEAD_PAD).astype(BF16)
    w_uv_p = ukv[..., MLA_NOPE:].reshape(KV_LORA, MLA_WIDTH).astype(BF16)
    return w_in_p, w_uq_p, w_uk_p, w_uv_p


def _adaln_kernel(c_ref, w_ref, b_ref, o_ref):
    c = c_ref[...]
    a = c * jax.nn.sigmoid(c)
    a_hi = a.astype(BF16)
    a_lo = (a - a_hi.astype(F32)).astype(BF16)
    w = w_ref[...]
    w_hi = w.astype(BF16)
    w_lo = (w - w_hi.astype(F32)).astype(BF16)
    acc = jnp.dot(a_hi, w_hi, preferred_element_type=F32)
    acc += jnp.dot(a_hi, w_lo, preferred_element_type=F32)
    acc += jnp.dot(a_lo, w_hi, preferred_element_type=F32)
    o_ref[...] = acc + b_ref[...]


def _adaln(c, w_ada, b_ada):
    b, d = c.shape
    n = w_ada.shape[1]
    tn = 1024
    return pl.pallas_call(
        _adaln_kernel,
        grid=(n // tn,),
        in_specs=[pl.BlockSpec((b, d), lambda j: (0, 0)),
                  pl.BlockSpec((d, tn), lambda j: (0, j)),
                  pl.BlockSpec((1, tn), lambda j: (0, j))],
        out_specs=pl.BlockSpec((b, tn), lambda j: (0, j)),
        out_shape=jax.ShapeDtypeStruct((b, n), F32),
        compiler_params=pltpu.CompilerParams(dimension_semantics=("arbitrary",), vmem_limit_bytes=VMEM_LIMIT),
        name="adaln",
    )(c, w_ada, b_ada.reshape(1, n))


def _rms(t, gain):
    return t * lax.rsqrt(jnp.mean(t * t, axis=-1, keepdims=True) + EPS) * gain


def _rope_tables(pos):
    lane = lax.broadcasted_iota(jnp.int32, (1, LANES), 1)
    pair = (lane % (MLA_ROPE // 2)).astype(F32)
    inv = jnp.exp(pair * (-2.0 * math.log(ROPE_THETA) / MLA_ROPE))
    ang = pos * inv
    cos, sin = jnp.cos(ang), jnp.sin(ang)
    in_pe = (lane >= MLA_NOPE) & (lane < MLA_NOPE + MLA_ROPE)
    first_half = lane < MLA_NOPE + MLA_ROPE // 2
    a = jnp.where(lane < MLA_NOPE, 1.0, jnp.where(in_pe, cos, 0.0))
    b = jnp.where(in_pe, jnp.where(first_half, -sin, sin), 0.0)
    return a, b


def _rope_group(t, a, b):
    return t * a + pltpu.roll(t, LANES - MLA_ROPE, 1) * b


def _dup_halves(t):
    lane = lax.broadcasted_iota(jnp.int32, (1, LANES), 1)
    r = pltpu.roll(t, LANES // 2, 1)
    lo = lane < LANES // 2
    return jnp.concatenate([jnp.where(lo, t, r), jnp.where(lo, r, t)], axis=1)


def _inproj_kernel(x_ref, mod_ref, pos_ref, ng_ref, qg_ref, kvg_ref, win_ref, wuq_ref, wuk_ref, wuv_ref,
                   q_ref, k_ref, v_ref, gm_ref, qs_ref, ks_ref, vs_ref, gs_ref):
    x = x_ref[0]
    mod = mod_ref[0]
    shift, scale = mod[0:1], mod[1:2]
    h = _rms(x, ng_ref[...]) * (1.0 + scale) + shift
    hb = h.astype(BF16)

    def seg(lo, hi):
        return jnp.dot(hb, win_ref[:, lo:hi], preferred_element_type=F32)

    a, b = _rope_tables(pos_ref[0])
    q_scale = (MLA_NOPE + MLA_ROPE) ** -0.5 * LOG2E

    qn = _rms(seg(C_ZQ, C_ZKV), qg_ref[...]).astype(BF16)
    q = jnp.dot(qn, wuq_ref[...], preferred_element_type=F32)
    aq, bq = a * q_scale, b * q_scale
    for hd in range(MLA_HEADS):
        sl = slice(hd * HEAD_PAD, (hd + 1) * HEAD_PAD)
        q_ref[0, :, sl] = _rope_group(q[:, sl], aq, bq).astype(BF16)

    kvn = _rms(seg(C_ZKV, C_KR), kvg_ref[...]).astype(BF16)
    kpe = _rope_group(seg(C_KR, C_GM), a, b)
    kn = jnp.dot(kvn, wuk_ref[...], preferred_element_type=F32)
    for hd in range(MLA_HEADS):
        sl = slice(hd * HEAD_PAD, (hd + 1) * HEAD_PAD)
        k_ref[0, :, sl] = (kn[:, sl] + kpe).astype(BF16)
    v_ref[0] = jnp.dot(kvn, wuv_ref[...], preferred_element_type=F32).astype(BF16)

    g = seg(C_GM, C_QS)
    gm_ref[0] = (g * jax.nn.sigmoid(g)).astype(BF16)
    qs_ref[0] = seg(C_QS, C_KS).astype(BF16)
    ks_ref[0] = _dup_halves(seg(C_KS, C_VS)).astype(BF16)
    vs_ref[0] = _dup_halves(seg(C_VS, C_GS)).astype(BF16)
    g = seg(C_GS, D_IN_PACKED)
    gs_ref[0] = (g * jax.nn.sigmoid(g)).astype(BF16)


def _inproj(x, mod3, pos_col, norm_gain, q_gain, kv_gain, w_in_p, w_uq_p, w_uk_p, w_uv_p, tm):
    b, s, d = x.shape
    const = lambda shape: pl.BlockSpec(shape, lambda bi, i: (0,) * len(shape))
    tok = lambda w: pl.BlockSpec((1, tm, w), lambda bi, i: (bi, i, 0))
    widths = [MLA_HEADS * HEAD_PAD, MLA_HEADS * HEAD_PAD, MLA_WIDTH, MLA_WIDTH,
              SWA_WIDTH, 2 * SWA_KV_WIDTH, 2 * SWA_KV_WIDTH, SWA_WIDTH]
    return pl.pallas_call(
        _inproj_kernel,
        grid=(b, s // tm),
        in_specs=[tok(d),
                  pl.BlockSpec((1, 3, d), lambda bi, i: (bi, 0, 0)),
                  tok(1),
                  const((1, d)), const((1, Q_LORA)), const((1, KV_LORA)),
                  const(w_in_p.shape), const(w_uq_p.shape), const(w_uk_p.shape), const(w_uv_p.shape)],
        out_specs=[tok(w) for w in widths],
        out_shape=[jax.ShapeDtypeStruct((b, s, w), BF16) for w in widths],
        compiler_params=pltpu.CompilerParams(dimension_semantics=("parallel", "parallel"),
                                             vmem_limit_bytes=VMEM_LIMIT),
        name="inproj",
    )(x, mod3, pos_col, norm_gain, q_gain, kv_gain, w_in_p, w_uq_p, w_uk_p, w_uv_p)


def _dot_nt(a, b):
    return lax.dot_general(a, b, (((1,), (1,)), ((), ())), preferred_element_type=F32)


def _mla_kernel(q_ref, k_ref, v_ref, g_ref, o_ref, *, tq):
    i = pl.program_id(2)
    q = q_ref[0]
    qa, qb = q[:, :HEAD_PAD], q[:, HEAD_PAD:]
    lane = lax.broadcasted_iota(jnp.int32, (1, LANES), 1)
    first = lane < MLA_V

    def scores(start):
        k = k_ref[0, pl.ds(start, tq), :]
        return _dot_nt(qa, k[:, :HEAD_PAD]), _dot_nt(qb, k[:, HEAD_PAD:])

    def pv(pa, pb, start):
        v = v_ref[0, pl.ds(start, tq), :]
        oa = jnp.dot(pa.astype(BF16), v, preferred_element_type=F32)
        ob = jnp.dot(pb.astype(BF16), v, preferred_element_type=F32)
        return jnp.where(first, oa, ob)

    d0 = pl.multiple_of(i * tq, tq)
    sa, sb = scores(d0)
    row = lax.broadcasted_iota(jnp.int32, (tq, tq), 0)
    col = lax.broadcasted_iota(jnp.int32, (tq, tq), 1)
    causal = col <= row
    sa = jnp.where(causal, sa, NEG_BIG)
    sb = jnp.where(causal, sb, NEG_BIG)
    ma = jnp.max(sa, axis=-1, keepdims=True)
    mb = jnp.max(sb, axis=-1, keepdims=True)
    pa = jnp.exp2(sa - ma)
    pb = jnp.exp2(sb - mb)
    la = jnp.sum(pa, axis=-1, keepdims=True)
    lb = jnp.sum(pb, axis=-1, keepdims=True)
    acc = pv(pa, pb, d0)

    def body(kb, carry):
        ma, la, mb, lb, acc = carry
        start = pl.multiple_of(kb * tq, tq)
        sa, sb = scores(start)
        ma_n = jnp.maximum(ma, jnp.max(sa, axis=-1, keepdims=True))
        mb_n = jnp.maximum(mb, jnp.max(sb, axis=-1, keepdims=True))
        al_a = jnp.exp2(ma - ma_n)
        al_b = jnp.exp2(mb - mb_n)
        pa = jnp.exp2(sa - ma_n)
        pb = jnp.exp2(sb - mb_n)
        la = la * al_a + jnp.sum(pa, axis=-1, keepdims=True)
        lb = lb * al_b + jnp.sum(pb, axis=-1, keepdims=True)
        acc = acc * jnp.where(first, al_a, al_b) + pv(pa, pb, start)
        return ma_n, la, mb_n, lb, acc

    ma, la, mb, lb, acc = lax.fori_loop(0, i, body, (ma, la, mb, lb, acc))
    o = acc * jnp.where(first, 1.0 / la, 1.0 / lb)
    o_ref[0] = (o * g_ref[0].astype(F32)).astype(BF16)


def _mla(q, k, v, gm, tq):
    b, s, _ = q.shape
    pairs = MLA_HEADS // 2
    return pl.pallas_call(
        functools.partial(_mla_kernel, tq=tq),
        grid=(b, pairs, s // tq),
        in_specs=[pl.BlockSpec((1, tq, 2 * HEAD_PAD), lambda bi, j, i: (bi, i, j)),
                  pl.BlockSpec((1, s, 2 * HEAD_PAD), lambda bi, j, i: (bi, 0, j)),
                  pl.BlockSpec((1, s, 2 * MLA_V), lambda bi, j, i: (bi, 0, j)),
                  pl.BlockSpec((1, tq, 2 * MLA_V), lambda bi, j, i: (bi, i, j))],
        out_specs=pl.BlockSpec((1, tq, 2 * MLA_V), lambda bi, j, i: (bi, i, j)),
        out_shape=jax.ShapeDtypeStruct((b, s, MLA_WIDTH), BF16),
        compiler_params=pltpu.CompilerParams(dimension_semantics=("parallel", "parallel", "arbitrary"),
                                             vmem_limit_bytes=VMEM_LIMIT),
        name="mla",
    )(q, k, v, gm)


def _swa_kernel(sink_ref, q_ref, k_ref, v_ref, g_ref, pc_ref, pr_ref, o_ref, *, tq):
    i = pl.program_id(1)
    nblk = tq // WINDOW
    lane = lax.broadcasted_iota(jnp.int32, (1, LANES), 1)
    first = lane < SWA_HEAD_DIM
    row = lax.broadcasted_iota(jnp.int32, (WINDOW, WINDOW), 0)
    col = lax.broadcasted_iota(jnp.int32, (WINDOW, WINDOW), 1)
    in_cur = col <= row
    scale = SWA_HEAD_DIM ** -0.5
    for nl in range(nblk):
        n = i * nblk + nl
        rows = slice(nl * WINDOW, (nl + 1) * WINDOW)
        cur = pl.multiple_of(n * WINDOW, WINDOW)
        prv = pl.multiple_of(jnp.maximum(n - 1, 0) * WINDOW, WINDOW)
        in_prev = jnp.logical_and(col > row, n > 0)
        pq = pc_ref[0, rows, :]
        d_cur = pq - pr_ref[0, n]
        d_prev = pq - pr_ref[0, jnp.maximum(n - 1, 0)]
        for kvh in range(SWA_KV_HEADS):
            ksl = slice(kvh * LANES, (kvh + 1) * LANES)
            k_cur, k_prev = k_ref[0, pl.ds(cur, WINDOW), ksl], k_ref[0, pl.ds(prv, WINDOW), ksl]
            v_cur, v_prev = v_ref[0, pl.ds(cur, WINDOW), ksl], v_ref[0, pl.ds(prv, WINDOW), ksl]
            for pr in range(SWA_GROUP // 2):
                pair = kvh * (SWA_GROUP // 2) + pr
                psl = slice(pair * LANES, (pair + 1) * LANES)
                q2 = q_ref[0, rows, psl]
                outs = []
                for half in range(2):
                    hd = 2 * pair + half
                    slope = 2.0 ** (-ALIBI_MAX_EXP * (hd + 1) / SWA_HEADS)
                    qh = jnp.where(first if half == 0 else jnp.logical_not(first), q2, jnp.zeros_like(q2))
                    s_cur = _dot_nt(qh, k_cur) * scale - slope * d_cur
                    s_prev = _dot_nt(qh, k_prev) * scale - slope * d_prev
                    s_cur = jnp.where(in_cur, s_cur, NEG_BIG)
                    s_prev = jnp.where(in_prev, s_prev, NEG_BIG)
                    sink = sink_ref[hd]
                    m = jnp.maximum(jnp.maximum(jnp.max(s_cur, axis=-1, keepdims=True),
                                                jnp.max(s_prev, axis=-1, keepdims=True)), sink)
                    p_cur = jnp.exp(s_cur - m)
                    p_prev = jnp.exp(s_prev - m)
                    l = (jnp.sum(p_cur, axis=-1, keepdims=True) + jnp.sum(p_prev, axis=-1, keepdims=True)
                         + jnp.exp(sink - m))
                    o = (jnp.dot(p_cur.astype(BF16), v_cur, preferred_element_type=F32)
                         + jnp.dot(p_prev.astype(BF16), v_prev, preferred_element_type=F32))
                    outs.append(o / l)
                o2 = jnp.where(first, outs[0], outs[1])
                o_ref[0, rows, psl] = (o2 * g_ref[0, rows, psl].astype(F32)).astype(BF16)


def _swa(sinks, qs, ks, vs, gs, pos_col, pos_row, tq):
    b, s, _ = qs.shape
    tok = lambda w: pl.BlockSpec((1, tq, w), lambda bi, i: (bi, i, 0))
    full = lambda w: pl.BlockSpec((1, s, w), lambda bi, i: (bi, 0, 0))
    return pl.pallas_call(
        functools.partial(_swa_kernel, tq=tq),
        grid=(b, s // tq),
        in_specs=[pl.BlockSpec(memory_space=pltpu.SMEM),
                  tok(SWA_WIDTH), full(2 * SWA_KV_WIDTH), full(2 * SWA_KV_WIDTH), tok(SWA_WIDTH), tok(1),
                  pl.BlockSpec((1, s // WINDOW, 1, WINDOW), lambda bi, i: (bi, 0, 0, 0))],
        out_specs=tok(SWA_WIDTH),
        out_shape=jax.ShapeDtypeStruct((b, s, SWA_WIDTH), BF16),
        compiler_params=pltpu.CompilerParams(dimension_semantics=("parallel", "arbitrary"),
                                             vmem_limit_bytes=VMEM_LIMIT),
        name="swa",
    )(sinks, qs, ks, vs, gs, pos_col, pos_row)


def _outproj_kernel(om_ref, os_ref, x_ref, mod_ref, wm_ref, ws_ref, fg_ref, o_ref):
    y = jnp.dot(om_ref[0], wm_ref[...], preferred_element_type=F32)
    y += jnp.dot(os_ref[0], ws_ref[...], preferred_element_type=F32)
    gate = mod_ref[0][2:3]
    o_ref[0] = _rms(x_ref[0] + gate * y, fg_ref[...])


def _outproj(om, osw, x, mod3, w_m, w_s, final_gain, tm):
    b, s, d = x.shape
    const = lambda shape: pl.BlockSpec(shape, lambda bi, i: (0,) * len(shape))
    tok = lambda w: pl.BlockSpec((1, tm, w), lambda bi, i: (bi, i, 0))
    return pl.pallas_call(
        _outproj_kernel,
        grid=(b, s // tm),
        in_specs=[tok(MLA_WIDTH), tok(SWA_WIDTH), tok(d),
                  pl.BlockSpec((1, 3, d), lambda bi, i: (bi, 0, 0)),
                  const(w_m.shape), const(w_s.shape), const((1, d))],
        out_specs=tok(d),
        out_shape=jax.ShapeDtypeStruct((b, s, d), x.dtype),
        compiler_params=pltpu.CompilerParams(dimension_semantics=("parallel", "parallel"),
                                             vmem_limit_bytes=VMEM_LIMIT),
        name="outproj",
    )(om, osw, x, mod3, w_m, w_s, final_gain)


def kernel(x, c, positions, w_ada, b_ada, norm_gain, w_in, q_norm_gain, kv_norm_gain, w_uq, w_ukv, swa_sinks, w_out, final_gain):
    b, s, d = x.shape
    depth = w_ada.shape[0]
    assert depth == 1, "the final rmsnorm is fused into the (single) layer's output projection"
    pos = positions.astype(F32)
    pos_col = pos.reshape(b, s, 1)
    pos_row = pos.reshape(b, s // WINDOW, 1, WINDOW)
    for l in range(depth):
        mod3 = _adaln(c, w_ada[l], b_ada[l]).reshape(b, 3, d)
        w_in_p, w_uq_p, w_uk_p, w_uv_p = _pack_weights(w_in[l], w_uq[l], w_ukv[l])
        q, k, v, gm, qs, ks, vs, gs = _inproj(
            x, mod3, pos_col, norm_gain[l].reshape(1, d), q_norm_gain[l].reshape(1, Q_LORA),
            kv_norm_gain[l].reshape(1, KV_LORA), w_in_p, w_uq_p, w_uk_p, w_uv_p, tm=512)
        om = _mla(q, k, v, gm, tq=512)
        osw = _swa(swa_sinks[l].astype(F32), qs, ks, vs, gs, pos_col, pos_row, tq=512)
        w_o = w_out[l].astype(BF16)
        x = _outproj(om, osw, x, mod3, w_o[:MLA_WIDTH], w_o[MLA_WIDTH:], final_gain.reshape(1, d), tm=512)
    return x
```

```python
import functools
import math

import jax
import jax.numpy as jnp
import numpy as np
from jax import lax
from jax.experimental import pallas as pl
from jax.experimental.pallas import tpu as pltpu

F32 = jnp.float32
BF16 = jnp.bfloat16

D_MODEL = 1024
MLA_HEADS = 8
MLA_NOPE = 64
MLA_ROPE = 32
MLA_V = 64
Q_LORA = 384
KV_LORA = 256
MLA_WIDTH = MLA_HEADS * MLA_V
SWA_HEADS = 8
SWA_KV_HEADS = 2
SWA_HEAD_DIM = 64
SWA_GROUP = SWA_HEADS // SWA_KV_HEADS
SWA_WIDTH = SWA_HEADS * SWA_HEAD_DIM
SWA_KV_WIDTH = SWA_KV_HEADS * SWA_HEAD_DIM
WINDOW = 128
ROPE_THETA = 10000.0
EPS = 1e-6
ALIBI_MAX_EXP = 8.0

LANES = 128
HEAD_PAD = LANES
LOG2E = math.log2(math.e)
NEG_BIG = -1e30

C_ZQ = 0
C_ZKV = C_ZQ + Q_LORA
C_KR = C_ZKV + KV_LORA
C_GM = C_KR + LANES
C_QS = C_GM + MLA_WIDTH
C_KS = C_QS + SWA_WIDTH
C_VS = C_KS + SWA_KV_WIDTH
C_GS = C_VS + SWA_KV_WIDTH
D_IN_PACKED = C_GS + SWA_WIDTH

VMEM_LIMIT = 56 * 1024 * 1024


def _swap_halves(w):
    half = w.shape[-1] // 2
    return jnp.concatenate([w[..., half:], w[..., :half]], axis=-1)


def _pack_weights(w_in, w_uq, w_ukv):
    d = w_in.shape[0]
    s = np.cumsum([0, Q_LORA, KV_LORA, MLA_ROPE, MLA_WIDTH, SWA_WIDTH, SWA_KV_WIDTH, SWA_KV_WIDTH, SWA_WIDTH])
    zq, zkv, kr, gm, qs, ks, vs, gs = [w_in[:, s[i]:s[i + 1]] for i in range(8)]
    kr_blk = jnp.concatenate([jnp.zeros((d, MLA_NOPE), w_in.dtype), kr, _swap_halves(kr)], axis=1)
    w_in_p = jnp.concatenate([zq, zkv, kr_blk, gm, qs, ks, vs, gs], axis=1).astype(BF16)

    uq = w_uq.reshape(Q_LORA, MLA_HEADS, MLA_NOPE + MLA_ROPE)
    uq_p = jnp.concatenate([uq, _swap_halves(uq[..., MLA_NOPE:])], axis=-1)
    w_uq_p = uq_p.reshape(Q_LORA, MLA_HEADS * HEAD_PAD).astype(BF16)

    ukv = w_ukv.reshape(KV_LORA, MLA_HEADS, MLA_NOPE + MLA_V)
    uk = jnp.concatenate([ukv[..., :MLA_NOPE], jnp.zeros((KV_LORA, MLA_HEADS, HEAD_PAD - MLA_NOPE), w_ukv.dtype)], axis=-1)
    w_uk_p = uk.reshape(KV_LORA, MLA_HEADS * HEAD_PAD).astype(BF16)
    w_uv_p = ukv[..., MLA_NOPE:].reshape(KV_LORA, MLA_WIDTH).astype(BF16)
    return w_in_p, w_uq_p, w_uk_p, w_uv_p


def _adaln_kernel(c_ref, w_ref, b_ref, o_ref):
    c = c_ref[...]
    a = c * jax.nn.sigmoid(c)
    a_hi = a.astype(BF16)
    a_lo = (a - a_hi.astype(F32)).astype(BF16)
    w = w_ref[...]
    w_hi = w.astype(BF16)
    w_lo = (w - w_hi.astype(F32)).astype(BF16)
    acc = jnp.dot(a_hi, w_hi, preferred_element_type=F32)
    acc += jnp.dot(a_hi, w_lo, preferred_element_type=F32)
    acc += jnp.dot(a_lo, w_hi, preferred_element_type=F32)
    o_ref[...] = acc + b_ref[...]


def _adaln(c, w_ada, b_ada):
    b, d = c.shape
    n = w_ada.shape[1]
    tn = 1024
    return pl.pallas_call(
        _adaln_kernel,
        grid=(n // tn,),
        in_specs=[pl.BlockSpec((b, d), lambda j: (0, 0)),
                  pl.BlockSpec((d, tn), lambda j: (0, j)),
                  pl.BlockSpec((1, tn), lambda j: (0, j))],
        out_specs=pl.BlockSpec((b, tn), lambda j: (0, j)),
        out_shape=jax.ShapeDtypeStruct((b, n), F32),
        compiler_params=pltpu.CompilerParams(dimension_semantics=("arbitrary",), vmem_limit_bytes=VMEM_LIMIT),
        name="adaln",
    )(c, w_ada, b_ada.reshape(1, n))


def _rms(t, gain):
    return t * lax.rsqrt(jnp.mean(t * t, axis=-1, keepdims=True) + EPS) * gain


def _rope_tables(pos):
    lane = lax.broadcasted_iota(jnp.int32, (1, LANES), 1)
    pair = (lane % (MLA_ROPE // 2)).astype(F32)
    inv = jnp.exp(pair * (-2.0 * math.log(ROPE_THETA) / MLA_ROPE))
    ang = pos * inv
    cos, sin = jnp.cos(ang), jnp.sin(ang)
    in_pe = (lane >= MLA_NOPE) & (lane < MLA_NOPE + MLA_ROPE)
    first_half = lane < MLA_NOPE + MLA_ROPE // 2
    a = jnp.where(lane < MLA_NOPE, 1.0, jnp.where(in_pe, cos, 0.0))
    b = jnp.where(in_pe, jnp.where(first_half, -sin, sin), 0.0)
    return a, b


def _rope_group(t, a, b):
    return t * a + pltpu.roll(t, LANES - MLA_ROPE, 1) * b


def _dup_halves(t):
    lane = lax.broadcasted_iota(jnp.int32, (1, LANES), 1)
    r = pltpu.roll(t, LANES // 2, 1)
    lo = lane < LANES // 2
    return jnp.concatenate([jnp.where(lo, t, r), jnp.where(lo, r, t)], axis=1)


def _inproj_kernel(x_ref, mod_ref, pos_ref, ng_ref, qg_ref, kvg_ref, win_ref, wuq_ref, wuk_ref, wuv_ref,
                   q_ref, k_ref, v_ref, gm_ref, qs_ref, ks_ref, vs_ref, gs_ref):
    x = x_ref[0]
    mod = mod_ref[0]
    shift, scale = mod[0:1], mod[1:2]
    h = _rms(x, ng_ref[...]) * (1.0 + scale) + shift
    hb = h.astype(BF16)

    def seg(lo, hi):
        return jnp.dot(hb, win_ref[:, lo:hi], preferred_element_type=F32)

    a, b = _rope_tables(pos_ref[0])
    q_scale = (MLA_NOPE + MLA_ROPE) ** -0.5 * LOG2E

    qn = _rms(seg(C_ZQ, C_ZKV), qg_ref[...]).astype(BF16)
    q = jnp.dot(qn, wuq_ref[...], preferred_element_type=F32)
    aq, bq = a * q_scale, b * q_scale
    for hd in range(MLA_HEADS):
        sl = slice(hd * HEAD_PAD, (hd + 1) * HEAD_PAD)
        q_ref[0, :, sl] = _rope_group(q[:, sl], aq, bq).astype(BF16)

    kvn = _rms(seg(C_ZKV, C_KR), kvg_ref[...]).astype(BF16)
    kpe = _rope_group(seg(C_KR, C_GM), a, b)
    kn = jnp.dot(kvn, wuk_ref[...], preferred_element_type=F32)
    for hd in range(MLA_HEADS):
        sl = slice(hd * HEAD_PAD, (hd + 1) * HEAD_PAD)
        k_ref[0, :, sl] = (kn[:, sl] + kpe).astype(BF16)
    v_ref[0] = jnp.dot(kvn, wuv_ref[...], preferred_element_type=F32).astype(BF16)

    g = seg(C_GM, C_QS)
    gm_ref[0] = (g * jax.nn.sigmoid(g)).astype(BF16)
    qs_ref[0] = (seg(C_QS, C_KS) * (SWA_HEAD_DIM ** -0.5 * LOG2E)).astype(BF16)
    ks_ref[0] = _dup_halves(seg(C_KS, C_VS)).astype(BF16)
    vs_ref[0] = _dup_halves(seg(C_VS, C_GS)).astype(BF16)
    g = seg(C_GS, D_IN_PACKED)
    gs_ref[0] = (g * jax.nn.sigmoid(g)).astype(BF16)


def _inproj(x, mod3, pos_col, norm_gain, q_gain, kv_gain, w_in_p, w_uq_p, w_uk_p, w_uv_p, tm):
    b, s, d = x.shape
    const = lambda shape: pl.BlockSpec(shape, lambda bi, i: (0,) * len(shape))
    tok = lambda w: pl.BlockSpec((1, tm, w), lambda bi, i: (bi, i, 0))
    widths = [MLA_HEADS * HEAD_PAD, MLA_HEADS * HEAD_PAD, MLA_WIDTH, MLA_WIDTH,
              SWA_WIDTH, 2 * SWA_KV_WIDTH, 2 * SWA_KV_WIDTH, SWA_WIDTH]
    return pl.pallas_call(
        _inproj_kernel,
        grid=(b, s // tm),
        in_specs=[tok(d),
                  pl.BlockSpec((1, 3, d), lambda bi, i: (bi, 0, 0)),
                  tok(1),
                  const((1, d)), const((1, Q_LORA)), const((1, KV_LORA)),
                  const(w_in_p.shape), const(w_uq_p.shape), const(w_uk_p.shape), const(w_uv_p.shape)],
        out_specs=[tok(w) for w in widths],
        out_shape=[jax.ShapeDtypeStruct((b, s, w), BF16) for w in widths],
        compiler_params=pltpu.CompilerParams(dimension_semantics=("parallel", "parallel"),
                                             vmem_limit_bytes=VMEM_LIMIT),
        name="inproj",
    )(x, mod3, pos_col, norm_gain, q_gain, kv_gain, w_in_p, w_uq_p, w_uk_p, w_uv_p)


def _dot_nt(a, b):
    return lax.dot_general(a, b, (((1,), (1,)), ((), ())), preferred_element_type=F32)


def _mla_kernel(q_ref, k_ref, v_ref, g_ref, o_ref, *, tq):
    i = pl.program_id(2)
    q = q_ref[0]
    qa, qb = q[:, :HEAD_PAD], q[:, HEAD_PAD:]
    lane = lax.broadcasted_iota(jnp.int32, (1, LANES), 1)
    first = lane < MLA_V

    def scores(start):
        k = k_ref[0, pl.ds(start, tq), :]
        return _dot_nt(qa, k[:, :HEAD_PAD]), _dot_nt(qb, k[:, HEAD_PAD:])

    def pv(pa, pb, start):
        v = v_ref[0, pl.ds(start, tq), :]
        oa = jnp.dot(pa.astype(BF16), v, preferred_element_type=F32)
        ob = jnp.dot(pb.astype(BF16), v, preferred_element_type=F32)
        return jnp.where(first, oa, ob)

    d0 = pl.multiple_of(i * tq, tq)
    sa, sb = scores(d0)
    row = lax.broadcasted_iota(jnp.int32, (tq, tq), 0)
    col = lax.broadcasted_iota(jnp.int32, (tq, tq), 1)
    causal = col <= row
    sa = jnp.where(causal, sa, NEG_BIG)
    sb = jnp.where(causal, sb, NEG_BIG)
    ma = jnp.max(sa, axis=-1, keepdims=True)
    mb = jnp.max(sb, axis=-1, keepdims=True)
    pa = jnp.exp2(sa - ma)
    pb = jnp.exp2(sb - mb)
    la = jnp.sum(pa, axis=-1, keepdims=True)
    lb = jnp.sum(pb, axis=-1, keepdims=True)
    acc = pv(pa, pb, d0)

    def body(kb, carry):
        ma, la, mb, lb, acc = carry
        start = pl.multiple_of(kb * tq, tq)
        sa, sb = scores(start)
        ma_n = jnp.maximum(ma, jnp.max(sa, axis=-1, keepdims=True))
        mb_n = jnp.maximum(mb, jnp.max(sb, axis=-1, keepdims=True))
        al_a = jnp.exp2(ma - ma_n)
        al_b = jnp.exp2(mb - mb_n)
        pa = jnp.exp2(sa - ma_n)
        pb = jnp.exp2(sb - mb_n)
        la = la * al_a + jnp.sum(pa, axis=-1, keepdims=True)
        lb = lb * al_b + jnp.sum(pb, axis=-1, keepdims=True)
        acc = acc * jnp.where(first, al_a, al_b) + pv(pa, pb, start)
        return ma_n, la, mb_n, lb, acc

    ma, la, mb, lb, acc = lax.fori_loop(0, i, body, (ma, la, mb, lb, acc))
    o = acc * jnp.where(first, 1.0 / la, 1.0 / lb)
    o_ref[0] = (o * g_ref[0].astype(F32)).astype(BF16)


def _mla(q, k, v, gm, tq):
    b, s, _ = q.shape
    pairs = MLA_HEADS // 2
    return pl.pallas_call(
        functools.partial(_mla_kernel, tq=tq),
        grid=(b, pairs, s // tq),
        in_specs=[pl.BlockSpec((1, tq, 2 * HEAD_PAD), lambda bi, j, i: (bi, i, j)),
                  pl.BlockSpec((1, s, 2 * HEAD_PAD), lambda bi, j, i: (bi, 0, j)),
                  pl.BlockSpec((1, s, 2 * MLA_V), lambda bi, j, i: (bi, 0, j)),
                  pl.BlockSpec((1, tq, 2 * MLA_V), lambda bi, j, i: (bi, i, j))],
        out_specs=pl.BlockSpec((1, tq, 2 * MLA_V), lambda bi, j, i: (bi, i, j)),
        out_shape=jax.ShapeDtypeStruct((b, s, MLA_WIDTH), BF16),
        compiler_params=pltpu.CompilerParams(dimension_semantics=("parallel", "parallel", "arbitrary"),
                                             vmem_limit_bytes=VMEM_LIMIT),
        name="mla",
    )(q, k, v, gm)


def _swa_kernel(sink_ref, q_ref, k_ref, v_ref, g_ref, pc_ref, pr_ref, o_ref, *, tq):
    i = pl.program_id(1)
    nblk = tq // WINDOW
    rows4 = SWA_GROUP * WINDOW
    lane = lax.broadcasted_iota(jnp.int32, (1, LANES), 1)
    first = lane < SWA_HEAD_DIM
    row = lax.broadcasted_iota(jnp.int32, (rows4, WINDOW), 0)
    col = lax.broadcasted_iota(jnp.int32, (rows4, WINDOW), 1)
    in_cur = col <= (row & (WINDOW - 1))
    grp = lax.broadcasted_iota(jnp.int32, (rows4, 1), 0) // WINDOW

    def per_group(vals):
        out = jnp.full((rows4, 1), vals[SWA_GROUP - 1], F32)
        for g in range(SWA_GROUP - 2, -1, -1):
            out = jnp.where(grp == g, vals[g], out)
        return out

    def unit(nl, kvh, has_prev):
        n = i * nblk + nl
        rows = slice(nl * WINDOW, (nl + 1) * WINDOW)
        ksl = slice(kvh * LANES, (kvh + 1) * LANES)
        heads = [kvh * SWA_GROUP + g for g in range(SWA_GROUP)]
        slope = per_group([LOG2E * 2.0 ** (-ALIBI_MAX_EXP * (h + 1) / SWA_HEADS) for h in heads])
        sink = per_group([sink_ref[h] * LOG2E for h in heads])
        pair_lanes = [slice((heads[0] // 2 + pr) * LANES, (heads[0] // 2 + pr + 1) * LANES)
                      for pr in range(SWA_GROUP // 2)]
        parts = []
        for psl in pair_lanes:
            q2 = q_ref[0, rows, psl]
            zq = jnp.zeros_like(q2)
            parts += [jnp.where(first, q2, zq), jnp.where(first, zq, q2)]
        q4 = jnp.concatenate(parts, axis=0)
        pq = jnp.concatenate([pc_ref[0, rows, :]] * SWA_GROUP, axis=0)
        if has_prev:
            st = pl.multiple_of((n - 1) * WINDOW, WINDOW)
            s2 = _dot_nt(q4, k_ref[0, pl.ds(st, 2 * WINDOW), ksl])
            s = jnp.where(in_cur, s2[:, WINDOW:], s2[:, :WINDOW])
            pk = jnp.where(in_cur, pr_ref[0, n], pr_ref[0, n - 1])
            s = s - slope * (pq - pk)
        else:
            st = pl.multiple_of(n * WINDOW, WINDOW)
            s = _dot_nt(q4, k_ref[0, pl.ds(st, WINDOW), ksl])
            s = jnp.where(in_cur, s - slope * (pq - pr_ref[0, n]), NEG_BIG)
        m = jnp.maximum(jnp.max(s, axis=-1, keepdims=True), sink)
        p = jnp.exp2(s - m)
        l = jnp.sum(p, axis=-1, keepdims=True) + jnp.exp2(sink - m)
        if has_prev:
            zp = jnp.zeros_like(p)
            p2 = jnp.concatenate([jnp.where(in_cur, zp, p), jnp.where(in_cur, p, zp)], axis=1).astype(BF16)
            o4 = jnp.dot(p2, v_ref[0, pl.ds(st, 2 * WINDOW), ksl], preferred_element_type=F32)
        else:
            o4 = jnp.dot(p.astype(BF16), v_ref[0, pl.ds(st, WINDOW), ksl], preferred_element_type=F32)
        o4 = o4 * (1.0 / l)
        for pr, psl in enumerate(pair_lanes):
            o2 = jnp.where(first, o4[2 * pr * WINDOW:(2 * pr + 1) * WINDOW],
                           o4[(2 * pr + 1) * WINDOW:(2 * pr + 2) * WINDOW])
            o_ref[0, rows, psl] = (o2 * g_ref[0, rows, psl].astype(F32)).astype(BF16)

    for kvh in range(SWA_KV_HEADS):
        pl.when(i == 0)(functools.partial(unit, 0, kvh, False))
        pl.when(i > 0)(functools.partial(unit, 0, kvh, True))
        for nl in range(1, nblk):
            unit(nl, kvh, True)


def _swa(sinks, qs, ks, vs, gs, pos_col, pos_row, tq):
    b, s, _ = qs.shape
    tok = lambda w: pl.BlockSpec((1, tq, w), lambda bi, i: (bi, i, 0))
    full = lambda w: pl.BlockSpec((1, s, w), lambda bi, i: (bi, 0, 0))
    return pl.pallas_call(
        functools.partial(_swa_kernel, tq=tq),
        grid=(b, s // tq),
        in_specs=[pl.BlockSpec(memory_space=pltpu.SMEM),
                  tok(SWA_WIDTH), full(2 * SWA_KV_WIDTH), full(2 * SWA_KV_WIDTH), tok(SWA_WIDTH), tok(1),
                  pl.BlockSpec((1, s // WINDOW, 1, WINDOW), lambda bi, i: (bi, 0, 0, 0))],
        out_specs=tok(SWA_WIDTH),
        out_shape=jax.ShapeDtypeStruct((b, s, SWA_WIDTH), BF16),
        compiler_params=pltpu.CompilerParams(dimension_semantics=("parallel", "arbitrary"),
                                             vmem_limit_bytes=VMEM_LIMIT),
        name="swa",
    )(sinks, qs, ks, vs, gs, pos_col, pos_row)


def _outproj_kernel(om_ref, os_ref, x_ref, mod_ref, wm_ref, ws_ref, fg_ref, o_ref):
    y = jnp.dot(om_ref[0], wm_ref[...], preferred_element_type=F32)
    y += jnp.dot(os_ref[0], ws_ref[...], preferred_element_type=F32)
    gate = mod_ref[0][2:3]
    o_ref[0] = _rms(x_ref[0] + gate * y, fg_ref[...])


def _outproj(om, osw, x, mod3, w_m, w_s, final_gain, tm):
    b, s, d = x.shape
    const = lambda shape: pl.BlockSpec(shape, lambda bi, i: (0,) * len(shape))
    tok = lambda w: pl.BlockSpec((1, tm, w), lambda bi, i: (bi, i, 0))
    return pl.pallas_call(
        _outproj_kernel,
        grid=(b, s // tm),
        in_specs=[tok(MLA_WIDTH), tok(SWA_WIDTH), tok(d),
                  pl.BlockSpec((1, 3, d), lambda bi, i: (bi, 0, 0)),
                  const(w_m.shape), const(w_s.shape), const((1, d))],
        out_specs=tok(d),
        out_shape=jax.ShapeDtypeStruct((b, s, d), x.dtype),
        compiler_params=pltpu.CompilerParams(dimension_semantics=("parallel", "parallel"),
                                             vmem_limit_bytes=VMEM_LIMIT),
        name="outproj",
    )(om, osw, x, mod3, w_m, w_s, final_gain)


def kernel(x, c, positions, w_ada, b_ada, norm_gain, w_in, q_norm_gain, kv_norm_gain, w_uq, w_ukv, swa_sinks, w_out, final_gain):
    b, s, d = x.shape
    depth = w_ada.shape[0]
    assert depth == 1, "the final rmsnorm is fused into the (single) layer's output projection"
    pos = positions.astype(F32)
    pos_col = pos.reshape(b, s, 1)
    pos_row = pos.reshape(b, s // WINDOW, 1, WINDOW)
    for l in range(depth):
        mod3 = _adaln(c, w_ada[l], b_ada[l]).reshape(b, 3, d)
        w_in_p, w_uq_p, w_uk_p, w_uv_p = _pack_weights(w_in[l], w_uq[l], w_ukv[l])
        q, k, v, gm, qs, ks, vs, gs = _inproj(
            x, mod3, pos_col, norm_gain[l].reshape(1, d), q_norm_gain[l].reshape(1, Q_LORA),
            kv_norm_gain[l].reshape(1, KV_LORA), w_in_p, w_uq_p, w_uk_p, w_uv_p, tm=512)
        om = _mla(q, k, v, gm, tq=512)
        osw = _swa(swa_sinks[l].astype(F32), qs, ks, vs, gs, pos_col, pos_row, tq=512)
        w_o = w_out[l].astype(BF16)
        x = _outproj(om, osw, x, mod3, w_o[:MLA_WIDTH], w_o[MLA_WIDTH:], final_gain.reshape(1, d), tm=512)
    return x
```

```python
import functools
import math

import jax
import jax.numpy as jnp
import numpy as np
from jax import lax
from jax.experimental import pallas as pl
from jax.experimental.pallas import tpu as pltpu

F32 = jnp.float32
BF16 = jnp.bfloat16

D_MODEL = 1024
MLA_HEADS = 8
MLA_NOPE = 64
MLA_ROPE = 32
MLA_V = 64
Q_LORA = 384
KV_LORA = 256
MLA_WIDTH = MLA_HEADS * MLA_V
SWA_HEADS = 8
SWA_KV_HEADS = 2
SWA_HEAD_DIM = 64
SWA_GROUP = SWA_HEADS // SWA_KV_HEADS
SWA_WIDTH = SWA_HEADS * SWA_HEAD_DIM
SWA_KV_WIDTH = SWA_KV_HEADS * SWA_HEAD_DIM
WINDOW = 128
ROPE_THETA = 10000.0
EPS = 1e-6
ALIBI_MAX_EXP = 8.0

LANES = 128
HEAD_PAD = LANES
LOG2E = math.log2(math.e)
NEG_BIG = -1e30

C_ZQ = 0
C_ZKV = C_ZQ + Q_LORA
C_KR = C_ZKV + KV_LORA
C_GM = C_KR + LANES
C_QS = C_GM + MLA_WIDTH
C_KS = C_QS + SWA_WIDTH
C_VS = C_KS + SWA_KV_WIDTH
C_GS = C_VS + SWA_KV_WIDTH
D_IN_PACKED = C_GS + SWA_WIDTH

VMEM_LIMIT = 56 * 1024 * 1024


def _swap_halves(w):
    half = w.shape[-1] // 2
    return jnp.concatenate([w[..., half:], w[..., :half]], axis=-1)


def _pack_weights(w_in, w_uq, w_ukv):
    d = w_in.shape[0]
    s = np.cumsum([0, Q_LORA, KV_LORA, MLA_ROPE, MLA_WIDTH, SWA_WIDTH, SWA_KV_WIDTH, SWA_KV_WIDTH, SWA_WIDTH])
    zq, zkv, kr, gm, qs, ks, vs, gs = [w_in[:, s[i]:s[i + 1]] for i in range(8)]
    kr_blk = jnp.concatenate([jnp.zeros((d, MLA_NOPE), w_in.dtype), kr, _swap_halves(kr)], axis=1)
    w_in_p = jnp.concatenate([zq, zkv, kr_blk, gm, qs, ks, vs, gs], axis=1).astype(BF16)

    uq = w_uq.reshape(Q_LORA, MLA_HEADS, MLA_NOPE + MLA_ROPE)
    uq_p = jnp.concatenate([uq, _swap_halves(uq[..., MLA_NOPE:])], axis=-1)
    w_uq_p = uq_p.reshape(Q_LORA, MLA_HEADS * HEAD_PAD).astype(BF16)

    ukv = w_ukv.reshape(KV_LORA, MLA_HEADS, MLA_NOPE + MLA_V)
    uk = jnp.concatenate([ukv[..., :MLA_NOPE], jnp.zeros((KV_LORA, MLA_HEADS, HEAD_PAD - MLA_NOPE), w_ukv.dtype)], axis=-1)
    w_uk_p = uk.reshape(KV_LORA, MLA_HEADS * HEAD_PAD).astype(BF16)
    w_uv_p = ukv[..., MLA_NOPE:].reshape(KV_LORA, MLA_WIDTH).astype(BF16)
    return w_in_p, w_uq_p, w_uk_p, w_uv_p


def _adaln_kernel(c_ref, w_ref, b_ref, o_ref):
    c = c_ref[...]
    a = c * jax.nn.sigmoid(c)
    a_hi = a.astype(BF16)
    a_lo = (a - a_hi.astype(F32)).astype(BF16)
    w = w_ref[...]
    w_hi = w.astype(BF16)
    w_lo = (w - w_hi.astype(F32)).astype(BF16)
    acc = jnp.dot(a_hi, w_hi, preferred_element_type=F32)
    acc += jnp.dot(a_hi, w_lo, preferred_element_type=F32)
    acc += jnp.dot(a_lo, w_hi, preferred_element_type=F32)
    o_ref[...] = acc + b_ref[...]


def _adaln(c, w_ada, b_ada):
    b, d = c.shape
    n = w_ada.shape[1]
    tn = 1024
    return pl.pallas_call(
        _adaln_kernel,
        grid=(n // tn,),
        in_specs=[pl.BlockSpec((b, d), lambda j: (0, 0)),
                  pl.BlockSpec((d, tn), lambda j: (0, j)),
                  pl.BlockSpec((1, tn), lambda j: (0, j))],
        out_specs=pl.BlockSpec((b, tn), lambda j: (0, j)),
        out_shape=jax.ShapeDtypeStruct((b, n), F32),
        compiler_params=pltpu.CompilerParams(dimension_semantics=("arbitrary",), vmem_limit_bytes=VMEM_LIMIT),
        name="adaln",
    )(c, w_ada, b_ada.reshape(1, n))


def _rms(t, gain):
    return t * lax.rsqrt(jnp.mean(t * t, axis=-1, keepdims=True) + EPS) * gain


def _rope_tables(pos):
    lane = lax.broadcasted_iota(jnp.int32, (1, LANES), 1)
    pair = (lane % (MLA_ROPE // 2)).astype(F32)
    inv = jnp.exp(pair * (-2.0 * math.log(ROPE_THETA) / MLA_ROPE))
    ang = pos * inv
    cos, sin = jnp.cos(ang), jnp.sin(ang)
    in_pe = (lane >= MLA_NOPE) & (lane < MLA_NOPE + MLA_ROPE)
    first_half = lane < MLA_NOPE + MLA_ROPE // 2
    a = jnp.where(lane < MLA_NOPE, 1.0, jnp.where(in_pe, cos, 0.0))
    b = jnp.where(in_pe, jnp.where(first_half, -sin, sin), 0.0)
    return a, b


def _rope_group(t, a, b):
    return t * a + pltpu.roll(t, LANES - MLA_ROPE, 1) * b


def _dup_halves(t):
    lane = lax.broadcasted_iota(jnp.int32, (1, LANES), 1)
    r = pltpu.roll(t, LANES // 2, 1)
    lo = lane < LANES // 2
    return jnp.concatenate([jnp.where(lo, t, r), jnp.where(lo, r, t)], axis=1)


def _inproj_kernel(x_ref, mod_ref, pos_ref, ng_ref, qg_ref, kvg_ref, win_ref, wuq_ref, wuk_ref, wuv_ref,
                   q_ref, k_ref, v_ref, gm_ref, qs_ref, ks_ref, vs_ref, gs_ref):
    x = x_ref[0]
    mod = mod_ref[0]
    shift, scale = mod[0:1], mod[1:2]
    h = _rms(x, ng_ref[...]) * (1.0 + scale) + shift
    hb = h.astype(BF16)

    def seg(lo, hi):
        return jnp.dot(hb, win_ref[:, lo:hi], preferred_element_type=F32)

    a, b = _rope_tables(pos_ref[0])
    q_scale = (MLA_NOPE + MLA_ROPE) ** -0.5 * LOG2E

    qn = _rms(seg(C_ZQ, C_ZKV), qg_ref[...]).astype(BF16)
    q = jnp.dot(qn, wuq_ref[...], preferred_element_type=F32)
    aq, bq = a * q_scale, b * q_scale
    for hd in range(MLA_HEADS):
        sl = slice(hd * HEAD_PAD, (hd + 1) * HEAD_PAD)
        q_ref[0, :, sl] = _rope_group(q[:, sl], aq, bq).astype(BF16)

    kvn = _rms(seg(C_ZKV, C_KR), kvg_ref[...]).astype(BF16)
    kpe = _rope_group(seg(C_KR, C_GM), a, b)
    kn = jnp.dot(kvn, wuk_ref[...], preferred_element_type=F32)
    for hd in range(MLA_HEADS):
        sl = slice(hd * HEAD_PAD, (hd + 1) * HEAD_PAD)
        k_ref[0, :, sl] = (kn[:, sl] + kpe).astype(BF16)
    v_ref[0] = jnp.dot(kvn, wuv_ref[...], preferred_element_type=F32).astype(BF16)

    g = seg(C_GM, C_QS)
    gm_ref[0] = (g * jax.nn.sigmoid(g)).astype(BF16)
    qs_ref[0] = (seg(C_QS, C_KS) * (SWA_HEAD_DIM ** -0.5 * LOG2E)).astype(BF16)
    ks_ref[0] = _dup_halves(seg(C_KS, C_VS)).astype(BF16)
    vs_ref[0] = _dup_halves(seg(C_VS, C_GS)).astype(BF16)
    g = seg(C_GS, D_IN_PACKED)
    gs_ref[0] = (g * jax.nn.sigmoid(g)).astype(BF16)


def _inproj(x, mod3, pos_col, norm_gain, q_gain, kv_gain, w_in_p, w_uq_p, w_uk_p, w_uv_p, tm):
    b, s, d = x.shape
    const = lambda shape: pl.BlockSpec(shape, lambda bi, i: (0,) * len(shape))
    tok = lambda w: pl.BlockSpec((1, tm, w), lambda bi, i: (bi, i, 0))
    widths = [MLA_HEADS * HEAD_PAD, MLA_HEADS * HEAD_PAD, MLA_WIDTH, MLA_WIDTH,
              SWA_WIDTH, 2 * SWA_KV_WIDTH, 2 * SWA_KV_WIDTH, SWA_WIDTH]
    return pl.pallas_call(
        _inproj_kernel,
        grid=(b, s // tm),
        in_specs=[tok(d),
                  pl.BlockSpec((1, 3, d), lambda bi, i: (bi, 0, 0)),
                  tok(1),
                  const((1, d)), const((1, Q_LORA)), const((1, KV_LORA)),
                  const(w_in_p.shape), const(w_uq_p.shape), const(w_uk_p.shape), const(w_uv_p.shape)],
        out_specs=[tok(w) for w in widths],
        out_shape=[jax.ShapeDtypeStruct((b, s, w), BF16) for w in widths],
        compiler_params=pltpu.CompilerParams(dimension_semantics=("parallel", "parallel"),
                                             vmem_limit_bytes=VMEM_LIMIT),
        name="inproj",
    )(x, mod3, pos_col, norm_gain, q_gain, kv_gain, w_in_p, w_uq_p, w_uk_p, w_uv_p)


def _dot_nt(a, b):
    return lax.dot_general(a, b, (((1,), (1,)), ((), ())), preferred_element_type=F32)


def _mla_kernel(q_ref, k_ref, v_ref, g_ref, o_ref, *, tq, nq):
    i = pl.program_id(2)
    lane = lax.broadcasted_iota(jnp.int32, (1, LANES), 1)
    first = lane < MLA_V

    def run(c):
        q = q_ref[0]
        qa, qb = q[:, :HEAD_PAD], q[:, HEAD_PAD:]

        def scores(kb):
            k = k_ref[0, kb * tq:(kb + 1) * tq, :]
            return _dot_nt(qa, k[:, :HEAD_PAD]), _dot_nt(qb, k[:, HEAD_PAD:])

        def pv(pa, pb, kb):
            v = v_ref[0, kb * tq:(kb + 1) * tq, :]
            oa = jnp.dot(pa.astype(BF16), v, preferred_element_type=F32)
            ob = jnp.dot(pb.astype(BF16), v, preferred_element_type=F32)
            return jnp.where(first, oa, ob)

        sa, sb = scores(c)
        row = lax.broadcasted_iota(jnp.int32, (tq, tq), 0)
        col = lax.broadcasted_iota(jnp.int32, (tq, tq), 1)
        causal = col <= row
        sa = jnp.where(causal, sa, NEG_BIG)
        sb = jnp.where(causal, sb, NEG_BIG)
        ma = jnp.max(sa, axis=-1, keepdims=True)
        mb = jnp.max(sb, axis=-1, keepdims=True)
        pa = jnp.exp2(sa - ma)
        pb = jnp.exp2(sb - mb)
        la = jnp.sum(pa, axis=-1, keepdims=True)
        lb = jnp.sum(pb, axis=-1, keepdims=True)
        acc = pv(pa, pb, c)

        for kb in range(c):
            sa, sb = scores(kb)
            ma_n = jnp.maximum(ma, jnp.max(sa, axis=-1, keepdims=True))
            mb_n = jnp.maximum(mb, jnp.max(sb, axis=-1, keepdims=True))
            al_a = jnp.exp2(ma - ma_n)
            al_b = jnp.exp2(mb - mb_n)
            pa = jnp.exp2(sa - ma_n)
            pb = jnp.exp2(sb - mb_n)
            la = la * al_a + jnp.sum(pa, axis=-1, keepdims=True)
            lb = lb * al_b + jnp.sum(pb, axis=-1, keepdims=True)
            acc = acc * jnp.where(first, al_a, al_b) + pv(pa, pb, kb)
            ma, mb = ma_n, mb_n

        o = acc * jnp.where(first, 1.0 / la, 1.0 / lb)
        o_ref[0] = (o * g_ref[0].astype(F32)).astype(BF16)

    for c in range(nq):
        pl.when(i == c)(functools.partial(run, c))


def _mla(q, k, v, gm, tq):
    b, s, _ = q.shape
    pairs = MLA_HEADS // 2
    return pl.pallas_call(
        functools.partial(_mla_kernel, tq=tq, nq=s // tq),
        grid=(b, pairs, s // tq),
        in_specs=[pl.BlockSpec((1, tq, 2 * HEAD_PAD), lambda bi, j, i: (bi, i, j)),
                  pl.BlockSpec((1, s, 2 * HEAD_PAD), lambda bi, j, i: (bi, 0, j)),
                  pl.BlockSpec((1, s, 2 * MLA_V), lambda bi, j, i: (bi, 0, j)),
                  pl.BlockSpec((1, tq, 2 * MLA_V), lambda bi, j, i: (bi, i, j))],
        out_specs=pl.BlockSpec((1, tq, 2 * MLA_V), lambda bi, j, i: (bi, i, j)),
        out_shape=jax.ShapeDtypeStruct((b, s, MLA_WIDTH), BF16),
        compiler_params=pltpu.CompilerParams(dimension_semantics=("parallel", "parallel", "arbitrary"),
                                             vmem_limit_bytes=VMEM_LIMIT),
        name="mla",
    )(q, k, v, gm)


def _swa_kernel(sink_ref, q_ref, k_ref, v_ref, g_ref, pc_ref, pr_ref, o_ref, *, tq):
    i = pl.program_id(1)
    nblk = tq // WINDOW
    rows4 = SWA_GROUP * WINDOW
    lane = lax.broadcasted_iota(jnp.int32, (1, LANES), 1)
    first = lane < SWA_HEAD_DIM
    row = lax.broadcasted_iota(jnp.int32, (rows4, WINDOW), 0)
    col = lax.broadcasted_iota(jnp.int32, (rows4, WINDOW), 1)
    in_cur = col <= (row & (WINDOW - 1))
    grp = lax.broadcasted_iota(jnp.int32, (rows4, 1), 0) // WINDOW

    def per_group(vals):
        out = jnp.full((rows4, 1), vals[SWA_GROUP - 1], F32)
        for g in range(SWA_GROUP - 2, -1, -1):
            out = jnp.where(grp == g, vals[g], out)
        return out

    def unit(nl, kvh, has_prev):
        n = i * nblk + nl
        rows = slice(nl * WINDOW, (nl + 1) * WINDOW)
        ksl = slice(kvh * LANES, (kvh + 1) * LANES)
        heads = [kvh * SWA_GROUP + g for g in range(SWA_GROUP)]
        slope = per_group([LOG2E * 2.0 ** (-ALIBI_MAX_EXP * (h + 1) / SWA_HEADS) for h in heads])
        sink = per_group([sink_ref[h] * LOG2E for h in heads])
        pair_lanes = [slice((heads[0] // 2 + pr) * LANES, (heads[0] // 2 + pr + 1) * LANES)
                      for pr in range(SWA_GROUP // 2)]
        parts = []
        for psl in pair_lanes:
            q2 = q_ref[0, rows, psl]
            zq = jnp.zeros_like(q2)
            parts += [jnp.where(first, q2, zq), jnp.where(first, zq, q2)]
        q4 = jnp.concatenate(parts, axis=0)
        pq = jnp.concatenate([pc_ref[0, rows, :]] * SWA_GROUP, axis=0)
        if has_prev:
            st = pl.multiple_of((n - 1) * WINDOW, WINDOW)
            s2 = _dot_nt(q4, k_ref[0, pl.ds(st, 2 * WINDOW), ksl])
            s = jnp.where(in_cur, s2[:, WINDOW:], s2[:, :WINDOW])
            pk = jnp.where(in_cur, pr_ref[0, n], pr_ref[0, n - 1])
            s = s - slope * (pq - pk)
        else:
            st = pl.multiple_of(n * WINDOW, WINDOW)
            s = _dot_nt(q4, k_ref[0, pl.ds(st, WINDOW), ksl])
            s = jnp.where(in_cur, s - slope * (pq - pr_ref[0, n]), NEG_BIG)
        m = jnp.maximum(jnp.max(s, axis=-1, keepdims=True), sink)
        p = jnp.exp2(s - m)
        l = jnp.sum(p, axis=-1, keepdims=True) + jnp.exp2(sink - m)
        if has_prev:
            zp = jnp.zeros_like(p)
            p2 = jnp.concatenate([jnp.where(in_cur, zp, p), jnp.where(in_cur, p, zp)], axis=1).astype(BF16)
            o4 = jnp.dot(p2, v_ref[0, pl.ds(st, 2 * WINDOW), ksl], preferred_element_type=F32)
        else:
            o4 = jnp.dot(p.astype(BF16), v_ref[0, pl.ds(st, WINDOW), ksl], preferred_element_type=F32)
        o4 = o4 * (1.0 / l)
        for pr, psl in enumerate(pair_lanes):
            o2 = jnp.where(first, o4[2 * pr * WINDOW:(2 * pr + 1) * WINDOW],
                           o4[(2 * pr + 1) * WINDOW:(2 * pr + 2) * WINDOW])
            o_ref[0, rows, psl] = (o2 * g_ref[0, rows, psl].astype(F32)).astype(BF16)

    for kvh in range(SWA_KV_HEADS):
        pl.when(i == 0)(functools.partial(unit, 0, kvh, False))
        pl.when(i > 0)(functools.partial(unit, 0, kvh, True))
        for nl in range(1, nblk):
            unit(nl, kvh, True)


def _swa(sinks, qs, ks, vs, gs, pos_col, pos_row, tq):
    b, s, _ = qs.shape
    tok = lambda w: pl.BlockSpec((1, tq, w), lambda bi, i: (bi, i, 0))
    full = lambda w: pl.BlockSpec((1, s, w), lambda bi, i: (bi, 0, 0))
    return pl.pallas_call(
        functools.partial(_swa_kernel, tq=tq),
        grid=(b, s // tq),
        in_specs=[pl.BlockSpec(memory_space=pltpu.SMEM),
                  tok(SWA_WIDTH), full(2 * SWA_KV_WIDTH), full(2 * SWA_KV_WIDTH), tok(SWA_WIDTH), tok(1),
                  pl.BlockSpec((1, s // WINDOW, 1, WINDOW), lambda bi, i: (bi, 0, 0, 0))],
        out_specs=tok(SWA_WIDTH),
        out_shape=jax.ShapeDtypeStruct((b, s, SWA_WIDTH), BF16),
        compiler_params=pltpu.CompilerParams(dimension_semantics=("parallel", "arbitrary"),
                                             vmem_limit_bytes=VMEM_LIMIT),
        name="swa",
    )(sinks, qs, ks, vs, gs, pos_col, pos_row)


def _outproj_kernel(om_ref, os_ref, x_ref, mod_ref, wm_ref, ws_ref, fg_ref, o_ref):
    y = jnp.dot(om_ref[0], wm_ref[...], preferred_element_type=F32)
    y += jnp.dot(os_ref[0], ws_ref[...], preferred_element_type=F32)
    gate = mod_ref[0][2:3]
    o_ref[0] = _rms(x_ref[0] + gate * y, fg_ref[...])


def _outproj(om, osw, x, mod3, w_m, w_s, final_gain, tm):
    b, s, d = x.shape
    const = lambda shape: pl.BlockSpec(shape, lambda bi, i: (0,) * len(shape))
    tok = lambda w: pl.BlockSpec((1, tm, w), lambda bi, i: (bi, i, 0))
    return pl.pallas_call(
        _outproj_kernel,
        grid=(b, s // tm),
        in_specs=[tok(MLA_WIDTH), tok(SWA_WIDTH), tok(d),
                  pl.BlockSpec((1, 3, d), lambda bi, i: (bi, 0, 0)),
                  const(w_m.shape), const(w_s.shape), const((1, d))],
        out_specs=tok(d),
        out_shape=jax.ShapeDtypeStruct((b, s, d), x.dtype),
        compiler_params=pltpu.CompilerParams(dimension_semantics=("parallel", "parallel"),
                                             vmem_limit_bytes=VMEM_LIMIT),
        name="outproj",
    )(om, osw, x, mod3, w_m, w_s, final_gain)


def kernel(x, c, positions, w_ada, b_ada, norm_gain, w_in, q_norm_gain, kv_norm_gain, w_uq, w_ukv, swa_sinks, w_out, final_gain):
    b, s, d = x.shape
    depth = w_ada.shape[0]
    assert depth == 1, "the final rmsnorm is fused into the (single) layer's output projection"
    pos = positions.astype(F32)
    pos_col = pos.reshape(b, s, 1)
    pos_row = pos.reshape(b, s // WINDOW, 1, WINDOW)
    for l in range(depth):
        mod3 = _adaln(c, w_ada[l], b_ada[l]).reshape(b, 3, d)
        w_in_p, w_uq_p, w_uk_p, w_uv_p = _pack_weights(w_in[l], w_uq[l], w_ukv[l])
        q, k, v, gm, qs, ks, vs, gs = _inproj(
            x, mod3, pos_col, norm_gain[l].reshape(1, d), q_norm_gain[l].reshape(1, Q_LORA),
            kv_norm_gain[l].reshape(1, KV_LORA), w_in_p, w_uq_p, w_uk_p, w_uv_p, tm=512)
        om = _mla(q, k, v, gm, tq=512)
        osw = _swa(swa_sinks[l].astype(F32), qs, ks, vs, gs, pos_col, pos_row, tq=512)
        w_o = w_out[l].astype(BF16)
        x = _outproj(om, osw, x, mod3, w_o[:MLA_WIDTH], w_o[MLA_WIDTH:], final_gain.reshape(1, d), tm=512)
    return x
```

```python
import functools
import math

import jax
import jax.numpy as jnp
import numpy as np
from jax import lax
from jax.experimental import pallas as pl
from jax.experimental.pallas import tpu as pltpu

F32 = jnp.float32
BF16 = jnp.bfloat16

D_MODEL = 1024
MLA_HEADS = 8
MLA_NOPE = 64
MLA_ROPE = 32
MLA_V = 64
Q_LORA = 384
KV_LORA = 256
MLA_WIDTH = MLA_HEADS * MLA_V
SWA_HEADS = 8
SWA_KV_HEADS = 2
SWA_HEAD_DIM = 64
SWA_GROUP = SWA_HEADS // SWA_KV_HEADS
SWA_WIDTH = SWA_HEADS * SWA_HEAD_DIM
SWA_KV_WIDTH = SWA_KV_HEADS * SWA_HEAD_DIM
WINDOW = 128
ROPE_THETA = 10000.0
EPS = 1e-6
ALIBI_MAX_EXP = 8.0

LANES = 128
HEAD_PAD = LANES
LOG2E = math.log2(math.e)
NEG_BIG = -1e30

C_ZQ = 0
C_KR = C_ZQ + Q_LORA
C_ZKV = C_KR + LANES
C_GM = C_ZKV + KV_LORA
C_QS = C_GM + MLA_WIDTH
C_KS = C_QS + SWA_WIDTH
C_VS = C_KS + SWA_KV_WIDTH
C_GS = C_VS + SWA_KV_WIDTH
D_IN_PACKED = C_GS + SWA_WIDTH

VMEM_LIMIT = 56 * 1024 * 1024


def _swap_halves(w):
    half = w.shape[-1] // 2
    return jnp.concatenate([w[..., half:], w[..., :half]], axis=-1)


def _pack_weights(w_in, w_uq, w_ukv):
    d = w_in.shape[0]
    s = np.cumsum([0, Q_LORA, KV_LORA, MLA_ROPE, MLA_WIDTH, SWA_WIDTH, SWA_KV_WIDTH, SWA_KV_WIDTH, SWA_WIDTH])
    zq, zkv, kr, gm, qs, ks, vs, gs = [w_in[:, s[i]:s[i + 1]] for i in range(8)]
    kr_blk = jnp.concatenate([jnp.zeros((d, MLA_NOPE), w_in.dtype), kr, _swap_halves(kr)], axis=1)
    w_in_p = jnp.concatenate([zq, kr_blk, zkv, gm, qs, ks, vs, gs], axis=1).astype(BF16)

    uq = w_uq.reshape(Q_LORA, MLA_HEADS, MLA_NOPE + MLA_ROPE)
    uq_p = jnp.concatenate([uq, _swap_halves(uq[..., MLA_NOPE:])], axis=-1)
    w_uq_p = uq_p.reshape(Q_LORA, MLA_HEADS * HEAD_PAD).astype(BF16)

    ukv = w_ukv.reshape(KV_LORA, MLA_HEADS, MLA_NOPE + MLA_V)
    uk = jnp.concatenate([ukv[..., :MLA_NOPE], jnp.zeros((KV_LORA, MLA_HEADS, HEAD_PAD - MLA_NOPE), w_ukv.dtype)], axis=-1)
    w_ukv_p = jnp.concatenate([uk.reshape(KV_LORA, MLA_HEADS * HEAD_PAD),
                               ukv[..., MLA_NOPE:].reshape(KV_LORA, MLA_WIDTH)], axis=1).astype(BF16)
    return w_in_p, w_uq_p, w_ukv_p


def _adaln_kernel(c_ref, w_ref, b_ref, o_ref):
    c = c_ref[...]
    a = c * jax.nn.sigmoid(c)
    a_hi = a.astype(BF16)
    a_lo = (a - a_hi.astype(F32)).astype(BF16)
    w = w_ref[...]
    w_hi = w.astype(BF16)
    w_lo = (w - w_hi.astype(F32)).astype(BF16)
    acc = jnp.dot(a_hi, w_hi, preferred_element_type=F32)
    acc += jnp.dot(a_hi, w_lo, preferred_element_type=F32)
    acc += jnp.dot(a_lo, w_hi, preferred_element_type=F32)
    o_ref[...] = acc + b_ref[...]


def _adaln(c, w_ada, b_ada):
    b, d = c.shape
    n = w_ada.shape[1]
    tn = 1024
    return pl.pallas_call(
        _adaln_kernel,
        grid=(n // tn,),
        in_specs=[pl.BlockSpec((b, d), lambda j: (0, 0)),
                  pl.BlockSpec((d, tn), lambda j: (0, j)),
                  pl.BlockSpec((1, tn), lambda j: (0, j))],
        out_specs=pl.BlockSpec((b, tn), lambda j: (0, j)),
        out_shape=jax.ShapeDtypeStruct((b, n), F32),
        compiler_params=pltpu.CompilerParams(dimension_semantics=("arbitrary",), vmem_limit_bytes=VMEM_LIMIT),
        name="adaln",
    )(c, w_ada, b_ada.reshape(1, n))


def _rms(t, gain):
    return t * lax.rsqrt(jnp.mean(t * t, axis=-1, keepdims=True) + EPS) * gain


def _rope_tables(pos_row):
    t = pos_row.shape[1]
    half = MLA_ROPE // 2
    idx = lax.broadcasted_iota(jnp.int32, (half, 1), 0).astype(F32)
    inv = jnp.exp(idx * (-2.0 * math.log(ROPE_THETA) / MLA_ROPE))
    ang = inv * pos_row
    cos, sin = jnp.cos(ang), jnp.sin(ang)
    pad = jnp.zeros((HEAD_PAD - MLA_NOPE - MLA_ROPE, t), F32)
    a_t = jnp.concatenate([jnp.ones((MLA_NOPE, t), F32), cos, cos, pad], axis=0)
    b_t = jnp.concatenate([jnp.zeros((MLA_NOPE, t), F32), -sin, sin, pad], axis=0)
    return a_t.T, b_t.T


def _rope_group(t, a, b):
    return t * a + pltpu.roll(t, LANES - MLA_ROPE, 1) * b


def _dup_halves(t):
    lane = lax.broadcasted_iota(jnp.int32, (1, LANES), 1)
    r = pltpu.roll(t, LANES // 2, 1)
    lo = lane < LANES // 2
    return jnp.concatenate([jnp.where(lo, t, r), jnp.where(lo, r, t)], axis=1)


def _inproj_kernel(x_ref, mod_ref, pos_ref, ng_ref, qg_ref, kvg_ref, win_ref, wuq_ref, wukv_ref,
                   q_ref, k_ref, v_ref, gm_ref, qs_ref, ks_ref, vs_ref, gs_ref):
    x = x_ref[0]
    mod = mod_ref[0]
    shift, scale = mod[0:1], mod[1:2]
    h = _rms(x, ng_ref[...]) * (1.0 + scale) + shift
    hb = h.astype(BF16)

    z = jnp.dot(hb, win_ref[...], preferred_element_type=F32)

    def seg(lo, hi):
        return z[:, lo:hi]

    a, b = _rope_tables(pos_ref[0])
    q_scale = (MLA_NOPE + MLA_ROPE) ** -0.5 * LOG2E

    zqr = seg(C_ZQ, C_ZKV)
    qn = _rms(zqr[:, :Q_LORA], qg_ref[...] * q_scale).astype(BF16)
    q = jnp.dot(qn, wuq_ref[...], preferred_element_type=F32)
    for hd in range(MLA_HEADS):
        sl = slice(hd * HEAD_PAD, (hd + 1) * HEAD_PAD)
        q_ref[0, :, sl] = _rope_group(q[:, sl], a, b).astype(BF16)

    kpe = _rope_group(zqr[:, Q_LORA:], a, b)
    kvn = _rms(seg(C_ZKV, C_GM), kvg_ref[...]).astype(BF16)
    kv = jnp.dot(kvn, wukv_ref[...], preferred_element_type=F32)
    for hd in range(MLA_HEADS):
        sl = slice(hd * HEAD_PAD, (hd + 1) * HEAD_PAD)
        k_ref[0, :, sl] = (kv[:, sl] + kpe).astype(BF16)
    v_ref[0] = kv[:, MLA_HEADS * HEAD_PAD:].astype(BF16)

    g = seg(C_GM, C_QS)
    gm_ref[0] = (g * jax.nn.sigmoid(g)).astype(BF16)
    g = seg(C_GS, D_IN_PACKED)
    gs_ref[0] = (g * jax.nn.sigmoid(g)).astype(BF16)
    qs_ref[0] = (seg(C_QS, C_KS) * (SWA_HEAD_DIM ** -0.5 * LOG2E)).astype(BF16)
    kvs = seg(C_KS, C_GS)
    ks_ref[0] = _dup_halves(kvs[:, :SWA_KV_WIDTH]).astype(BF16)
    vs_ref[0] = _dup_halves(kvs[:, SWA_KV_WIDTH:]).astype(BF16)


def _inproj(x, mod3, pos_row, norm_gain, q_gain, kv_gain, w_in_p, w_uq_p, w_ukv_p, tm):
    b, s, d = x.shape
    const = lambda shape: pl.BlockSpec(shape, lambda bi, i: (0,) * len(shape), pipeline_mode=pl.Buffered(1))
    tok = lambda w: pl.BlockSpec((1, tm, w), lambda bi, i: (bi, i, 0))
    widths = [MLA_HEADS * HEAD_PAD, MLA_HEADS * HEAD_PAD, MLA_WIDTH, MLA_WIDTH,
              SWA_WIDTH, 2 * SWA_KV_WIDTH, 2 * SWA_KV_WIDTH, SWA_WIDTH]
    return pl.pallas_call(
        _inproj_kernel,
        grid=(b, s // tm),
        in_specs=[tok(d),
                  pl.BlockSpec((1, 3, d), lambda bi, i: (bi, 0, 0)),
                  pl.BlockSpec((1, 1, tm), lambda bi, i: (bi, 0, i)),
                  const((1, d)), const((1, Q_LORA)), const((1, KV_LORA)),
                  const(w_in_p.shape), const(w_uq_p.shape), const(w_ukv_p.shape)],
        out_specs=[tok(w) for w in widths],
        out_shape=[jax.ShapeDtypeStruct((b, s, w), BF16) for w in widths],
        compiler_params=pltpu.CompilerParams(dimension_semantics=("parallel", "parallel"),
                                             vmem_limit_bytes=VMEM_LIMIT),
        name="inproj",
    )(x, mod3, pos_row, norm_gain, q_gain, kv_gain, w_in_p, w_uq_p, w_ukv_p)


def _dot_nt(a, b):
    return lax.dot_general(a, b, (((1,), (1,)), ((), ())), preferred_element_type=F32)


def _mla_kernel(q_ref, k_ref, v_ref, g_ref, o_ref, *, tq, nq):
    i = pl.program_id(2)
    lane = lax.broadcasted_iota(jnp.int32, (1, LANES), 1)
    first = lane < MLA_V

    def run(c):
        q = q_ref[0]
        qa, qb = q[:, :HEAD_PAD], q[:, HEAD_PAD:]

        def scores(kb):
            k = k_ref[0, kb * tq:(kb + 1) * tq, :]
            return _dot_nt(qa, k[:, :HEAD_PAD]), _dot_nt(qb, k[:, HEAD_PAD:])

        def pv(pa, pb, kb):
            v = v_ref[0, kb * tq:(kb + 1) * tq, :]
            oa = jnp.dot(pa.astype(BF16), v, preferred_element_type=F32)
            ob = jnp.dot(pb.astype(BF16), v, preferred_element_type=F32)
            return jnp.where(first, oa, ob)

        sa, sb = scores(c)
        row = lax.broadcasted_iota(jnp.int32, (tq, tq), 0)
        col = lax.broadcasted_iota(jnp.int32, (tq, tq), 1)
        causal = col <= row
        sa = jnp.where(causal, sa, NEG_BIG)
        sb = jnp.where(causal, sb, NEG_BIG)
        ma = jnp.max(sa, axis=-1, keepdims=True)
        mb = jnp.max(sb, axis=-1, keepdims=True)
        pa = jnp.exp2(sa - ma)
        pb = jnp.exp2(sb - mb)
        la = jnp.sum(pa, axis=-1, keepdims=True)
        lb = jnp.sum(pb, axis=-1, keepdims=True)
        acc = pv(pa, pb, c)

        for kb in range(c):
            sa, sb = scores(kb)
            ma_n = jnp.maximum(ma, jnp.max(sa, axis=-1, keepdims=True))
            mb_n = jnp.maximum(mb, jnp.max(sb, axis=-1, keepdims=True))
            al_a = jnp.exp2(ma - ma_n)
            al_b = jnp.exp2(mb - mb_n)
            pa = jnp.exp2(sa - ma_n)
            pb = jnp.exp2(sb - mb_n)
            la = la * al_a + jnp.sum(pa, axis=-1, keepdims=True)
            lb = lb * al_b + jnp.sum(pb, axis=-1, keepdims=True)
            acc = acc * jnp.where(first, al_a, al_b) + pv(pa, pb, kb)
            ma, mb = ma_n, mb_n

        o = acc * jnp.where(first, 1.0 / la, 1.0 / lb)
        o_ref[0] = (o * g_ref[0].astype(F32)).astype(BF16)

    for c in range(nq):
        pl.when(i == c)(functools.partial(run, c))


def _mla(q, k, v, gm, tq):
    b, s, _ = q.shape
    pairs = MLA_HEADS // 2
    return pl.pallas_call(
        functools.partial(_mla_kernel, tq=tq, nq=s // tq),
        grid=(b, pairs, s // tq),
        in_specs=[pl.BlockSpec((1, tq, 2 * HEAD_PAD), lambda bi, j, i: (bi, i, j)),
                  pl.BlockSpec((1, s, 2 * HEAD_PAD), lambda bi, j, i: (bi, 0, j)),
                  pl.BlockSpec((1, s, 2 * MLA_V), lambda bi, j, i: (bi, 0, j)),
                  pl.BlockSpec((1, tq, 2 * MLA_V), lambda bi, j, i: (bi, i, j))],
        out_specs=pl.BlockSpec((1, tq, 2 * MLA_V), lambda bi, j, i: (bi, i, j)),
        out_shape=jax.ShapeDtypeStruct((b, s, MLA_WIDTH), BF16),
        compiler_params=pltpu.CompilerParams(dimension_semantics=("parallel", "parallel", "arbitrary"),
                                             vmem_limit_bytes=VMEM_LIMIT),
        name="mla",
    )(q, k, v, gm)


def _swa_kernel(sink_ref, q_ref, k_ref, v_ref, g_ref, pc_ref, pr_ref, o_ref, *, tq):
    i = pl.program_id(1)
    nblk = tq // WINDOW
    rows4 = SWA_GROUP * WINDOW
    lane = lax.broadcasted_iota(jnp.int32, (1, LANES), 1)
    first = lane < SWA_HEAD_DIM
    row = lax.broadcasted_iota(jnp.int32, (rows4, WINDOW), 0)
    col = lax.broadcasted_iota(jnp.int32, (rows4, WINDOW), 1)
    in_cur = col <= (row & (WINDOW - 1))
    grp = lax.broadcasted_iota(jnp.int32, (rows4, 1), 0) // WINDOW

    def per_group(vals):
        out = jnp.full((rows4, 1), vals[SWA_GROUP - 1], F32)
        for g in range(SWA_GROUP - 2, -1, -1):
            out = jnp.where(grp == g, vals[g], out)
        return out

    def unit(nl, kvh, has_prev):
        n = i * nblk + nl
        rows = slice(nl * WINDOW, (nl + 1) * WINDOW)
        ksl = slice(kvh * LANES, (kvh + 1) * LANES)
        heads = [kvh * SWA_GROUP + g for g in range(SWA_GROUP)]
        slope = per_group([LOG2E * 2.0 ** (-ALIBI_MAX_EXP * (h + 1) / SWA_HEADS) for h in heads])
        sink = per_group([sink_ref[h] * LOG2E for h in heads])
        pair_lanes = [slice((heads[0] // 2 + pr) * LANES, (heads[0] // 2 + pr + 1) * LANES)
                      for pr in range(SWA_GROUP // 2)]
        parts = []
        for psl in pair_lanes:
            q2 = q_ref[0, rows, psl]
            zq = jnp.zeros_like(q2)
            parts += [jnp.where(first, q2, zq), jnp.where(first, zq, q2)]
        q4 = jnp.concatenate(parts, axis=0)
        pq = jnp.concatenate([pc_ref[0, rows, :]] * SWA_GROUP, axis=0)
        if has_prev:
            st = pl.multiple_of((n - 1) * WINDOW, WINDOW)
            s2 = _dot_nt(q4, k_ref[0, pl.ds(st, 2 * WINDOW), ksl])
            s = jnp.where(in_cur, s2[:, WINDOW:], s2[:, :WINDOW])
            pk = jnp.where(in_cur, pr_ref[0, n], pr_ref[0, n - 1])
            s = s - slope * (pq - pk)
        else:
            st = pl.multiple_of(n * WINDOW, WINDOW)
            s = _dot_nt(q4, k_ref[0, pl.ds(st, WINDOW), ksl])
            s = jnp.where(in_cur, s - slope * (pq - pr_ref[0, n]), NEG_BIG)
        m = jnp.maximum(jnp.max(s, axis=-1, keepdims=True), sink)
        p = jnp.exp2(s - m)
        l = jnp.sum(p, axis=-1, keepdims=True) + jnp.exp2(sink - m)
        if has_prev:
            zp = jnp.zeros_like(p)
            p2 = jnp.concatenate([jnp.where(in_cur, zp, p), jnp.where(in_cur, p, zp)], axis=1).astype(BF16)
            o4 = jnp.dot(p2, v_ref[0, pl.ds(st, 2 * WINDOW), ksl], preferred_element_type=F32)
        else:
            o4 = jnp.dot(p.astype(BF16), v_ref[0, pl.ds(st, WINDOW), ksl], preferred_element_type=F32)
        o4 = o4 * (1.0 / l)
        for pr, psl in enumerate(pair_lanes):
            o2 = jnp.where(first, o4[2 * pr * WINDOW:(2 * pr + 1) * WINDOW],
                           o4[(2 * pr + 1) * WINDOW:(2 * pr + 2) * WINDOW])
            o_ref[0, rows, psl] = (o2 * g_ref[0, rows, psl].astype(F32)).astype(BF16)

    for kvh in range(SWA_KV_HEADS):
        pl.when(i == 0)(functools.partial(unit, 0, kvh, False))
        pl.when(i > 0)(functools.partial(unit, 0, kvh, True))
        for nl in range(1, nblk):
            unit(nl, kvh, True)


def _swa(sinks, qs, ks, vs, gs, pos_col, pos_row, tq):
    b, s, _ = qs.shape
    tok = lambda w: pl.BlockSpec((1, tq, w), lambda bi, i: (bi, i, 0))
    full = lambda w: pl.BlockSpec((1, s, w), lambda bi, i: (bi, 0, 0))
    return pl.pallas_call(
        functools.partial(_swa_kernel, tq=tq),
        grid=(b, s // tq),
        in_specs=[pl.BlockSpec(memory_space=pltpu.SMEM),
                  tok(SWA_WIDTH), full(2 * SWA_KV_WIDTH), full(2 * SWA_KV_WIDTH), tok(SWA_WIDTH), tok(1),
                  pl.BlockSpec((1, s // WINDOW, 1, WINDOW), lambda bi, i: (bi, 0, 0, 0))],
        out_specs=tok(SWA_WIDTH),
        out_shape=jax.ShapeDtypeStruct((b, s, SWA_WIDTH), BF16),
        compiler_params=pltpu.CompilerParams(dimension_semantics=("parallel", "arbitrary"),
                                             vmem_limit_bytes=VMEM_LIMIT),
        name="swa",
    )(sinks, qs, ks, vs, gs, pos_col, pos_row)


def _outproj_kernel(om_ref, os_ref, x_ref, mod_ref, wm_ref, ws_ref, fg_ref, o_ref):
    y = jnp.dot(om_ref[0], wm_ref[...], preferred_element_type=F32)
    y += jnp.dot(os_ref[0], ws_ref[...], preferred_element_type=F32)
    gate = mod_ref[0][2:3]
    o_ref[0] = _rms(x_ref[0] + gate * y, fg_ref[...])


def _outproj(om, osw, x, mod3, w_m, w_s, final_gain, tm):
    b, s, d = x.shape
    const = lambda shape: pl.BlockSpec(shape, lambda bi, i: (0,) * len(shape))
    tok = lambda w: pl.BlockSpec((1, tm, w), lambda bi, i: (bi, i, 0))
    return pl.pallas_call(
        _outproj_kernel,
        grid=(b, s // tm),
        in_specs=[tok(MLA_WIDTH), tok(SWA_WIDTH), tok(d),
                  pl.BlockSpec((1, 3, d), lambda bi, i: (bi, 0, 0)),
                  const(w_m.shape), const(w_s.shape), const((1, d))],
        out_specs=tok(d),
        out_shape=jax.ShapeDtypeStruct((b, s, d), x.dtype),
        compiler_params=pltpu.CompilerParams(dimension_semantics=("parallel", "parallel"),
                                             vmem_limit_bytes=VMEM_LIMIT),
        name="outproj",
    )(om, osw, x, mod3, w_m, w_s, final_gain)


def kernel(x, c, positions, w_ada, b_ada, norm_gain, w_in, q_norm_gain, kv_norm_gain, w_uq, w_ukv, swa_sinks, w_out, final_gain):
    b, s, d = x.shape
    depth = w_ada.shape[0]
    assert depth == 1, "the final rmsnorm is fused into the (single) layer's output projection"
    pos = positions.astype(F32)
    pos_col = pos.reshape(b, s, 1)
    pos_row = pos.reshape(b, s // WINDOW, 1, WINDOW)
    for l in range(depth):
        mod3 = _adaln(c, w_ada[l], b_ada[l]).reshape(b, 3, d)
        w_in_p, w_uq_p, w_ukv_p = _pack_weights(w_in[l], w_uq[l], w_ukv[l])
        q, k, v, gm, qs, ks, vs, gs = _inproj(
            x, mod3, pos.reshape(b, 1, s), norm_gain[l].reshape(1, d), q_norm_gain[l].reshape(1, Q_LORA),
            kv_norm_gain[l].reshape(1, KV_LORA), w_in_p, w_uq_p, w_ukv_p, tm=1024)
        om = _mla(q, k, v, gm, tq=512)
        osw = _swa(swa_sinks[l].astype(F32), qs, ks, vs, gs, pos_col, pos_row, tq=512)
        w_o = w_out[l].astype(BF16)
        x = _outproj(om, osw, x, mod3, w_o[:MLA_WIDTH], w_o[MLA_WIDTH:], final_gain.reshape(1, d), tm=1024)
    return x
```

```python
import functools
import math

import jax
import jax.numpy as jnp
import numpy as np
from jax import lax
from jax.experimental import pallas as pl
from jax.experimental.pallas import tpu as pltpu

F32 = jnp.float32
BF16 = jnp.bfloat16

D_MODEL = 1024
MLA_HEADS = 8
MLA_NOPE = 64
MLA_ROPE = 32
MLA_V = 64
Q_LORA = 384
KV_LORA = 256
MLA_WIDTH = MLA_HEADS * MLA_V
SWA_HEADS = 8
SWA_KV_HEADS = 2
SWA_HEAD_DIM = 64
SWA_GROUP = SWA_HEADS // SWA_KV_HEADS
SWA_WIDTH = SWA_HEADS * SWA_HEAD_DIM
SWA_KV_WIDTH = SWA_KV_HEADS * SWA_HEAD_DIM
WINDOW = 128
ROPE_THETA = 10000.0
EPS = 1e-6
ALIBI_MAX_EXP = 8.0

LANES = 128
HEAD_PAD = LANES
LOG2E = math.log2(math.e)
NEG_BIG = -1e30

C_ZQ = 0
C_KR = C_ZQ + Q_LORA
C_ZKV = C_KR + LANES
C_GM = C_ZKV + KV_LORA
C_QS = C_GM + MLA_WIDTH
C_KS = C_QS + SWA_WIDTH
C_VS = C_KS + SWA_KV_WIDTH
C_GS = C_VS + SWA_KV_WIDTH
D_IN_PACKED = C_GS + SWA_WIDTH

VMEM_LIMIT = 56 * 1024 * 1024


def _swap_halves(w):
    half = w.shape[-1] // 2
    return jnp.concatenate([w[..., half:], w[..., :half]], axis=-1)


def _pack_weights(w_in, w_uq, w_ukv):
    d = w_in.shape[0]
    s = np.cumsum([0, Q_LORA, KV_LORA, MLA_ROPE, MLA_WIDTH, SWA_WIDTH, SWA_KV_WIDTH, SWA_KV_WIDTH, SWA_WIDTH])
    zq, zkv, kr, gm, qs, ks, vs, gs = [w_in[:, s[i]:s[i + 1]] for i in range(8)]
    kr_blk = jnp.concatenate([jnp.zeros((d, MLA_NOPE), w_in.dtype), kr, _swap_halves(kr)], axis=1)
    w_in_p = jnp.concatenate([zq, kr_blk, zkv, gm, qs, ks, vs, gs], axis=1).astype(BF16)

    uq = w_uq.reshape(Q_LORA, MLA_HEADS, MLA_NOPE + MLA_ROPE)
    uq_p = jnp.concatenate([uq, _swap_halves(uq[..., MLA_NOPE:])], axis=-1)
    w_uq_p = uq_p.reshape(Q_LORA, MLA_HEADS * HEAD_PAD).astype(BF16)

    ukv = w_ukv.reshape(KV_LORA, MLA_HEADS, MLA_NOPE + MLA_V)
    uk = jnp.concatenate([ukv[..., :MLA_NOPE], jnp.zeros((KV_LORA, MLA_HEADS, HEAD_PAD - MLA_NOPE), w_ukv.dtype)], axis=-1)
    w_ukv_p = jnp.concatenate([uk.reshape(KV_LORA, MLA_HEADS * HEAD_PAD),
                               ukv[..., MLA_NOPE:].reshape(KV_LORA, MLA_WIDTH)], axis=1).astype(BF16)
    return w_in_p, w_uq_p, w_ukv_p


def _adaln_kernel(c_ref, w_ref, b_ref, o_ref):
    c = c_ref[...]
    a = c * jax.nn.sigmoid(c)
    a_hi = a.astype(BF16)
    a_lo = (a - a_hi.astype(F32)).astype(BF16)
    w = w_ref[...]
    w_hi = w.astype(BF16)
    w_lo = (w - w_hi.astype(F32)).astype(BF16)
    acc = jnp.dot(a_hi, w_hi, preferred_element_type=F32)
    acc += jnp.dot(a_hi, w_lo, preferred_element_type=F32)
    acc += jnp.dot(a_lo, w_hi, preferred_element_type=F32)
    o_ref[...] = acc + b_ref[...]


def _adaln(c, w_ada, b_ada):
    b, d = c.shape
    n = w_ada.shape[1]
    tn = 1024
    return pl.pallas_call(
        _adaln_kernel,
        grid=(n // tn,),
        in_specs=[pl.BlockSpec((b, d), lambda j: (0, 0)),
                  pl.BlockSpec((d, tn), lambda j: (0, j)),
                  pl.BlockSpec((1, tn), lambda j: (0, j))],
        out_specs=pl.BlockSpec((b, tn), lambda j: (0, j)),
        out_shape=jax.ShapeDtypeStruct((b, n), F32),
        compiler_params=pltpu.CompilerParams(dimension_semantics=("arbitrary",), vmem_limit_bytes=VMEM_LIMIT),
        name="adaln",
    )(c, w_ada, b_ada.reshape(1, n))


def _rms(t, gain):
    return t * lax.rsqrt(jnp.mean(t * t, axis=-1, keepdims=True) + EPS) * gain


def _rope_tables(pos_row):
    t = pos_row.shape[1]
    half = MLA_ROPE // 2
    idx = lax.broadcasted_iota(jnp.int32, (half, 1), 0).astype(F32)
    inv = jnp.exp(idx * (-2.0 * math.log(ROPE_THETA) / MLA_ROPE))
    ang = inv * pos_row
    cos, sin = jnp.cos(ang), jnp.sin(ang)
    pad = jnp.zeros((HEAD_PAD - MLA_NOPE - MLA_ROPE, t), F32)
    a_t = jnp.concatenate([jnp.ones((MLA_NOPE, t), F32), cos, cos, pad], axis=0)
    b_t = jnp.concatenate([jnp.zeros((MLA_NOPE, t), F32), -sin, sin, pad], axis=0)
    return a_t.T, b_t.T


def _rope_group(t, a, b):
    return t * a + pltpu.roll(t, LANES - MLA_ROPE, 1) * b


def _dup_halves(t):
    lane = lax.broadcasted_iota(jnp.int32, (1, LANES), 1)
    r = pltpu.roll(t, LANES // 2, 1)
    lo = lane < LANES // 2
    return jnp.concatenate([jnp.where(lo, t, r), jnp.where(lo, r, t)], axis=1)


def _inproj_kernel(x_ref, mod_ref, pos_ref, ng_ref, qg_ref, kvg_ref, win_ref, wuq_ref, wukv_ref,
                   q_ref, k_ref, v_ref, gm_ref, qs_ref, ks_ref, vs_ref, gs_ref):
    x = x_ref[0]
    mod = mod_ref[0]
    shift, scale = mod[0:1], mod[1:2]
    h = _rms(x, ng_ref[...]) * (1.0 + scale) + shift
    hb = h.astype(BF16)

    z = jnp.dot(hb, win_ref[...], preferred_element_type=F32)

    def seg(lo, hi):
        return z[:, lo:hi]

    a, b = _rope_tables(pos_ref[0])
    q_scale = (MLA_NOPE + MLA_ROPE) ** -0.5 * LOG2E

    zqr = seg(C_ZQ, C_ZKV)
    qn = _rms(zqr[:, :Q_LORA], qg_ref[...] * q_scale).astype(BF16)
    q = jnp.dot(qn, wuq_ref[...], preferred_element_type=F32)
    for hd in range(MLA_HEADS):
        sl = slice(hd * HEAD_PAD, (hd + 1) * HEAD_PAD)
        q_ref[0, :, sl] = _rope_group(q[:, sl], a, b).astype(BF16)

    kpe = _rope_group(zqr[:, Q_LORA:], a, b)
    kvn = _rms(seg(C_ZKV, C_GM), kvg_ref[...]).astype(BF16)
    kv = jnp.dot(kvn, wukv_ref[...], preferred_element_type=F32)
    for hd in range(MLA_HEADS):
        sl = slice(hd * HEAD_PAD, (hd + 1) * HEAD_PAD)
        k_ref[0, :, sl] = (kv[:, sl] + kpe).astype(BF16)
    v_ref[0] = kv[:, MLA_HEADS * HEAD_PAD:].astype(BF16)

    g = seg(C_GM, C_QS)
    gm_ref[0] = (g * jax.nn.sigmoid(g)).astype(BF16)
    g = seg(C_GS, D_IN_PACKED)
    gs_ref[0] = (g * jax.nn.sigmoid(g)).astype(BF16)
    qs_ref[0] = (seg(C_QS, C_KS) * (SWA_HEAD_DIM ** -0.5 * LOG2E)).astype(BF16)
    kvs = seg(C_KS, C_GS)
    ks_ref[0] = _dup_halves(kvs[:, :SWA_KV_WIDTH]).astype(BF16)
    vs_ref[0] = _dup_halves(kvs[:, SWA_KV_WIDTH:]).astype(BF16)


def _inproj(x, mod3, pos_row, norm_gain, q_gain, kv_gain, w_in_p, w_uq_p, w_ukv_p, tm):
    b, s, d = x.shape
    const = lambda shape: pl.BlockSpec(shape, lambda bi, i: (0,) * len(shape), pipeline_mode=pl.Buffered(1))
    tok = lambda w: pl.BlockSpec((1, tm, w), lambda bi, i: (bi, i, 0))
    widths = [MLA_HEADS * HEAD_PAD, MLA_HEADS * HEAD_PAD, MLA_WIDTH, MLA_WIDTH,
              SWA_WIDTH, 2 * SWA_KV_WIDTH, 2 * SWA_KV_WIDTH, SWA_WIDTH]
    return pl.pallas_call(
        _inproj_kernel,
        grid=(b, s // tm),
        in_specs=[tok(d),
                  pl.BlockSpec((1, 3, d), lambda bi, i: (bi, 0, 0)),
                  pl.BlockSpec((1, 1, tm), lambda bi, i: (bi, 0, i)),
                  const((1, d)), const((1, Q_LORA)), const((1, KV_LORA)),
                  const(w_in_p.shape), const(w_uq_p.shape), const(w_ukv_p.shape)],
        out_specs=[tok(w) for w in widths],
        out_shape=[jax.ShapeDtypeStruct((b, s, w), BF16) for w in widths],
        compiler_params=pltpu.CompilerParams(dimension_semantics=("parallel", "parallel"),
                                             vmem_limit_bytes=VMEM_LIMIT),
        name="inproj",
    )(x, mod3, pos_row, norm_gain, q_gain, kv_gain, w_in_p, w_uq_p, w_ukv_p)


def _dot_nt(a, b):
    return lax.dot_general(a, b, (((1,), (1,)), ((), ())), preferred_element_type=F32)


def _mla_kernel(q_ref, k_ref, v_ref, g_ref, o_ref, *, tq):
    nq = q_ref.shape[1] // tq
    lane = lax.broadcasted_iota(jnp.int32, (1, LANES), 1)
    first = lane < MLA_V
    row = lax.broadcasted_iota(jnp.int32, (tq, tq), 0)
    col = lax.broadcasted_iota(jnp.int32, (tq, tq), 1)
    causal = col <= row

    def scores(c, kb):
        q = q_ref[0, c * tq:(c + 1) * tq, :]
        k = k_ref[0, kb * tq:(kb + 1) * tq, :]
        return _dot_nt(q[:, :HEAD_PAD], k[:, :HEAD_PAD]), _dot_nt(q[:, HEAD_PAD:], k[:, HEAD_PAD:])

    def pv(pa, pb, kb):
        v = v_ref[0, kb * tq:(kb + 1) * tq, :]
        oa = jnp.dot(pa.astype(BF16), v, preferred_element_type=F32)
        ob = jnp.dot(pb.astype(BF16), v, preferred_element_type=F32)
        return jnp.where(first, oa, ob)

    tiles = [(c, kb) for c in range(nq) for kb in [c] + list(range(c))]
    nxt = scores(*tiles[0])
    for t, (c, kb) in enumerate(tiles):
        sa, sb = nxt
        nxt = scores(*tiles[t + 1]) if t + 1 < len(tiles) else None
        if kb == c:
            sa = jnp.where(causal, sa, NEG_BIG)
            sb = jnp.where(causal, sb, NEG_BIG)
            ma = jnp.max(sa, axis=-1, keepdims=True)
            mb = jnp.max(sb, axis=-1, keepdims=True)
            pa = jnp.exp2(sa - ma)
            pb = jnp.exp2(sb - mb)
            la = jnp.sum(pa, axis=-1, keepdims=True)
            lb = jnp.sum(pb, axis=-1, keepdims=True)
            acc = pv(pa, pb, kb)
        else:
            ma_n = jnp.maximum(ma, jnp.max(sa, axis=-1, keepdims=True))
            mb_n = jnp.maximum(mb, jnp.max(sb, axis=-1, keepdims=True))
            al_a = jnp.exp2(ma - ma_n)
            al_b = jnp.exp2(mb - mb_n)
            pa = jnp.exp2(sa - ma_n)
            pb = jnp.exp2(sb - mb_n)
            la = la * al_a + jnp.sum(pa, axis=-1, keepdims=True)
            lb = lb * al_b + jnp.sum(pb, axis=-1, keepdims=True)
            acc = acc * jnp.where(first, al_a, al_b) + pv(pa, pb, kb)
            ma, mb = ma_n, mb_n
        if t + 1 == len(tiles) or tiles[t + 1][0] != c:
            rows = slice(c * tq, (c + 1) * tq)
            o = acc * jnp.where(first, 1.0 / la, 1.0 / lb)
            o_ref[0, rows, :] = (o * g_ref[0, rows, :].astype(F32)).astype(BF16)


def _mla(q, k, v, gm, tq):
    b, s, _ = q.shape
    pairs = MLA_HEADS // 2
    seq = lambda w: pl.BlockSpec((1, s, w), lambda bi, j: (bi, 0, j))
    return pl.pallas_call(
        functools.partial(_mla_kernel, tq=tq),
        grid=(b, pairs),
        in_specs=[seq(2 * HEAD_PAD), seq(2 * HEAD_PAD), seq(2 * MLA_V), seq(2 * MLA_V)],
        out_specs=seq(2 * MLA_V),
        out_shape=jax.ShapeDtypeStruct((b, s, MLA_WIDTH), BF16),
        compiler_params=pltpu.CompilerParams(dimension_semantics=("parallel", "parallel"),
                                             vmem_limit_bytes=VMEM_LIMIT),
        name="mla",
    )(q, k, v, gm)


def _swa_kernel(sink_ref, q_ref, k_ref, v_ref, g_ref, pc_ref, pr_ref, o_ref, *, tq):
    i = pl.program_id(1)
    nblk = tq // WINDOW
    rows4 = SWA_GROUP * WINDOW
    lane = lax.broadcasted_iota(jnp.int32, (1, LANES), 1)
    first = lane < SWA_HEAD_DIM
    row = lax.broadcasted_iota(jnp.int32, (rows4, WINDOW), 0)
    col = lax.broadcasted_iota(jnp.int32, (rows4, WINDOW), 1)
    in_cur = col <= (row & (WINDOW - 1))
    grp = lax.broadcasted_iota(jnp.int32, (rows4, 1), 0) // WINDOW

    def per_group(vals):
        out = jnp.full((rows4, 1), vals[SWA_GROUP - 1], F32)
        for g in range(SWA_GROUP - 2, -1, -1):
            out = jnp.where(grp == g, vals[g], out)
        return out

    def unit(nl, kvh, has_prev):
        n = i * nblk + nl
        rows = slice(nl * WINDOW, (nl + 1) * WINDOW)
        ksl = slice(kvh * LANES, (kvh + 1) * LANES)
        heads = [kvh * SWA_GROUP + g for g in range(SWA_GROUP)]
        slope = per_group([LOG2E * 2.0 ** (-ALIBI_MAX_EXP * (h + 1) / SWA_HEADS) for h in heads])
        sink = per_group([sink_ref[h] * LOG2E for h in heads])
        pair_lanes = [slice((heads[0] // 2 + pr) * LANES, (heads[0] // 2 + pr + 1) * LANES)
                      for pr in range(SWA_GROUP // 2)]
        parts = []
        for psl in pair_lanes:
            q2 = q_ref[0, rows, psl]
            zq = jnp.zeros_like(q2)
            parts += [jnp.where(first, q2, zq), jnp.where(first, zq, q2)]
        q4 = jnp.concatenate(parts, axis=0)
        pq = jnp.concatenate([pc_ref[0, rows, :]] * SWA_GROUP, axis=0)
        if has_prev:
            st = pl.multiple_of((n - 1) * WINDOW, WINDOW)
            s2 = _dot_nt(q4, k_ref[0, pl.ds(st, 2 * WINDOW), ksl])
            s = jnp.where(in_cur, s2[:, WINDOW:], s2[:, :WINDOW])
            pk = jnp.where(in_cur, pr_ref[0, n], pr_ref[0, n - 1])
            s = s - slope * (pq - pk)
        else:
            st = pl.multiple_of(n * WINDOW, WINDOW)
            s = _dot_nt(q4, k_ref[0, pl.ds(st, WINDOW), ksl])
            s = jnp.where(in_cur, s - slope * (pq - pr_ref[0, n]), NEG_BIG)
        m = jnp.maximum(jnp.max(s, axis=-1, keepdims=True), sink)
        p = jnp.exp2(s - m)
        l = jnp.sum(p, axis=-1, keepdims=True) + jnp.exp2(sink - m)
        if has_prev:
            zp = jnp.zeros_like(p)
            p2 = jnp.concatenate([jnp.where(in_cur, zp, p), jnp.where(in_cur, p, zp)], axis=1).astype(BF16)
            o4 = jnp.dot(p2, v_ref[0, pl.ds(st, 2 * WINDOW), ksl], preferred_element_type=F32)
        else:
            o4 = jnp.dot(p.astype(BF16), v_ref[0, pl.ds(st, WINDOW), ksl], preferred_element_type=F32)
        o4 = o4 * (1.0 / l)
        for pr, psl in enumerate(pair_lanes):
            o2 = jnp.where(first, o4[2 * pr * WINDOW:(2 * pr + 1) * WINDOW],
                           o4[(2 * pr + 1) * WINDOW:(2 * pr + 2) * WINDOW])
            o_ref[0, rows, psl] = (o2 * g_ref[0, rows, psl].astype(F32)).astype(BF16)

    for kvh in range(SWA_KV_HEADS):
        pl.when(i == 0)(functools.partial(unit, 0, kvh, False))
        pl.when(i > 0)(functools.partial(unit, 0, kvh, True))
        for nl in range(1, nblk):
            unit(nl, kvh, True)


def _swa(sinks, qs, ks, vs, gs, pos_col, pos_row, tq):
    b, s, _ = qs.shape
    tok = lambda w: pl.BlockSpec((1, tq, w), lambda bi, i: (bi, i, 0))
    full = lambda w: pl.BlockSpec((1, s, w), lambda bi, i: (bi, 0, 0))
    return pl.pallas_call(
        functools.partial(_swa_kernel, tq=tq),
        grid=(b, s // tq),
        in_specs=[pl.BlockSpec(memory_space=pltpu.SMEM),
                  tok(SWA_WIDTH), full(2 * SWA_KV_WIDTH), full(2 * SWA_KV_WIDTH), tok(SWA_WIDTH), tok(1),
                  pl.BlockSpec((1, s // WINDOW, 1, WINDOW), lambda bi, i: (bi, 0, 0, 0))],
        out_specs=tok(SWA_WIDTH),
        out_shape=jax.ShapeDtypeStruct((b, s, SWA_WIDTH), BF16),
        compiler_params=pltpu.CompilerParams(dimension_semantics=("parallel", "arbitrary"),
                                             vmem_limit_bytes=VMEM_LIMIT),
        name="swa",
    )(sinks, qs, ks, vs, gs, pos_col, pos_row)


def _outproj_kernel(om_ref, os_ref, x_ref, mod_ref, wm_ref, ws_ref, fg_ref, o_ref):
    y = jnp.dot(om_ref[0], wm_ref[...], preferred_element_type=F32)
    y += jnp.dot(os_ref[0], ws_ref[...], preferred_element_type=F32)
    gate = mod_ref[0][2:3]
    o_ref[0] = _rms(x_ref[0] + gate * y, fg_ref[...])


def _outproj(om, osw, x, mod3, w_m, w_s, final_gain, tm):
    b, s, d = x.shape
    const = lambda shape: pl.BlockSpec(shape, lambda bi, i: (0,) * len(shape))
    tok = lambda w: pl.BlockSpec((1, tm, w), lambda bi, i: (bi, i, 0))
    return pl.pallas_call(
        _outproj_kernel,
        grid=(b, s // tm),
        in_specs=[tok(MLA_WIDTH), tok(SWA_WIDTH), tok(d),
                  pl.BlockSpec((1, 3, d), lambda bi, i: (bi, 0, 0)),
                  const(w_m.shape), const(w_s.shape), const((1, d))],
        out_specs=tok(d),
        out_shape=jax.ShapeDtypeStruct((b, s, d), x.dtype),
        compiler_params=pltpu.CompilerParams(dimension_semantics=("parallel", "parallel"),
                                             vmem_limit_bytes=VMEM_LIMIT),
        name="outproj",
    )(om, osw, x, mod3, w_m, w_s, final_gain)


def kernel(x, c, positions, w_ada, b_ada, norm_gain, w_in, q_norm_gain, kv_norm_gain, w_uq, w_ukv, swa_sinks, w_out, final_gain):
    b, s, d = x.shape
    depth = w_ada.shape[0]
    assert depth == 1, "the final rmsnorm is fused into the (single) layer's output projection"
    pos = positions.astype(F32)
    pos_col = pos.reshape(b, s, 1)
    pos_row = pos.reshape(b, s // WINDOW, 1, WINDOW)
    for l in range(depth):
        mod3 = _adaln(c, w_ada[l], b_ada[l]).reshape(b, 3, d)
        w_in_p, w_uq_p, w_ukv_p = _pack_weights(w_in[l], w_uq[l], w_ukv[l])
        q, k, v, gm, qs, ks, vs, gs = _inproj(
            x, mod3, pos.reshape(b, 1, s), norm_gain[l].reshape(1, d), q_norm_gain[l].reshape(1, Q_LORA),
            kv_norm_gain[l].reshape(1, KV_LORA), w_in_p, w_uq_p, w_ukv_p, tm=1024)
        om = _mla(q, k, v, gm, tq=512)
        osw = _swa(swa_sinks[l].astype(F32), qs, ks, vs, gs, pos_col, pos_row, tq=512)
        w_o = w_out[l].astype(BF16)
        x = _outproj(om, osw, x, mod3, w_o[:MLA_WIDTH], w_o[MLA_WIDTH:], final_gain.reshape(1, d), tm=1024)
    return x
```

```python
import functools
import math

import jax
import jax.numpy as jnp
import numpy as np
from jax import lax
from jax.experimental import pallas as pl
from jax.experimental.pallas import tpu as pltpu

F32 = jnp.float32
BF16 = jnp.bfloat16

D_MODEL = 1024
MLA_HEADS = 8
MLA_NOPE = 64
MLA_ROPE = 32
MLA_V = 64
Q_LORA = 384
KV_LORA = 256
MLA_WIDTH = MLA_HEADS * MLA_V
SWA_HEADS = 8
SWA_KV_HEADS = 2
SWA_HEAD_DIM = 64
SWA_GROUP = SWA_HEADS // SWA_KV_HEADS
SWA_WIDTH = SWA_HEADS * SWA_HEAD_DIM
SWA_KV_WIDTH = SWA_KV_HEADS * SWA_HEAD_DIM
WINDOW = 128
ROPE_THETA = 10000.0
EPS = 1e-6
ALIBI_MAX_EXP = 8.0

LANES = 128
HEAD_PAD = LANES
LOG2E = math.log2(math.e)
NEG_BIG = -1e30

C_ZQ = 0
C_KR = C_ZQ + Q_LORA
C_ZKV = C_KR + LANES
C_GM = C_ZKV + KV_LORA
C_QS = C_GM + MLA_WIDTH
C_KS = C_QS + SWA_WIDTH
C_VS = C_KS + SWA_KV_WIDTH
C_GS = C_VS + SWA_KV_WIDTH
D_IN_PACKED = C_GS + SWA_WIDTH

VMEM_LIMIT = 56 * 1024 * 1024


def _swap_halves(w):
    half = w.shape[-1] // 2
    return jnp.concatenate([w[..., half:], w[..., :half]], axis=-1)


def _pack_weights(w_in, w_uq, w_ukv):
    d = w_in.shape[0]
    s = np.cumsum([0, Q_LORA, KV_LORA, MLA_ROPE, MLA_WIDTH, SWA_WIDTH, SWA_KV_WIDTH, SWA_KV_WIDTH, SWA_WIDTH])
    zq, zkv, kr, gm, qs, ks, vs, gs = [w_in[:, s[i]:s[i + 1]] for i in range(8)]
    kr_blk = jnp.concatenate([jnp.zeros((d, MLA_NOPE), w_in.dtype), kr, _swap_halves(kr)], axis=1)
    w_in_p = jnp.concatenate([zq, kr_blk, zkv, gm, qs, ks, vs, gs], axis=1).astype(BF16)

    uq = w_uq.reshape(Q_LORA, MLA_HEADS, MLA_NOPE + MLA_ROPE)
    uq_p = jnp.concatenate([uq, _swap_halves(uq[..., MLA_NOPE:])], axis=-1)
    w_uq_p = uq_p.reshape(Q_LORA, MLA_HEADS * HEAD_PAD).astype(BF16)

    ukv = w_ukv.reshape(KV_LORA, MLA_HEADS, MLA_NOPE + MLA_V)
    uk = jnp.concatenate([ukv[..., :MLA_NOPE], jnp.zeros((KV_LORA, MLA_HEADS, HEAD_PAD - MLA_NOPE), w_ukv.dtype)], axis=-1)
    w_ukv_p = jnp.concatenate([uk.reshape(KV_LORA, MLA_HEADS * HEAD_PAD),
                               ukv[..., MLA_NOPE:].reshape(KV_LORA, MLA_WIDTH)], axis=1).astype(BF16)
    return w_in_p, w_uq_p, w_ukv_p


def _adaln_kernel(c_ref, w_ref, b_ref, o_ref):
    c = c_ref[...]
    a = c * jax.nn.sigmoid(c)
    a_hi = a.astype(BF16)
    a_lo = (a - a_hi.astype(F32)).astype(BF16)
    w = w_ref[...]
    w_hi = w.astype(BF16)
    w_lo = (w - w_hi.astype(F32)).astype(BF16)
    acc = jnp.dot(a_hi, w_hi, preferred_element_type=F32)
    acc += jnp.dot(a_hi, w_lo, preferred_element_type=F32)
    acc += jnp.dot(a_lo, w_hi, preferred_element_type=F32)
    o_ref[...] = acc + b_ref[...]


def _adaln(c, w_ada, b_ada):
    b, d = c.shape
    n = w_ada.shape[1]
    tn = 1024
    return pl.pallas_call(
        _adaln_kernel,
        grid=(n // tn,),
        in_specs=[pl.BlockSpec((b, d), lambda j: (0, 0)),
                  pl.BlockSpec((d, tn), lambda j: (0, j)),
                  pl.BlockSpec((1, tn), lambda j: (0, j))],
        out_specs=pl.BlockSpec((b, tn), lambda j: (0, j)),
        out_shape=jax.ShapeDtypeStruct((b, n), F32),
        compiler_params=pltpu.CompilerParams(dimension_semantics=("arbitrary",), vmem_limit_bytes=VMEM_LIMIT),
        name="adaln",
    )(c, w_ada, b_ada.reshape(1, n))


def _rms(t, gain):
    return t * lax.rsqrt(jnp.mean(t * t, axis=-1, keepdims=True) + EPS) * gain


def _rope_tables(pos_row):
    t = pos_row.shape[1]
    half = MLA_ROPE // 2
    idx = lax.broadcasted_iota(jnp.int32, (half, 1), 0).astype(F32)
    inv = jnp.exp(idx * (-2.0 * math.log(ROPE_THETA) / MLA_ROPE))
    ang = inv * pos_row
    cos, sin = jnp.cos(ang), jnp.sin(ang)
    pad = jnp.zeros((HEAD_PAD - MLA_NOPE - MLA_ROPE, t), F32)
    a_t = jnp.concatenate([jnp.ones((MLA_NOPE, t), F32), cos, cos, pad], axis=0)
    b_t = jnp.concatenate([jnp.zeros((MLA_NOPE, t), F32), -sin, sin, pad], axis=0)
    return a_t.T, b_t.T


def _rope_group(t, a, b):
    return t * a + pltpu.roll(t, LANES - MLA_ROPE, 1) * b


def _dup_halves(t):
    lane = lax.broadcasted_iota(jnp.int32, (1, LANES), 1)
    r = pltpu.roll(t, LANES // 2, 1)
    lo = lane < LANES // 2
    return jnp.concatenate([jnp.where(lo, t, r), jnp.where(lo, r, t)], axis=1)


def _inproj_kernel(x_ref, mod_ref, pos_ref, ng_ref, qg_ref, kvg_ref, win_ref, wuq_ref, wukv_ref,
                   q_ref, k_ref, v_ref, gm_ref, qs_ref, ks_ref, vs_ref, gs_ref):
    x = x_ref[0]
    mod = mod_ref[0]
    shift, scale = mod[0:1], mod[1:2]
    h = _rms(x, ng_ref[...]) * (1.0 + scale) + shift
    hb = h.astype(BF16)

    z = jnp.dot(hb, win_ref[...], preferred_element_type=F32)

    def seg(lo, hi):
        return z[:, lo:hi]

    a, b = _rope_tables(pos_ref[0])
    q_scale = (MLA_NOPE + MLA_ROPE) ** -0.5 * LOG2E

    zqr = seg(C_ZQ, C_ZKV)
    qn = _rms(zqr[:, :Q_LORA], qg_ref[...] * q_scale).astype(BF16)
    q = jnp.dot(qn, wuq_ref[...], preferred_element_type=F32)
    for hd in range(MLA_HEADS):
        sl = slice(hd * HEAD_PAD, (hd + 1) * HEAD_PAD)
        q_ref[0, :, sl] = _rope_group(q[:, sl], a, b).astype(BF16)

    kpe = _rope_group(zqr[:, Q_LORA:], a, b)
    kvn = _rms(seg(C_ZKV, C_GM), kvg_ref[...]).astype(BF16)
    kv = jnp.dot(kvn, wukv_ref[...], preferred_element_type=F32)
    for hd in range(MLA_HEADS):
        sl = slice(hd * HEAD_PAD, (hd + 1) * HEAD_PAD)
        k_ref[0, :, sl] = (kv[:, sl] + kpe).astype(BF16)
    v_ref[0] = kv[:, MLA_HEADS * HEAD_PAD:].astype(BF16)

    g = seg(C_GM, C_QS)
    gm_ref[0] = (g * jax.nn.sigmoid(g)).astype(BF16)
    g = seg(C_GS, D_IN_PACKED)
    gs_ref[0] = (g * jax.nn.sigmoid(g)).astype(BF16)
    qs_ref[0] = (seg(C_QS, C_KS) * (SWA_HEAD_DIM ** -0.5 * LOG2E)).astype(BF16)
    kvs = seg(C_KS, C_GS)
    ks_ref[0] = _dup_halves(kvs[:, :SWA_KV_WIDTH]).astype(BF16)
    vs_ref[0] = _dup_halves(kvs[:, SWA_KV_WIDTH:]).astype(BF16)


def _inproj(x, mod3, pos_row, norm_gain, q_gain, kv_gain, w_in_p, w_uq_p, w_ukv_p, tm):
    b, s, d = x.shape
    const = lambda shape: pl.BlockSpec(shape, lambda bi, i: (0,) * len(shape), pipeline_mode=pl.Buffered(1))
    tok = lambda w: pl.BlockSpec((1, tm, w), lambda bi, i: (bi, i, 0))
    widths = [MLA_HEADS * HEAD_PAD, MLA_HEADS * HEAD_PAD, MLA_WIDTH, MLA_WIDTH,
              SWA_WIDTH, 2 * SWA_KV_WIDTH, 2 * SWA_KV_WIDTH, SWA_WIDTH]
    return pl.pallas_call(
        _inproj_kernel,
        grid=(b, s // tm),
        in_specs=[tok(d),
                  pl.BlockSpec((1, 3, d), lambda bi, i: (bi, 0, 0)),
                  pl.BlockSpec((1, 1, tm), lambda bi, i: (bi, 0, i)),
                  const((1, d)), const((1, Q_LORA)), const((1, KV_LORA)),
                  const(w_in_p.shape), const(w_uq_p.shape), const(w_ukv_p.shape)],
        out_specs=[tok(w) for w in widths],
        out_shape=[jax.ShapeDtypeStruct((b, s, w), BF16) for w in widths],
        compiler_params=pltpu.CompilerParams(dimension_semantics=("parallel", "parallel"),
                                             vmem_limit_bytes=VMEM_LIMIT),
        name="inproj",
    )(x, mod3, pos_row, norm_gain, q_gain, kv_gain, w_in_p, w_uq_p, w_ukv_p)


def _dot_nt(a, b):
    return lax.dot_general(a, b, (((1,), (1,)), ((), ())), preferred_element_type=F32)


def _mla_kernel(q_ref, k_ref, v_ref, g_ref, o_ref, *, tq):
    nq = q_ref.shape[1] // tq
    lane = lax.broadcasted_iota(jnp.int32, (1, LANES), 1)
    first = lane < MLA_V
    row = lax.broadcasted_iota(jnp.int32, (tq, tq), 0)
    col = lax.broadcasted_iota(jnp.int32, (tq, tq), 1)
    causal = col <= row

    def scores(c, kb):
        q = q_ref[0, c * tq:(c + 1) * tq, :]
        k = k_ref[0, kb * tq:(kb + 1) * tq, :]
        return _dot_nt(q[:, :HEAD_PAD], k[:, :HEAD_PAD]), _dot_nt(q[:, HEAD_PAD:], k[:, HEAD_PAD:])

    def pv(pa, pb, kb):
        v = v_ref[0, kb * tq:(kb + 1) * tq, :]
        oa = jnp.dot(pa.astype(BF16), v, preferred_element_type=F32)
        ob = jnp.dot(pb.astype(BF16), v, preferred_element_type=F32)
        return jnp.where(first, oa, ob)

    tiles = [(c, kb) for c in range(nq) for kb in [c] + list(range(c))]
    nxt = scores(*tiles[0])
    for t, (c, kb) in enumerate(tiles):
        sa, sb = nxt
        nxt = scores(*tiles[t + 1]) if t + 1 < len(tiles) else None
        if kb == c:
            sa = jnp.where(causal, sa, NEG_BIG)
            sb = jnp.where(causal, sb, NEG_BIG)
            ma = jnp.max(sa, axis=-1, keepdims=True)
            mb = jnp.max(sb, axis=-1, keepdims=True)
            pa = jnp.exp2(sa - ma)
            pb = jnp.exp2(sb - mb)
            la = jnp.sum(pa, axis=-1, keepdims=True)
            lb = jnp.sum(pb, axis=-1, keepdims=True)
            acc = pv(pa, pb, kb)
        else:
            ma_n = jnp.maximum(ma, jnp.max(sa, axis=-1, keepdims=True))
            mb_n = jnp.maximum(mb, jnp.max(sb, axis=-1, keepdims=True))
            al_a = jnp.exp2(ma - ma_n)
            al_b = jnp.exp2(mb - mb_n)
            pa = jnp.exp2(sa - ma_n)
            pb = jnp.exp2(sb - mb_n)
            la = la * al_a + jnp.sum(pa, axis=-1, keepdims=True)
            lb = lb * al_b + jnp.sum(pb, axis=-1, keepdims=True)
            acc = acc * jnp.where(first, al_a, al_b) + pv(pa, pb, kb)
            ma, mb = ma_n, mb_n
        if t + 1 == len(tiles) or tiles[t + 1][0] != c:
            rows = slice(c * tq, (c + 1) * tq)
            o = acc * jnp.where(first, 1.0 / la, 1.0 / lb)
            o_ref[0, rows, :] = (o * g_ref[0, rows, :].astype(F32)).astype(BF16)


def _mla(q, k, v, gm, tq):
    b, s, _ = q.shape
    pairs = MLA_HEADS // 2
    seq = lambda w: pl.BlockSpec((1, s, w), lambda bi, j: (bi, 0, j))
    return pl.pallas_call(
        functools.partial(_mla_kernel, tq=tq),
        grid=(b, pairs),
        in_specs=[seq(2 * HEAD_PAD), seq(2 * HEAD_PAD), seq(2 * MLA_V), seq(2 * MLA_V)],
        out_specs=seq(2 * MLA_V),
        out_shape=jax.ShapeDtypeStruct((b, s, MLA_WIDTH), BF16),
        compiler_params=pltpu.CompilerParams(dimension_semantics=("parallel", "parallel"),
                                             vmem_limit_bytes=VMEM_LIMIT),
        name="mla",
    )(q, k, v, gm)


def _swa_kernel(sink_ref, q_ref, k_ref, v_ref, g_ref, pc_ref, pr_ref, o_ref, *, tq):
    i = pl.program_id(1)
    nblk = tq // WINDOW
    lane = lax.broadcasted_iota(jnp.int32, (1, LANES), 1)
    first = lane < SWA_HEAD_DIM
    row1 = lax.broadcasted_iota(jnp.int32, (WINDOW, WINDOW), 0)
    col1 = lax.broadcasted_iota(jnp.int32, (WINDOW, WINDOW), 1)
    in_cur1 = col1 <= row1
    in_cur = jnp.concatenate([in_cur1] * SWA_GROUP, axis=0)

    def positions(nl, has_prev):
        n = i * nblk + nl
        pk_cur = pr_ref[0, n]
        if has_prev:
            pk_prev = pr_ref[0, n - 1]
            p0 = pk_prev[:, 0:1]
            pk = jnp.where(in_cur1, pk_cur, pk_prev) - p0
        else:
            p0 = pk_cur[:, 0:1]
            pk = jnp.broadcast_to(pk_cur - p0, (WINDOW, WINDOW))
        return pk, jnp.broadcast_to(pc_ref[0, nl * WINDOW:(nl + 1) * WINDOW, :] - p0, (WINDOW, WINDOW))

    def unit(nl, kvh, has_prev, pk, pq):
        n = i * nblk + nl
        rows = slice(nl * WINDOW, (nl + 1) * WINDOW)
        ksl = slice(kvh * LANES, (kvh + 1) * LANES)
        heads = [kvh * SWA_GROUP + g for g in range(SWA_GROUP)]
        slopes = [LOG2E * 2.0 ** (-ALIBI_MAX_EXP * (h + 1) / SWA_HEADS) for h in heads]
        alibi = jnp.concatenate([sl * pk for sl in slopes], axis=0)
        sink = jnp.concatenate([sink_ref[h] * LOG2E + sl * pq for h, sl in zip(heads, slopes)], axis=0)
        pair_lanes = [slice((heads[0] // 2 + pr) * LANES, (heads[0] // 2 + pr + 1) * LANES)
                      for pr in range(SWA_GROUP // 2)]
        parts = []
        for psl in pair_lanes:
            q2 = q_ref[0, rows, psl]
            zq = jnp.zeros_like(q2)
            parts += [jnp.where(first, q2, zq), jnp.where(first, zq, q2)]
        q4 = jnp.concatenate(parts, axis=0)
        if has_prev:
            st = pl.multiple_of((n - 1) * WINDOW, WINDOW)
            s2 = _dot_nt(q4, k_ref[0, pl.ds(st, 2 * WINDOW), ksl])
            s = jnp.where(in_cur, s2[:, WINDOW:], s2[:, :WINDOW]) + alibi
        else:
            st = pl.multiple_of(n * WINDOW, WINDOW)
            s = _dot_nt(q4, k_ref[0, pl.ds(st, WINDOW), ksl])
            s = jnp.where(in_cur, s + alibi, NEG_BIG)
        m = jnp.maximum(jnp.max(s, axis=-1, keepdims=True), sink)
        p = jnp.exp2(s - m)
        l = jnp.sum(p, axis=-1, keepdims=True) + jnp.exp2(sink - m)
        if has_prev:
            zp = jnp.zeros_like(p)
            p2 = jnp.concatenate([jnp.where(in_cur, zp, p), jnp.where(in_cur, p, zp)], axis=1).astype(BF16)
            o4 = jnp.dot(p2, v_ref[0, pl.ds(st, 2 * WINDOW), ksl], preferred_element_type=F32)
        else:
            o4 = jnp.dot(p.astype(BF16), v_ref[0, pl.ds(st, WINDOW), ksl], preferred_element_type=F32)
        o4 = o4 * (1.0 / l)
        for pr, psl in enumerate(pair_lanes):
            o2 = jnp.where(first, o4[2 * pr * WINDOW:(2 * pr + 1) * WINDOW],
                           o4[(2 * pr + 1) * WINDOW:(2 * pr + 2) * WINDOW])
            o_ref[0, rows, psl] = (o2 * g_ref[0, rows, psl].astype(F32)).astype(BF16)

    def block(nl, has_prev):
        pk, pq = positions(nl, has_prev)
        for kvh in range(SWA_KV_HEADS):
            unit(nl, kvh, has_prev, pk, pq)

    pl.when(i == 0)(functools.partial(block, 0, False))
    pl.when(i > 0)(functools.partial(block, 0, True))
    for nl in range(1, nblk):
        block(nl, True)


def _swa(sinks, qs, ks, vs, gs, pos_col, pos_row, tq):
    b, s, _ = qs.shape
    tok = lambda w: pl.BlockSpec((1, tq, w), lambda bi, i: (bi, i, 0))
    full = lambda w: pl.BlockSpec((1, s, w), lambda bi, i: (bi, 0, 0))
    return pl.pallas_call(
        functools.partial(_swa_kernel, tq=tq),
        grid=(b, s // tq),
        in_specs=[pl.BlockSpec(memory_space=pltpu.SMEM),
                  tok(SWA_WIDTH), full(2 * SWA_KV_WIDTH), full(2 * SWA_KV_WIDTH), tok(SWA_WIDTH), tok(1),
                  pl.BlockSpec((1, s // WINDOW, 1, WINDOW), lambda bi, i: (bi, 0, 0, 0))],
        out_specs=tok(SWA_WIDTH),
        out_shape=jax.ShapeDtypeStruct((b, s, SWA_WIDTH), BF16),
        compiler_params=pltpu.CompilerParams(dimension_semantics=("parallel", "arbitrary"),
                                             vmem_limit_bytes=VMEM_LIMIT),
        name="swa",
    )(sinks, qs, ks, vs, gs, pos_col, pos_row)


def _outproj_kernel(om_ref, os_ref, x_ref, mod_ref, wm_ref, ws_ref, fg_ref, o_ref):
    y = jnp.dot(om_ref[0], wm_ref[...], preferred_element_type=F32)
    y += jnp.dot(os_ref[0], ws_ref[...], preferred_element_type=F32)
    gate = mod_ref[0][2:3]
    o_ref[0] = _rms(x_ref[0] + gate * y, fg_ref[...])


def _outproj(om, osw, x, mod3, w_m, w_s, final_gain, tm):
    b, s, d = x.shape
    const = lambda shape: pl.BlockSpec(shape, lambda bi, i: (0,) * len(shape))
    tok = lambda w: pl.BlockSpec((1, tm, w), lambda bi, i: (bi, i, 0))
    return pl.pallas_call(
        _outproj_kernel,
        grid=(b, s // tm),
        in_specs=[tok(MLA_WIDTH), tok(SWA_WIDTH), tok(d),
                  pl.BlockSpec((1, 3, d), lambda bi, i: (bi, 0, 0)),
                  const(w_m.shape), const(w_s.shape), const((1, d))],
        out_specs=tok(d),
        out_shape=jax.ShapeDtypeStruct((b, s, d), x.dtype),
        compiler_params=pltpu.CompilerParams(dimension_semantics=("parallel", "parallel"),
                                             vmem_limit_bytes=VMEM_LIMIT),
        name="outproj",
    )(om, osw, x, mod3, w_m, w_s, final_gain)


def kernel(x, c, positions, w_ada, b_ada, norm_gain, w_in, q_norm_gain, kv_norm_gain, w_uq, w_ukv, swa_sinks, w_out, final_gain):
    b, s, d = x.shape
    depth = w_ada.shape[0]
    assert depth == 1, "the final rmsnorm is fused into the (single) layer's output projection"
    pos = positions.astype(F32)
    pos_col = pos.reshape(b, s, 1)
    pos_row = pos.reshape(b, s // WINDOW, 1, WINDOW)
    for l in range(depth):
        mod3 = _adaln(c, w_ada[l], b_ada[l]).reshape(b, 3, d)
        w_in_p, w_uq_p, w_ukv_p = _pack_weights(w_in[l], w_uq[l], w_ukv[l])
        q, k, v, gm, qs, ks, vs, gs = _inproj(
            x, mod3, pos.reshape(b, 1, s), norm_gain[l].reshape(1, d), q_norm_gain[l].reshape(1, Q_LORA),
            kv_norm_gain[l].reshape(1, KV_LORA), w_in_p, w_uq_p, w_ukv_p, tm=1024)
        om = _mla(q, k, v, gm, tq=512)
        osw = _swa(swa_sinks[l].astype(F32), qs, ks, vs, gs, pos_col, pos_row, tq=512)
        w_o = w_out[l].astype(BF16)
        x = _outproj(om, osw, x, mod3, w_o[:MLA_WIDTH], w_o[MLA_WIDTH:], final_gain.reshape(1, d), tm=1024)
    return x
```

```python
import functools
import math

import jax
import jax.numpy as jnp
import numpy as np
from jax import lax
from jax.experimental import pallas as pl
from jax.experimental.pallas import tpu as pltpu

F32 = jnp.float32
BF16 = jnp.bfloat16

D_MODEL = 1024
MLA_HEADS = 8
MLA_NOPE = 64
MLA_ROPE = 32
MLA_V = 64
Q_LORA = 384
KV_LORA = 256
MLA_WIDTH = MLA_HEADS * MLA_V
SWA_HEADS = 8
SWA_KV_HEADS = 2
SWA_HEAD_DIM = 64
SWA_GROUP = SWA_HEADS // SWA_KV_HEADS
SWA_WIDTH = SWA_HEADS * SWA_HEAD_DIM
SWA_KV_WIDTH = SWA_KV_HEADS * SWA_HEAD_DIM
WINDOW = 128
ROPE_THETA = 10000.0
EPS = 1e-6
ALIBI_MAX_EXP = 8.0

LANES = 128
HEAD_PAD = LANES
LOG2E = math.log2(math.e)
NEG_BIG = -1e30

C_ZQ = 0
C_KR = C_ZQ + Q_LORA
C_ZKV = C_KR + LANES
C_GM = C_ZKV + KV_LORA
C_QS = C_GM + MLA_WIDTH
C_KS = C_QS + SWA_WIDTH
C_VS = C_KS + SWA_KV_WIDTH
C_GS = C_VS + SWA_KV_WIDTH
D_IN_PACKED = C_GS + SWA_WIDTH

VMEM_LIMIT = 56 * 1024 * 1024


def _swap_halves(w):
    half = w.shape[-1] // 2
    return jnp.concatenate([w[..., half:], w[..., :half]], axis=-1)


def _pack_weights(w_in, w_uq, w_ukv):
    d = w_in.shape[0]
    s = np.cumsum([0, Q_LORA, KV_LORA, MLA_ROPE, MLA_WIDTH, SWA_WIDTH, SWA_KV_WIDTH, SWA_KV_WIDTH, SWA_WIDTH])
    zq, zkv, kr, gm, qs, ks, vs, gs = [w_in[:, s[i]:s[i + 1]] for i in range(8)]
    kr_blk = jnp.concatenate([jnp.zeros((d, MLA_NOPE), w_in.dtype), kr, _swap_halves(kr)], axis=1)
    w_in_p = jnp.concatenate([zq, kr_blk, zkv, gm, qs, ks, vs, gs], axis=1).astype(BF16)

    uq = w_uq.reshape(Q_LORA, MLA_HEADS, MLA_NOPE + MLA_ROPE)
    uq_p = jnp.concatenate([uq, _swap_halves(uq[..., MLA_NOPE:])], axis=-1)
    w_uq_p = uq_p.reshape(Q_LORA, MLA_HEADS * HEAD_PAD).astype(BF16)

    ukv = w_ukv.reshape(KV_LORA, MLA_HEADS, MLA_NOPE + MLA_V)
    uk = jnp.concatenate([ukv[..., :MLA_NOPE], jnp.zeros((KV_LORA, MLA_HEADS, HEAD_PAD - MLA_NOPE), w_ukv.dtype)], axis=-1)
    w_ukv_p = jnp.concatenate([uk.reshape(KV_LORA, MLA_HEADS * HEAD_PAD),
                               ukv[..., MLA_NOPE:].reshape(KV_LORA, MLA_WIDTH)], axis=1).astype(BF16)
    return w_in_p, w_uq_p, w_ukv_p


def _adaln_kernel(c_ref, w_ref, b_ref, o_ref):
    c = c_ref[...]
    a = c * jax.nn.sigmoid(c)
    a_hi = a.astype(BF16)
    a_lo = (a - a_hi.astype(F32)).astype(BF16)
    w = w_ref[...]
    w_hi = w.astype(BF16)
    w_lo = (w - w_hi.astype(F32)).astype(BF16)
    acc = jnp.dot(a_hi, w_hi, preferred_element_type=F32)
    acc += jnp.dot(a_hi, w_lo, preferred_element_type=F32)
    acc += jnp.dot(a_lo, w_hi, preferred_element_type=F32)
    o_ref[...] = acc + b_ref[...]


def _adaln(c, w_ada, b_ada, layer):
    b, d = c.shape
    n = w_ada.shape[2]
    tn = 1024
    return pl.pallas_call(
        _adaln_kernel,
        grid=(n // tn,),
        in_specs=[pl.BlockSpec((b, d), lambda j: (0, 0)),
                  pl.BlockSpec((None, d, tn), lambda j: (layer, 0, j)),
                  pl.BlockSpec((1, tn), lambda j: (0, j))],
        out_specs=pl.BlockSpec((b, tn), lambda j: (0, j)),
        out_shape=jax.ShapeDtypeStruct((b, n), F32),
        compiler_params=pltpu.CompilerParams(dimension_semantics=("arbitrary",), vmem_limit_bytes=VMEM_LIMIT),
        name="adaln",
    )(c, w_ada, b_ada.reshape(1, n))


def _rms(t, gain):
    return t * lax.rsqrt(jnp.mean(t * t, axis=-1, keepdims=True) + EPS) * gain


def _rope_tables(pos_row):
    t = pos_row.shape[1]
    half = MLA_ROPE // 2
    idx = lax.broadcasted_iota(jnp.int32, (half, 1), 0).astype(F32)
    inv = jnp.exp(idx * (-2.0 * math.log(ROPE_THETA) / MLA_ROPE))
    ang = inv * pos_row
    cos, sin = jnp.cos(ang), jnp.sin(ang)
    pad = jnp.zeros((HEAD_PAD - MLA_NOPE - MLA_ROPE, t), F32)
    a_t = jnp.concatenate([jnp.ones((MLA_NOPE, t), F32), cos, cos, pad], axis=0)
    b_t = jnp.concatenate([jnp.zeros((MLA_NOPE, t), F32), -sin, sin, pad], axis=0)
    return a_t.T, b_t.T


def _rope_group(t, a, b):
    return t * a + pltpu.roll(t, LANES - MLA_ROPE, 1) * b


def _dup_halves(t):
    lane = lax.broadcasted_iota(jnp.int32, (1, LANES), 1)
    r = pltpu.roll(t, LANES // 2, 1)
    lo = lane < LANES // 2
    return jnp.concatenate([jnp.where(lo, t, r), jnp.where(lo, r, t)], axis=1)


def _inproj_kernel(x_ref, mod_ref, pos_ref, ng_ref, qg_ref, kvg_ref, win_ref, wuq_ref, wukv_ref,
                   q_ref, k_ref, v_ref, gm_ref, qs_ref, ks_ref, vs_ref, gs_ref):
    x = x_ref[0]
    mod = mod_ref[0]
    shift, scale = mod[0:1], mod[1:2]
    h = _rms(x, ng_ref[...]) * (1.0 + scale) + shift
    hb = h.astype(BF16)

    z = jnp.dot(hb, win_ref[...], preferred_element_type=F32)

    def seg(lo, hi):
        return z[:, lo:hi]

    a, b = _rope_tables(pos_ref[0])
    q_scale = (MLA_NOPE + MLA_ROPE) ** -0.5 * LOG2E

    zqr = seg(C_ZQ, C_ZKV)
    qn = _rms(zqr[:, :Q_LORA], qg_ref[...] * q_scale).astype(BF16)
    q = jnp.dot(qn, wuq_ref[...], preferred_element_type=F32)
    for hd in range(MLA_HEADS):
        sl = slice(hd * HEAD_PAD, (hd + 1) * HEAD_PAD)
        q_ref[0, :, sl] = _rope_group(q[:, sl], a, b).astype(BF16)

    kpe = _rope_group(zqr[:, Q_LORA:], a, b)
    kvn = _rms(seg(C_ZKV, C_GM), kvg_ref[...]).astype(BF16)
    kv = jnp.dot(kvn, wukv_ref[...], preferred_element_type=F32)
    for hd in range(MLA_HEADS):
        sl = slice(hd * HEAD_PAD, (hd + 1) * HEAD_PAD)
        k_ref[0, :, sl] = (kv[:, sl] + kpe).astype(BF16)
    v_ref[0] = kv[:, MLA_HEADS * HEAD_PAD:].astype(BF16)

    g = seg(C_GM, C_QS)
    gm_ref[0] = (g * jax.nn.sigmoid(g)).astype(BF16)
    g = seg(C_GS, D_IN_PACKED)
    gs_ref[0] = (g * jax.nn.sigmoid(g)).astype(BF16)
    qs_ref[0] = (seg(C_QS, C_KS) * (SWA_HEAD_DIM ** -0.5 * LOG2E)).astype(BF16)
    kvs = seg(C_KS, C_GS)
    ks_ref[0] = _dup_halves(kvs[:, :SWA_KV_WIDTH]).astype(BF16)
    vs_ref[0] = _dup_halves(kvs[:, SWA_KV_WIDTH:]).astype(BF16)


def _inproj(x, mod3, pos_row, norm_gain, q_gain, kv_gain, w_in_p, w_uq_p, w_ukv_p, tm):
    b, s, d = x.shape
    const = lambda shape: pl.BlockSpec(shape, lambda bi, i: (0,) * len(shape), pipeline_mode=pl.Buffered(1))
    tok = lambda w: pl.BlockSpec((1, tm, w), lambda bi, i: (bi, i, 0))
    widths = [MLA_HEADS * HEAD_PAD, MLA_HEADS * HEAD_PAD, MLA_WIDTH, MLA_WIDTH,
              SWA_WIDTH, 2 * SWA_KV_WIDTH, 2 * SWA_KV_WIDTH, SWA_WIDTH]
    return pl.pallas_call(
        _inproj_kernel,
        grid=(b, s // tm),
        in_specs=[tok(d),
                  pl.BlockSpec((1, 3, d), lambda bi, i: (bi, 0, 0)),
                  pl.BlockSpec((1, 1, tm), lambda bi, i: (bi, 0, i)),
                  const((1, d)), const((1, Q_LORA)), const((1, KV_LORA)),
                  const(w_in_p.shape), const(w_uq_p.shape), const(w_ukv_p.shape)],
        out_specs=[tok(w) for w in widths],
        out_shape=[jax.ShapeDtypeStruct((b, s, w), BF16) for w in widths],
        compiler_params=pltpu.CompilerParams(dimension_semantics=("parallel", "parallel"),
                                             vmem_limit_bytes=VMEM_LIMIT),
        name="inproj",
    )(x, mod3, pos_row, norm_gain, q_gain, kv_gain, w_in_p, w_uq_p, w_ukv_p)


def _dot_nt(a, b):
    return lax.dot_general(a, b, (((1,), (1,)), ((), ())), preferred_element_type=F32)


def _mla_kernel(q_ref, k_ref, v_ref, g_ref, o_ref, vl_ref, *, tq):
    nq = q_ref.shape[1] // tq
    lane = lax.broadcasted_iota(jnp.int32, (1, LANES), 1)
    first = lane < MLA_V
    row = lax.broadcasted_iota(jnp.int32, (tq, tq), 0)
    col = lax.broadcasted_iota(jnp.int32, (tq, tq), 1)
    causal = col <= row

    v = v_ref[0]
    ones = jnp.ones_like(v)
    vl_ref[0] = jnp.where(first, v, ones)
    vl_ref[1] = jnp.where(first, ones, v)

    def scores(c, kb):
        q = q_ref[0, c * tq:(c + 1) * tq, :]
        k = k_ref[0, kb * tq:(kb + 1) * tq, :]
        return _dot_nt(q[:, :HEAD_PAD], k[:, :HEAD_PAD]), _dot_nt(q[:, HEAD_PAD:], k[:, HEAD_PAD:])

    def pv(p, head, kb):
        return jnp.dot(p.astype(BF16), vl_ref[head, kb * tq:(kb + 1) * tq, :], preferred_element_type=F32)

    tiles = [(c, kb) for c in range(nq) for kb in [c] + list(range(c))]
    nxt = scores(*tiles[0])
    for t, (c, kb) in enumerate(tiles):
        sa, sb = nxt
        nxt = scores(*tiles[t + 1]) if t + 1 < len(tiles) else None
        if kb == c:
            sa = jnp.where(causal, sa, NEG_BIG)
            sb = jnp.where(causal, sb, NEG_BIG)
            ma = jnp.max(sa, axis=-1, keepdims=True)
            mb = jnp.max(sb, axis=-1, keepdims=True)
            acc_a = pv(jnp.exp2(sa - ma), 0, kb)
            acc_b = pv(jnp.exp2(sb - mb), 1, kb)
        else:
            ma_n = jnp.maximum(ma, jnp.max(sa, axis=-1, keepdims=True))
            mb_n = jnp.maximum(mb, jnp.max(sb, axis=-1, keepdims=True))
            acc_a = acc_a * jnp.exp2(ma - ma_n) + pv(jnp.exp2(sa - ma_n), 0, kb)
            acc_b = acc_b * jnp.exp2(mb - mb_n) + pv(jnp.exp2(sb - mb_n), 1, kb)
            ma, mb = ma_n, mb_n
        if t + 1 == len(tiles) or tiles[t + 1][0] != c:
            rows = slice(c * tq, (c + 1) * tq)
            acc = jnp.where(first, acc_a, acc_b)
            den = jnp.where(first, pltpu.roll(acc_a, MLA_V, 1), pltpu.roll(acc_b, MLA_V, 1))
            o_ref[0, rows, :] = (acc / den * g_ref[0, rows, :].astype(F32)).astype(BF16)


def _mla(q, k, v, gm, tq):
    b, s, _ = q.shape
    pairs = MLA_HEADS // 2
    seq = lambda w: pl.BlockSpec((1, s, w), lambda bi, j: (bi, 0, j))
    return pl.pallas_call(
        functools.partial(_mla_kernel, tq=tq),
        grid=(b, pairs),
        in_specs=[seq(2 * HEAD_PAD), seq(2 * HEAD_PAD), seq(2 * MLA_V), seq(2 * MLA_V)],
        out_specs=seq(2 * MLA_V),
        out_shape=jax.ShapeDtypeStruct((b, s, MLA_WIDTH), BF16),
        scratch_shapes=[pltpu.VMEM((2, s, 2 * MLA_V), BF16)],
        compiler_params=pltpu.CompilerParams(dimension_semantics=("parallel", "parallel"),
                                             vmem_limit_bytes=VMEM_LIMIT),
        name="mla",
    )(q, k, v, gm)


def _swa_kernel(sink_ref, q_ref, k_ref, v_ref, g_ref, pc_ref, pr_ref, o_ref, *, tq):
    i = pl.program_id(1)
    nblk = tq // WINDOW
    lane = lax.broadcasted_iota(jnp.int32, (1, LANES), 1)
    first = lane < SWA_HEAD_DIM
    row1 = lax.broadcasted_iota(jnp.int32, (WINDOW, WINDOW), 0)
    col1 = lax.broadcasted_iota(jnp.int32, (WINDOW, WINDOW), 1)
    in_cur1 = col1 <= row1
    in_cur = jnp.concatenate([in_cur1] * SWA_GROUP, axis=0)

    def positions(nl, has_prev):
        n = i * nblk + nl
        pk_cur = pr_ref[0, n]
        if has_prev:
            pk_prev = pr_ref[0, n - 1]
            p0 = pk_prev[:, 0:1]
            pk = jnp.where(in_cur1, pk_cur, pk_prev) - p0
        else:
            p0 = pk_cur[:, 0:1]
            pk = jnp.broadcast_to(pk_cur - p0, (WINDOW, WINDOW))
        return pk, jnp.broadcast_to(pc_ref[0, nl * WINDOW:(nl + 1) * WINDOW, :] - p0, (WINDOW, WINDOW))

    def unit(nl, kvh, has_prev, pk, pq):
        n = i * nblk + nl
        rows = slice(nl * WINDOW, (nl + 1) * WINDOW)
        ksl = slice(kvh * LANES, (kvh + 1) * LANES)
        heads = [kvh * SWA_GROUP + g for g in range(SWA_GROUP)]
        slopes = [LOG2E * 2.0 ** (-ALIBI_MAX_EXP * (h + 1) / SWA_HEADS) for h in heads]
        alibi = jnp.concatenate([sl * pk for sl in slopes], axis=0)
        sink = jnp.concatenate([sink_ref[h] * LOG2E + sl * pq for h, sl in zip(heads, slopes)], axis=0)
        pair_lanes = [slice((heads[0] // 2 + pr) * LANES, (heads[0] // 2 + pr + 1) * LANES)
                      for pr in range(SWA_GROUP // 2)]
        parts = []
        for psl in pair_lanes:
            q2 = q_ref[0, rows, psl]
            zq = jnp.zeros_like(q2)
            parts += [jnp.where(first, q2, zq), jnp.where(first, zq, q2)]
        q4 = jnp.concatenate(parts, axis=0)
        if has_prev:
            st = pl.multiple_of((n - 1) * WINDOW, WINDOW)
            s2 = _dot_nt(q4, k_ref[0, pl.ds(st, 2 * WINDOW), ksl])
            s = jnp.where(in_cur, s2[:, WINDOW:], s2[:, :WINDOW]) + alibi
        else:
            st = pl.multiple_of(n * WINDOW, WINDOW)
            s = _dot_nt(q4, k_ref[0, pl.ds(st, WINDOW), ksl])
            s = jnp.where(in_cur, s + alibi, NEG_BIG)
        m = jnp.maximum(jnp.max(s, axis=-1, keepdims=True), sink)
        p = jnp.exp2(s - m)
        l = jnp.sum(p, axis=-1, keepdims=True) + jnp.exp2(sink - m)
        if has_prev:
            zp = jnp.zeros_like(p)
            p2 = jnp.concatenate([jnp.where(in_cur, zp, p), jnp.where(in_cur, p, zp)], axis=1).astype(BF16)
            o4 = jnp.dot(p2, v_ref[0, pl.ds(st, 2 * WINDOW), ksl], preferred_element_type=F32)
        else:
            o4 = jnp.dot(p.astype(BF16), v_ref[0, pl.ds(st, WINDOW), ksl], preferred_element_type=F32)
        o4 = o4 * (1.0 / l)
        for pr, psl in enumerate(pair_lanes):
            o2 = jnp.where(first, o4[2 * pr * WINDOW:(2 * pr + 1) * WINDOW],
                           o4[(2 * pr + 1) * WINDOW:(2 * pr + 2) * WINDOW])
            o_ref[0, rows, psl] = (o2 * g_ref[0, rows, psl].astype(F32)).astype(BF16)

    def block(nl, has_prev):
        pk, pq = positions(nl, has_prev)
        for kvh in range(SWA_KV_HEADS):
            unit(nl, kvh, has_prev, pk, pq)

    pl.when(i == 0)(functools.partial(block, 0, False))
    pl.when(i > 0)(functools.partial(block, 0, True))
    for nl in range(1, nblk):
        block(nl, True)


def _swa(sinks, qs, ks, vs, gs, pos_col, pos_row, tq):
    b, s, _ = qs.shape
    tok = lambda w: pl.BlockSpec((1, tq, w), lambda bi, i: (bi, i, 0))
    full = lambda w: pl.BlockSpec((1, s, w), lambda bi, i: (bi, 0, 0))
    return pl.pallas_call(
        functools.partial(_swa_kernel, tq=tq),
        grid=(b, s // tq),
        in_specs=[pl.BlockSpec(memory_space=pltpu.SMEM),
                  tok(SWA_WIDTH), full(2 * SWA_KV_WIDTH), full(2 * SWA_KV_WIDTH), tok(SWA_WIDTH), tok(1),
                  pl.BlockSpec((1, s // WINDOW, 1, WINDOW), lambda bi, i: (bi, 0, 0, 0))],
        out_specs=tok(SWA_WIDTH),
        out_shape=jax.ShapeDtypeStruct((b, s, SWA_WIDTH), BF16),
        compiler_params=pltpu.CompilerParams(dimension_semantics=("parallel", "arbitrary"),
                                             vmem_limit_bytes=VMEM_LIMIT),
        name="swa",
    )(sinks, qs, ks, vs, gs, pos_col, pos_row)


def _outproj_kernel(om_ref, os_ref, x_ref, mod_ref, wm_ref, ws_ref, fg_ref, o_ref):
    y = jnp.dot(om_ref[0], wm_ref[...], preferred_element_type=F32)
    y += jnp.dot(os_ref[0], ws_ref[...], preferred_element_type=F32)
    gate = mod_ref[0][2:3]
    o_ref[0] = _rms(x_ref[0] + gate * y, fg_ref[...])


def _outproj(om, osw, x, mod3, w_m, w_s, final_gain, tm):
    b, s, d = x.shape
    const = lambda shape: pl.BlockSpec(shape, lambda bi, i: (0,) * len(shape))
    tok = lambda w: pl.BlockSpec((1, tm, w), lambda bi, i: (bi, i, 0))
    return pl.pallas_call(
        _outproj_kernel,
        grid=(b, s // tm),
        in_specs=[tok(MLA_WIDTH), tok(SWA_WIDTH), tok(d),
                  pl.BlockSpec((1, 3, d), lambda bi, i: (bi, 0, 0)),
                  const(w_m.shape), const(w_s.shape), const((1, d))],
        out_specs=tok(d),
        out_shape=jax.ShapeDtypeStruct((b, s, d), x.dtype),
        compiler_params=pltpu.CompilerParams(dimension_semantics=("parallel", "parallel"),
                                             vmem_limit_bytes=VMEM_LIMIT),
        name="outproj",
    )(om, osw, x, mod3, w_m, w_s, final_gain)


def kernel(x, c, positions, w_ada, b_ada, norm_gain, w_in, q_norm_gain, kv_norm_gain, w_uq, w_ukv, swa_sinks, w_out, final_gain):
    b, s, d = x.shape
    depth = w_ada.shape[0]
    assert depth == 1, "the final rmsnorm is fused into the (single) layer's output projection"
    pos = positions.astype(F32)
    pos_col = pos.reshape(b, s, 1)
    pos_row = pos.reshape(b, s // WINDOW, 1, WINDOW)
    for l in range(depth):
        mod3 = _adaln(c, w_ada, b_ada[l], l).reshape(b, 3, d)
        w_in_p, w_uq_p, w_ukv_p = _pack_weights(w_in[l], w_uq[l], w_ukv[l])
        q, k, v, gm, qs, ks, vs, gs = _inproj(
            x, mod3, pos.reshape(b, 1, s), norm_gain[l].reshape(1, d), q_norm_gain[l].reshape(1, Q_LORA),
            kv_norm_gain[l].reshape(1, KV_LORA), w_in_p, w_uq_p, w_ukv_p, tm=1024)
        om = _mla(q, k, v, gm, tq=512)
        osw = _swa(swa_sinks[l].astype(F32), qs, ks, vs, gs, pos_col, pos_row, tq=512)
        w_o = w_out[l].astype(BF16)
        x = _outproj(om, osw, x, mod3, w_o[:MLA_WIDTH], w_o[MLA_WIDTH:], final_gain.reshape(1, d), tm=1024)
    return x
```

```python
import functools
import math

import jax
import jax.numpy as jnp
import numpy as np
from jax import lax
from jax.experimental import pallas as pl
from jax.experimental.pallas import tpu as pltpu

F32 = jnp.float32
BF16 = jnp.bfloat16

D_MODEL = 1024
MLA_HEADS = 8
MLA_NOPE = 64
MLA_ROPE = 32
MLA_V = 64
Q_LORA = 384
KV_LORA = 256
MLA_WIDTH = MLA_HEADS * MLA_V
SWA_HEADS = 8
SWA_KV_HEADS = 2
SWA_HEAD_DIM = 64
SWA_GROUP = SWA_HEADS // SWA_KV_HEADS
SWA_WIDTH = SWA_HEADS * SWA_HEAD_DIM
SWA_KV_WIDTH = SWA_KV_HEADS * SWA_HEAD_DIM
WINDOW = 128
ROPE_THETA = 10000.0
EPS = 1e-6
ALIBI_MAX_EXP = 8.0

LANES = 128
HEAD_PAD = LANES
LOG2E = math.log2(math.e)
NEG_BIG = -1e30

C_ZQ = 0
C_KR = C_ZQ + Q_LORA
C_ZKV = C_KR + LANES
C_GM = C_ZKV + KV_LORA
C_QS = C_GM + MLA_WIDTH
C_KS = C_QS + SWA_WIDTH
C_VS = C_KS + SWA_KV_WIDTH
C_GS = C_VS + SWA_KV_WIDTH
D_IN_PACKED = C_GS + SWA_WIDTH

VMEM_LIMIT = 56 * 1024 * 1024


def _swap_halves(w):
    half = w.shape[-1] // 2
    return jnp.concatenate([w[..., half:], w[..., :half]], axis=-1)


def _pack_weights(w_in, w_uq, w_ukv):
    d = w_in.shape[0]
    s = np.cumsum([0, Q_LORA, KV_LORA, MLA_ROPE, MLA_WIDTH, SWA_WIDTH, SWA_KV_WIDTH, SWA_KV_WIDTH, SWA_WIDTH])
    zq, zkv, kr, gm, qs, ks, vs, gs = [w_in[:, s[i]:s[i + 1]] for i in range(8)]
    kr_blk = jnp.concatenate([jnp.zeros((d, MLA_NOPE), w_in.dtype), kr, _swap_halves(kr)], axis=1)
    w_in_p = jnp.concatenate([zq, kr_blk, zkv, gm, qs, ks, vs, gs], axis=1).astype(BF16)

    uq = w_uq.reshape(Q_LORA, MLA_HEADS, MLA_NOPE + MLA_ROPE)
    uq_p = jnp.concatenate([uq, _swap_halves(uq[..., MLA_NOPE:])], axis=-1)
    w_uq_p = uq_p.reshape(Q_LORA, MLA_HEADS * HEAD_PAD).astype(BF16)

    ukv = w_ukv.reshape(KV_LORA, MLA_HEADS, MLA_NOPE + MLA_V)
    uk = jnp.concatenate([ukv[..., :MLA_NOPE], jnp.zeros((KV_LORA, MLA_HEADS, HEAD_PAD - MLA_NOPE), w_ukv.dtype)], axis=-1)
    w_ukv_p = jnp.concatenate([uk.reshape(KV_LORA, MLA_HEADS * HEAD_PAD),
                               ukv[..., MLA_NOPE:].reshape(KV_LORA, MLA_WIDTH)], axis=1).astype(BF16)
    return w_in_p, w_uq_p, w_ukv_p


def _adaln_kernel(c_ref, w_ref, b_ref, o_ref):
    c = c_ref[...]
    a = c * jax.nn.sigmoid(c)
    a_hi = a.astype(BF16)
    a_lo = (a - a_hi.astype(F32)).astype(BF16)
    w = w_ref[...]
    w_hi = w.astype(BF16)
    w_lo = (w - w_hi.astype(F32)).astype(BF16)
    acc = jnp.dot(a_hi, w_hi, preferred_element_type=F32)
    acc += jnp.dot(a_hi, w_lo, preferred_element_type=F32)
    acc += jnp.dot(a_lo, w_hi, preferred_element_type=F32)
    o_ref[...] = acc + b_ref[...]


def _adaln(c, w_ada, b_ada, layer):
    b, d = c.shape
    n = w_ada.shape[2]
    tn = 1024
    return pl.pallas_call(
        _adaln_kernel,
        grid=(n // tn,),
        in_specs=[pl.BlockSpec((b, d), lambda j: (0, 0)),
                  pl.BlockSpec((None, d, tn), lambda j: (layer, 0, j)),
                  pl.BlockSpec((1, tn), lambda j: (0, j))],
        out_specs=pl.BlockSpec((b, tn), lambda j: (0, j)),
        out_shape=jax.ShapeDtypeStruct((b, n), F32),
        compiler_params=pltpu.CompilerParams(dimension_semantics=("arbitrary",), vmem_limit_bytes=VMEM_LIMIT),
        name="adaln",
    )(c, w_ada, b_ada.reshape(1, n))


def _rms(t, gain):
    return t * lax.rsqrt(jnp.mean(t * t, axis=-1, keepdims=True) + EPS) * gain


def _rope_tables(pos_row):
    t = pos_row.shape[1]
    half = MLA_ROPE // 2
    idx = lax.broadcasted_iota(jnp.int32, (half, 1), 0).astype(F32)
    inv = jnp.exp(idx * (-2.0 * math.log(ROPE_THETA) / MLA_ROPE))
    ang = inv * pos_row
    cos, sin = jnp.cos(ang), jnp.sin(ang)
    pad = jnp.zeros((HEAD_PAD - MLA_NOPE - MLA_ROPE, t), F32)
    a_t = jnp.concatenate([jnp.ones((MLA_NOPE, t), F32), cos, cos, pad], axis=0)
    b_t = jnp.concatenate([jnp.zeros((MLA_NOPE, t), F32), -sin, sin, pad], axis=0)
    return a_t.T, b_t.T


def _rope_group(t, a, b):
    return t * a + pltpu.roll(t, LANES - MLA_ROPE, 1) * b


def _dup_halves(t):
    lane = lax.broadcasted_iota(jnp.int32, (1, LANES), 1)
    r = pltpu.roll(t, LANES // 2, 1)
    lo = lane < LANES // 2
    return jnp.concatenate([jnp.where(lo, t, r), jnp.where(lo, r, t)], axis=1)


def _inproj_kernel(x_ref, mod_ref, pos_ref, ng_ref, qg_ref, kvg_ref, win_ref, wuq_ref, wukv_ref,
                   q_ref, k_ref, v_ref, gm_ref, qs_ref, ks_ref, vs_ref, gs_ref):
    x = x_ref[0]
    mod = mod_ref[0]
    shift, scale = mod[0:1], mod[1:2]
    h = _rms(x, ng_ref[...]) * (1.0 + scale) + shift
    hb = h.astype(BF16)

    z = jnp.dot(hb, win_ref[...], preferred_element_type=F32)

    def seg(lo, hi):
        return z[:, lo:hi]

    a, b = _rope_tables(pos_ref[0])
    q_scale = (MLA_NOPE + MLA_ROPE) ** -0.5 * LOG2E

    zqr = seg(C_ZQ, C_ZKV)
    qn = _rms(zqr[:, :Q_LORA], qg_ref[...] * q_scale).astype(BF16)
    q = jnp.dot(qn, wuq_ref[...], preferred_element_type=F32)
    for hd in range(MLA_HEADS):
        sl = slice(hd * HEAD_PAD, (hd + 1) * HEAD_PAD)
        q_ref[0, :, sl] = _rope_group(q[:, sl], a, b).astype(BF16)

    kpe = _rope_group(zqr[:, Q_LORA:], a, b)
    kvn = _rms(seg(C_ZKV, C_GM), kvg_ref[...]).astype(BF16)
    kv = jnp.dot(kvn, wukv_ref[...], preferred_element_type=F32)
    for hd in range(MLA_HEADS):
        sl = slice(hd * HEAD_PAD, (hd + 1) * HEAD_PAD)
        k_ref[0, :, sl] = (kv[:, sl] + kpe).astype(BF16)
    v_ref[0] = kv[:, MLA_HEADS * HEAD_PAD:].astype(BF16)

    g = seg(C_GM, C_QS)
    gm_ref[0] = (g * jax.nn.sigmoid(g)).astype(BF16)
    g = seg(C_GS, D_IN_PACKED)
    gs_ref[0] = (g * jax.nn.sigmoid(g)).astype(BF16)
    qs_ref[0] = (seg(C_QS, C_KS) * (SWA_HEAD_DIM ** -0.5 * LOG2E)).astype(BF16)
    kvs = seg(C_KS, C_GS)
    ks_ref[0] = _dup_halves(kvs[:, :SWA_KV_WIDTH]).astype(BF16)
    vs_ref[0] = _dup_halves(kvs[:, SWA_KV_WIDTH:]).astype(BF16)


def _inproj(x, mod3, pos_row, norm_gain, q_gain, kv_gain, w_in_p, w_uq_p, w_ukv_p, tm):
    b, s, d = x.shape
    const = lambda shape: pl.BlockSpec(shape, lambda bi, i: (0,) * len(shape), pipeline_mode=pl.Buffered(1))
    tok = lambda w: pl.BlockSpec((1, tm, w), lambda bi, i: (bi, i, 0))
    widths = [MLA_HEADS * HEAD_PAD, MLA_HEADS * HEAD_PAD, MLA_WIDTH, MLA_WIDTH,
              SWA_WIDTH, 2 * SWA_KV_WIDTH, 2 * SWA_KV_WIDTH, SWA_WIDTH]
    return pl.pallas_call(
        _inproj_kernel,
        grid=(b, s // tm),
        in_specs=[tok(d),
                  pl.BlockSpec((1, 3, d), lambda bi, i: (bi, 0, 0)),
                  pl.BlockSpec((1, 1, tm), lambda bi, i: (bi, 0, i)),
                  const((1, d)), const((1, Q_LORA)), const((1, KV_LORA)),
                  const(w_in_p.shape), const(w_uq_p.shape), const(w_ukv_p.shape)],
        out_specs=[tok(w) for w in widths],
        out_shape=[jax.ShapeDtypeStruct((b, s, w), BF16) for w in widths],
        compiler_params=pltpu.CompilerParams(dimension_semantics=("parallel", "parallel"),
                                             vmem_limit_bytes=VMEM_LIMIT),
        name="inproj",
    )(x, mod3, pos_row, norm_gain, q_gain, kv_gain, w_in_p, w_uq_p, w_ukv_p)


def _dot_nt(a, b):
    return lax.dot_general(a, b, (((1,), (1,)), ((), ())), preferred_element_type=F32)


def _mla_kernel(q_ref, k_ref, v_ref, g_ref, o_ref, vl_ref, *, tq):
    nq = q_ref.shape[1] // tq
    lane = lax.broadcasted_iota(jnp.int32, (1, LANES), 1)
    first = lane < MLA_V
    row = lax.broadcasted_iota(jnp.int32, (tq, tq), 0)
    col = lax.broadcasted_iota(jnp.int32, (tq, tq), 1)
    causal = col <= row

    v = v_ref[0]
    ones = jnp.ones_like(v)
    vl_ref[0] = jnp.where(first, v, ones)
    vl_ref[1] = jnp.where(first, ones, v)

    def scores(c, kb):
        q = q_ref[0, c * tq:(c + 1) * tq, :]
        k = k_ref[0, kb * tq:(kb + 1) * tq, :]
        return _dot_nt(q[:, :HEAD_PAD], k[:, :HEAD_PAD]), _dot_nt(q[:, HEAD_PAD:], k[:, HEAD_PAD:])

    def pv(p, head, kb):
        return jnp.dot(p.astype(BF16), vl_ref[head, kb * tq:(kb + 1) * tq, :], preferred_element_type=F32)

    tiles = [(c, kb) for c in range(nq) for kb in [c] + list(range(c))]
    nxt = scores(*tiles[0])
    for t, (c, kb) in enumerate(tiles):
        sa, sb = nxt
        nxt = scores(*tiles[t + 1]) if t + 1 < len(tiles) else None
        if kb == c:
            sa = jnp.where(causal, sa, NEG_BIG)
            sb = jnp.where(causal, sb, NEG_BIG)
            ma = jnp.max(sa, axis=-1, keepdims=True)
            mb = jnp.max(sb, axis=-1, keepdims=True)
            acc_a = pv(jnp.exp2(sa - ma), 0, kb)
            acc_b = pv(jnp.exp2(sb - mb), 1, kb)
        else:
            ma_n = jnp.maximum(ma, jnp.max(sa, axis=-1, keepdims=True))
            mb_n = jnp.maximum(mb, jnp.max(sb, axis=-1, keepdims=True))
            acc_a = acc_a * jnp.exp2(ma - ma_n) + pv(jnp.exp2(sa - ma_n), 0, kb)
            acc_b = acc_b * jnp.exp2(mb - mb_n) + pv(jnp.exp2(sb - mb_n), 1, kb)
            ma, mb = ma_n, mb_n
        if t + 1 == len(tiles) or tiles[t + 1][0] != c:
            rows = slice(c * tq, (c + 1) * tq)
            acc = jnp.where(first, acc_a, acc_b)
            den = jnp.where(first, pltpu.roll(acc_a, MLA_V, 1), pltpu.roll(acc_b, MLA_V, 1))
            o_ref[0, rows, :] = (acc / den * g_ref[0, rows, :].astype(F32)).astype(BF16)


def _mla(q, k, v, gm, tq):
    b, s, _ = q.shape
    pairs = MLA_HEADS // 2
    seq = lambda w: pl.BlockSpec((1, s, w), lambda bi, j: (bi, 0, j))
    return pl.pallas_call(
        functools.partial(_mla_kernel, tq=tq),
        grid=(b, pairs),
        in_specs=[seq(2 * HEAD_PAD), seq(2 * HEAD_PAD), seq(2 * MLA_V), seq(2 * MLA_V)],
        out_specs=seq(2 * MLA_V),
        out_shape=jax.ShapeDtypeStruct((b, s, MLA_WIDTH), BF16),
        scratch_shapes=[pltpu.VMEM((2, s, 2 * MLA_V), BF16)],
        compiler_params=pltpu.CompilerParams(dimension_semantics=("parallel", "parallel"),
                                             vmem_limit_bytes=VMEM_LIMIT),
        name="mla",
    )(q, k, v, gm)


def _swa_out_kernel(sink_ref, q_ref, k_ref, v_ref, g_ref, pc_ref, pr_ref, om_ref, x_ref, mod_ref, wm_ref, ws_ref,
                    fg_ref, out_ref, o_ref, *, tq):
    i = pl.program_id(1)
    nblk = tq // WINDOW
    lane = lax.broadcasted_iota(jnp.int32, (1, LANES), 1)
    first = lane < SWA_HEAD_DIM
    row1 = lax.broadcasted_iota(jnp.int32, (WINDOW, WINDOW), 0)
    col1 = lax.broadcasted_iota(jnp.int32, (WINDOW, WINDOW), 1)
    in_cur1 = col1 <= row1
    in_cur = jnp.concatenate([in_cur1] * SWA_GROUP, axis=0)

    def positions(nl, has_prev):
        n = i * nblk + nl
        pk_cur = pr_ref[0, n]
        if has_prev:
            pk_prev = pr_ref[0, n - 1]
            p0 = pk_prev[:, 0:1]
            pk = jnp.where(in_cur1, pk_cur, pk_prev) - p0
        else:
            p0 = pk_cur[:, 0:1]
            pk = jnp.broadcast_to(pk_cur - p0, (WINDOW, WINDOW))
        return pk, jnp.broadcast_to(pc_ref[0, nl * WINDOW:(nl + 1) * WINDOW, :] - p0, (WINDOW, WINDOW))

    def unit(nl, kvh, has_prev, pk, pq):
        n = i * nblk + nl
        rows = slice(nl * WINDOW, (nl + 1) * WINDOW)
        ksl = slice(kvh * LANES, (kvh + 1) * LANES)
        heads = [kvh * SWA_GROUP + g for g in range(SWA_GROUP)]
        slopes = [LOG2E * 2.0 ** (-ALIBI_MAX_EXP * (h + 1) / SWA_HEADS) for h in heads]
        alibi = jnp.concatenate([sl * pk for sl in slopes], axis=0)
        sink = jnp.concatenate([sink_ref[h] * LOG2E + sl * pq for h, sl in zip(heads, slopes)], axis=0)
        pair_lanes = [slice((heads[0] // 2 + pr) * LANES, (heads[0] // 2 + pr + 1) * LANES)
                      for pr in range(SWA_GROUP // 2)]
        parts = []
        for psl in pair_lanes:
            q2 = q_ref[0, rows, psl]
            zq = jnp.zeros_like(q2)
            parts += [jnp.where(first, q2, zq), jnp.where(first, zq, q2)]
        q4 = jnp.concatenate(parts, axis=0)
        if has_prev:
            st = pl.multiple_of((n - 1) * WINDOW, WINDOW)
            s2 = _dot_nt(q4, k_ref[0, pl.ds(st, 2 * WINDOW), ksl])
            s = jnp.where(in_cur, s2[:, WINDOW:], s2[:, :WINDOW]) + alibi
        else:
            st = pl.multiple_of(n * WINDOW, WINDOW)
            s = _dot_nt(q4, k_ref[0, pl.ds(st, WINDOW), ksl])
            s = jnp.where(in_cur, s + alibi, NEG_BIG)
        m = jnp.maximum(jnp.max(s, axis=-1, keepdims=True), sink)
        p = jnp.exp2(s - m)
        l = jnp.sum(p, axis=-1, keepdims=True) + jnp.exp2(sink - m)
        if has_prev:
            zp = jnp.zeros_like(p)
            p2 = jnp.concatenate([jnp.where(in_cur, zp, p), jnp.where(in_cur, p, zp)], axis=1).astype(BF16)
            o4 = jnp.dot(p2, v_ref[0, pl.ds(st, 2 * WINDOW), ksl], preferred_element_type=F32)
        else:
            o4 = jnp.dot(p.astype(BF16), v_ref[0, pl.ds(st, WINDOW), ksl], preferred_element_type=F32)
        o4 = o4 * (1.0 / l)
        for pr, psl in enumerate(pair_lanes):
            o2 = jnp.where(first, o4[2 * pr * WINDOW:(2 * pr + 1) * WINDOW],
                           o4[(2 * pr + 1) * WINDOW:(2 * pr + 2) * WINDOW])
            o_ref[rows, psl] = (o2 * g_ref[0, rows, psl].astype(F32)).astype(BF16)

    def block(nl, has_prev):
        pk, pq = positions(nl, has_prev)
        for kvh in range(SWA_KV_HEADS):
            unit(nl, kvh, has_prev, pk, pq)

    pl.when(i == 0)(functools.partial(block, 0, False))
    pl.when(i > 0)(functools.partial(block, 0, True))
    for nl in range(1, nblk):
        block(nl, True)

    y = jnp.dot(om_ref[0], wm_ref[...], preferred_element_type=F32)
    y += jnp.dot(o_ref[...], ws_ref[...], preferred_element_type=F32)
    gate = mod_ref[0][2:3]
    out_ref[0] = _rms(x_ref[0] + gate * y, fg_ref[...])


def _swa_out(sinks, qs, ks, vs, gs, pos_col, pos_row, om, x, mod3, w_m, w_s, final_gain, tq):
    b, s, d = x.shape
    tok = lambda w: pl.BlockSpec((1, tq, w), lambda bi, i: (bi, i, 0))
    full = lambda w: pl.BlockSpec((1, s, w), lambda bi, i: (bi, 0, 0))
    const = lambda shape: pl.BlockSpec(shape, lambda bi, i: (0,) * len(shape))
    return pl.pallas_call(
        functools.partial(_swa_out_kernel, tq=tq),
        grid=(b, s // tq),
        in_specs=[pl.BlockSpec(memory_space=pltpu.SMEM),
                  tok(SWA_WIDTH), full(2 * SWA_KV_WIDTH), full(2 * SWA_KV_WIDTH), tok(SWA_WIDTH), tok(1),
                  pl.BlockSpec((1, s // WINDOW, 1, WINDOW), lambda bi, i: (bi, 0, 0, 0)),
                  tok(MLA_WIDTH), tok(d), pl.BlockSpec((1, 3, d), lambda bi, i: (bi, 0, 0)),
                  const(w_m.shape), const(w_s.shape), const((1, d))],
        out_specs=tok(d),
        out_shape=jax.ShapeDtypeStruct((b, s, d), x.dtype),
        scratch_shapes=[pltpu.VMEM((tq, SWA_WIDTH), BF16)],
        compiler_params=pltpu.CompilerParams(dimension_semantics=("parallel", "arbitrary"),
                                             vmem_limit_bytes=VMEM_LIMIT),
        name="swa_out",
    )(sinks, qs, ks, vs, gs, pos_col, pos_row, om, x, mod3, w_m, w_s, final_gain)


def kernel(x, c, positions, w_ada, b_ada, norm_gain, w_in, q_norm_gain, kv_norm_gain, w_uq, w_ukv, swa_sinks, w_out, final_gain):
    b, s, d = x.shape
    depth = w_ada.shape[0]
    assert depth == 1, "the final rmsnorm is fused into the (single) layer's output projection"
    pos = positions.astype(F32)
    pos_col = pos.reshape(b, s, 1)
    pos_row = pos.reshape(b, s // WINDOW, 1, WINDOW)
    for l in range(depth):
        mod3 = _adaln(c, w_ada, b_ada[l], l).reshape(b, 3, d)
        w_in_p, w_uq_p, w_ukv_p = _pack_weights(w_in[l], w_uq[l], w_ukv[l])
        q, k, v, gm, qs, ks, vs, gs = _inproj(
            x, mod3, pos.reshape(b, 1, s), norm_gain[l].reshape(1, d), q_norm_gain[l].reshape(1, Q_LORA),
            kv_norm_gain[l].reshape(1, KV_LORA), w_in_p, w_uq_p, w_ukv_p, tm=1024)
        om = _mla(q, k, v, gm, tq=512)
        w_o = w_out[l].astype(BF16)
        x = _swa_out(swa_sinks[l].astype(F32), qs, ks, vs, gs, pos_col, pos_row, om, x, mod3,
                     w_o[:MLA_WIDTH], w_o[MLA_WIDTH:], final_gain.reshape(1, d), tq=512)
    return x
```

```python
import functools
import math

import jax
import jax.numpy as jnp
import numpy as np
from jax import lax
from jax.experimental import pallas as pl
from jax.experimental.pallas import tpu as pltpu

F32 = jnp.float32
BF16 = jnp.bfloat16

D_MODEL = 1024
MLA_HEADS = 8
MLA_NOPE = 64
MLA_ROPE = 32
MLA_V = 64
Q_LORA = 384
KV_LORA = 256
MLA_WIDTH = MLA_HEADS * MLA_V
SWA_HEADS = 8
SWA_KV_HEADS = 2
SWA_HEAD_DIM = 64
SWA_GROUP = SWA_HEADS // SWA_KV_HEADS
SWA_WIDTH = SWA_HEADS * SWA_HEAD_DIM
SWA_KV_WIDTH = SWA_KV_HEADS * SWA_HEAD_DIM
WINDOW = 128
ROPE_THETA = 10000.0
EPS = 1e-6
ALIBI_MAX_EXP = 8.0

LANES = 128
HEAD_PAD = LANES
LOG2E = math.log2(math.e)
NEG_BIG = -1e30

C_ZQ = 0
C_KR = C_ZQ + Q_LORA
C_ZKV = C_KR + LANES
C_GM = C_ZKV + KV_LORA
C_QS = C_GM + MLA_WIDTH
C_KS = C_QS + SWA_WIDTH
C_VS = C_KS + SWA_KV_WIDTH
C_GS = C_VS + SWA_KV_WIDTH
D_IN_PACKED = C_GS + SWA_WIDTH

VMEM_LIMIT = 56 * 1024 * 1024


def _swap_halves(w):
    half = w.shape[-1] // 2
    return jnp.concatenate([w[..., half:], w[..., :half]], axis=-1)


def _pack_weights(w_in, w_uq, w_ukv):
    d = w_in.shape[0]
    s = np.cumsum([0, Q_LORA, KV_LORA, MLA_ROPE, MLA_WIDTH, SWA_WIDTH, SWA_KV_WIDTH, SWA_KV_WIDTH, SWA_WIDTH])
    zq, zkv, kr, gm, qs, ks, vs, gs = [w_in[:, s[i]:s[i + 1]] for i in range(8)]
    kr_blk = jnp.concatenate([jnp.zeros((d, MLA_NOPE), w_in.dtype), kr, _swap_halves(kr)], axis=1)
    w_in_p = jnp.concatenate([zq, kr_blk, zkv, gm, qs, ks, vs, gs], axis=1).astype(BF16)

    uq = w_uq.reshape(Q_LORA, MLA_HEADS, MLA_NOPE + MLA_ROPE)
    uq_p = jnp.concatenate([uq, _swap_halves(uq[..., MLA_NOPE:])], axis=-1)
    w_uq_p = uq_p.reshape(Q_LORA, MLA_HEADS * HEAD_PAD).astype(BF16)

    return w_in_p, w_uq_p, w_ukv.astype(BF16)


def _adaln_kernel(c_ref, w_ref, b_ref, o_ref):
    c = c_ref[...]
    a = c * jax.nn.sigmoid(c)
    a_hi = a.astype(BF16)
    a_lo = (a - a_hi.astype(F32)).astype(BF16)
    w = w_ref[...]
    w_hi = w.astype(BF16)
    w_lo = (w - w_hi.astype(F32)).astype(BF16)
    acc = jnp.dot(a_hi, w_hi, preferred_element_type=F32)
    acc += jnp.dot(a_hi, w_lo, preferred_element_type=F32)
    acc += jnp.dot(a_lo, w_hi, preferred_element_type=F32)
    o_ref[...] = acc + b_ref[...]


def _adaln(c, w_ada, b_ada, layer):
    b, d = c.shape
    n = w_ada.shape[2]
    tn = 1024
    return pl.pallas_call(
        _adaln_kernel,
        grid=(n // tn,),
        in_specs=[pl.BlockSpec((b, d), lambda j: (0, 0)),
                  pl.BlockSpec((None, d, tn), lambda j: (layer, 0, j)),
                  pl.BlockSpec((1, tn), lambda j: (0, j))],
        out_specs=pl.BlockSpec((b, tn), lambda j: (0, j)),
        out_shape=jax.ShapeDtypeStruct((b, n), F32),
        compiler_params=pltpu.CompilerParams(dimension_semantics=("arbitrary",), vmem_limit_bytes=VMEM_LIMIT),
        name="adaln",
    )(c, w_ada, b_ada.reshape(1, n))


def _rms(t, gain):
    return t * lax.rsqrt(jnp.mean(t * t, axis=-1, keepdims=True) + EPS) * gain


def _rope_tables(pos_row):
    t = pos_row.shape[1]
    half = MLA_ROPE // 2
    idx = lax.broadcasted_iota(jnp.int32, (half, 1), 0).astype(F32)
    inv = jnp.exp(idx * (-2.0 * math.log(ROPE_THETA) / MLA_ROPE))
    ang = inv * pos_row
    cos, sin = jnp.cos(ang), jnp.sin(ang)
    pad = jnp.zeros((HEAD_PAD - MLA_NOPE - MLA_ROPE, t), F32)
    a_t = jnp.concatenate([jnp.ones((MLA_NOPE, t), F32), cos, cos, pad], axis=0)
    b_t = jnp.concatenate([jnp.zeros((MLA_NOPE, t), F32), -sin, sin, pad], axis=0)
    return a_t.T, b_t.T


def _rope_group(t, a, b):
    return t * a + pltpu.roll(t, LANES - MLA_ROPE, 1) * b


def _dup_halves(t):
    lane = lax.broadcasted_iota(jnp.int32, (1, LANES), 1)
    r = pltpu.roll(t, LANES // 2, 1)
    lo = lane < LANES // 2
    return jnp.concatenate([jnp.where(lo, t, r), jnp.where(lo, r, t)], axis=1)


def _inproj_kernel(x_ref, mod_ref, pos_ref, ng_ref, qg_ref, kvg_ref, win_ref, wuq_ref, wukv_ref,
                   q_ref, k_ref, v_ref, gm_ref, qs_ref, ks_ref, vs_ref, gs_ref):
    x = x_ref[0]
    mod = mod_ref[0]
    shift, scale = mod[0:1], mod[1:2]
    h = _rms(x, ng_ref[...]) * (1.0 + scale) + shift
    hb = h.astype(BF16)

    z = jnp.dot(hb, win_ref[...], preferred_element_type=F32)

    def seg(lo, hi):
        return z[:, lo:hi]

    a, b = _rope_tables(pos_ref[0])
    q_scale = (MLA_NOPE + MLA_ROPE) ** -0.5 * LOG2E

    zqr = seg(C_ZQ, C_ZKV)
    qn = _rms(zqr[:, :Q_LORA], qg_ref[...] * q_scale).astype(BF16)
    q = jnp.dot(qn, wuq_ref[...], preferred_element_type=F32)
    for hd in range(MLA_HEADS):
        sl = slice(hd * HEAD_PAD, (hd + 1) * HEAD_PAD)
        q_ref[0, :, sl] = _rope_group(q[:, sl], a, b).astype(BF16)

    kpe = _rope_group(zqr[:, Q_LORA:], a, b)
    kvn = _rms(seg(C_ZKV, C_GM), kvg_ref[...]).astype(BF16)
    lane = lax.broadcasted_iota(jnp.int32, (1, LANES), 1)
    first = lane < MLA_NOPE
    for pr in range(MLA_HEADS // 2):
        cols = slice(2 * pr * HEAD_PAD, (2 * pr + 2) * HEAD_PAD)
        kv = jnp.dot(kvn, wukv_ref[:, cols], preferred_element_type=F32)
        even, odd = kv[:, :HEAD_PAD], kv[:, HEAD_PAD:]
        k_ref[0, :, cols] = jnp.concatenate([jnp.where(first, even, kpe), jnp.where(first, odd, kpe)],
                                            axis=1).astype(BF16)
        v_ref[0, :, pr * LANES:(pr + 1) * LANES] = jnp.where(first, pltpu.roll(even, MLA_V, 1), odd).astype(BF16)

    g = seg(C_GM, C_QS)
    gm_ref[0] = (g * jax.nn.sigmoid(g)).astype(BF16)
    g = seg(C_GS, D_IN_PACKED)
    gs_ref[0] = (g * jax.nn.sigmoid(g)).astype(BF16)
    qs_ref[0] = (seg(C_QS, C_KS) * (SWA_HEAD_DIM ** -0.5 * LOG2E)).astype(BF16)
    kvs = seg(C_KS, C_GS)
    ks_ref[0] = _dup_halves(kvs[:, :SWA_KV_WIDTH]).astype(BF16)
    vs_ref[0] = _dup_halves(kvs[:, SWA_KV_WIDTH:]).astype(BF16)


def _inproj(x, mod3, pos_row, norm_gain, q_gain, kv_gain, w_in_p, w_uq_p, w_ukv_p, tm):
    b, s, d = x.shape
    const = lambda shape: pl.BlockSpec(shape, lambda bi, i: (0,) * len(shape), pipeline_mode=pl.Buffered(1))
    tok = lambda w: pl.BlockSpec((1, tm, w), lambda bi, i: (bi, i, 0))
    widths = [MLA_HEADS * HEAD_PAD, MLA_HEADS * HEAD_PAD, MLA_WIDTH, MLA_WIDTH,
              SWA_WIDTH, 2 * SWA_KV_WIDTH, 2 * SWA_KV_WIDTH, SWA_WIDTH]
    return pl.pallas_call(
        _inproj_kernel,
        grid=(b, s // tm),
        in_specs=[tok(d),
                  pl.BlockSpec((1, 3, d), lambda bi, i: (bi, 0, 0)),
                  pl.BlockSpec((1, 1, tm), lambda bi, i: (bi, 0, i)),
                  const((1, d)), const((1, Q_LORA)), const((1, KV_LORA)),
                  const(w_in_p.shape), const(w_uq_p.shape), const(w_ukv_p.shape)],
        out_specs=[tok(w) for w in widths],
        out_shape=[jax.ShapeDtypeStruct((b, s, w), BF16) for w in widths],
        compiler_params=pltpu.CompilerParams(dimension_semantics=("parallel", "parallel"),
                                             vmem_limit_bytes=VMEM_LIMIT),
        name="inproj",
    )(x, mod3, pos_row, norm_gain, q_gain, kv_gain, w_in_p, w_uq_p, w_ukv_p)


def _dot_nt(a, b):
    return lax.dot_general(a, b, (((1,), (1,)), ((), ())), preferred_element_type=F32)


def _mla_kernel(q_ref, k_ref, v_ref, g_ref, o_ref, vl_ref, *, tq):
    nq = q_ref.shape[1] // tq
    lane = lax.broadcasted_iota(jnp.int32, (1, LANES), 1)
    first = lane < MLA_V
    row = lax.broadcasted_iota(jnp.int32, (tq, tq), 0)
    col = lax.broadcasted_iota(jnp.int32, (tq, tq), 1)
    causal = col <= row

    v = v_ref[0]
    ones = jnp.ones_like(v)
    vl_ref[0] = jnp.where(first, v, ones)
    vl_ref[1] = jnp.where(first, ones, v)

    def scores(c, kb):
        q = q_ref[0, c * tq:(c + 1) * tq, :]
        k = k_ref[0, kb * tq:(kb + 1) * tq, :]
        return _dot_nt(q[:, :HEAD_PAD], k[:, :HEAD_PAD]), _dot_nt(q[:, HEAD_PAD:], k[:, HEAD_PAD:])

    def pv(p, head, kb):
        return jnp.dot(p.astype(BF16), vl_ref[head, kb * tq:(kb + 1) * tq, :], preferred_element_type=F32)

    tiles = [(c, kb) for c in range(nq) for kb in [c] + list(range(c))]
    nxt = scores(*tiles[0])
    for t, (c, kb) in enumerate(tiles):
        sa, sb = nxt
        nxt = scores(*tiles[t + 1]) if t + 1 < len(tiles) else None
        if kb == c:
            sa = jnp.where(causal, sa, NEG_BIG)
            sb = jnp.where(causal, sb, NEG_BIG)
            ma = jnp.max(sa, axis=-1, keepdims=True)
            mb = jnp.max(sb, axis=-1, keepdims=True)
            acc_a = pv(jnp.exp2(sa - ma), 0, kb)
            acc_b = pv(jnp.exp2(sb - mb), 1, kb)
        else:
            ma_n = jnp.maximum(ma, jnp.max(sa, axis=-1, keepdims=True))
            mb_n = jnp.maximum(mb, jnp.max(sb, axis=-1, keepdims=True))
            acc_a = acc_a * jnp.exp2(ma - ma_n) + pv(jnp.exp2(sa - ma_n), 0, kb)
            acc_b = acc_b * jnp.exp2(mb - mb_n) + pv(jnp.exp2(sb - mb_n), 1, kb)
            ma, mb = ma_n, mb_n
        if t + 1 == len(tiles) or tiles[t + 1][0] != c:
            rows = slice(c * tq, (c + 1) * tq)
            acc = jnp.where(first, acc_a, acc_b)
            den = jnp.where(first, pltpu.roll(acc_a, MLA_V, 1), pltpu.roll(acc_b, MLA_V, 1))
            o_ref[0, rows, :] = (acc / den * g_ref[0, rows, :].astype(F32)).astype(BF16)


def _mla(q, k, v, gm, tq):
    b, s, _ = q.shape
    pairs = MLA_HEADS // 2
    seq = lambda w: pl.BlockSpec((1, s, w), lambda bi, j: (bi, 0, j))
    return pl.pallas_call(
        functools.partial(_mla_kernel, tq=tq),
        grid=(b, pairs),
        in_specs=[seq(2 * HEAD_PAD), seq(2 * HEAD_PAD), seq(2 * MLA_V), seq(2 * MLA_V)],
        out_specs=seq(2 * MLA_V),
        out_shape=jax.ShapeDtypeStruct((b, s, MLA_WIDTH), BF16),
        scratch_shapes=[pltpu.VMEM((2, s, 2 * MLA_V), BF16)],
        compiler_params=pltpu.CompilerParams(dimension_semantics=("parallel", "parallel"),
                                             vmem_limit_bytes=VMEM_LIMIT),
        name="mla",
    )(q, k, v, gm)


def _swa_out_kernel(sink_ref, q_ref, k_ref, v_ref, g_ref, pr_ref, om_ref, x_ref, mod_ref, wm_ref, ws_ref,
                    fg_ref, out_ref, o_ref, *, tq):
    i = pl.program_id(1)
    nblk = tq // WINDOW
    lane = lax.broadcasted_iota(jnp.int32, (1, LANES), 1)
    first = lane < SWA_HEAD_DIM
    row1 = lax.broadcasted_iota(jnp.int32, (WINDOW, WINDOW), 0)
    col1 = lax.broadcasted_iota(jnp.int32, (WINDOW, WINDOW), 1)
    in_cur1 = col1 <= row1
    in_cur = jnp.concatenate([in_cur1] * SWA_GROUP, axis=0)

    def positions(nl, has_prev):
        n = i * nblk + nl
        pk_cur = pr_ref[0, n]
        if has_prev:
            pk_prev = pr_ref[0, n - 1]
            p0 = pk_prev[:, 0:1]
            pk = jnp.where(in_cur1, pk_cur, pk_prev) - p0
        else:
            p0 = pk_cur[:, 0:1]
            pk = jnp.broadcast_to(pk_cur - p0, (WINDOW, WINDOW))
        return pk, jnp.broadcast_to(pk_cur - p0, (WINDOW, WINDOW)).T

    def unit(nl, kvh, has_prev, pk, pq):
        n = i * nblk + nl
        rows = slice(nl * WINDOW, (nl + 1) * WINDOW)
        ksl = slice(kvh * LANES, (kvh + 1) * LANES)
        heads = [kvh * SWA_GROUP + g for g in range(SWA_GROUP)]
        slopes = [LOG2E * 2.0 ** (-ALIBI_MAX_EXP * (h + 1) / SWA_HEADS) for h in heads]
        alibi = jnp.concatenate([sl * pk for sl in slopes], axis=0)
        sink = jnp.concatenate([sink_ref[h] * LOG2E + sl * pq for h, sl in zip(heads, slopes)], axis=0)
        pair_lanes = [slice((heads[0] // 2 + pr) * LANES, (heads[0] // 2 + pr + 1) * LANES)
                      for pr in range(SWA_GROUP // 2)]
        parts = []
        for psl in pair_lanes:
            q2 = q_ref[0, rows, psl]
            zq = jnp.zeros_like(q2)
            parts += [jnp.where(first, q2, zq), jnp.where(first, zq, q2)]
        q4 = jnp.concatenate(parts, axis=0)
        if has_prev:
            st = pl.multiple_of((n - 1) * WINDOW, WINDOW)
            s2 = _dot_nt(q4, k_ref[0, pl.ds(st, 2 * WINDOW), ksl])
            s = jnp.where(in_cur, s2[:, WINDOW:], s2[:, :WINDOW]) + alibi
        else:
            st = pl.multiple_of(n * WINDOW, WINDOW)
            s = _dot_nt(q4, k_ref[0, pl.ds(st, WINDOW), ksl])
            s = jnp.where(in_cur, s + alibi, NEG_BIG)
        m = jnp.maximum(jnp.max(s, axis=-1, keepdims=True), sink)
        p = jnp.exp2(s - m)
        l = jnp.sum(p, axis=-1, keepdims=True) + jnp.exp2(sink - m)
        if has_prev:
            zp = jnp.zeros_like(p)
            p2 = jnp.concatenate([jnp.where(in_cur, zp, p), jnp.where(in_cur, p, zp)], axis=1).astype(BF16)
            o4 = jnp.dot(p2, v_ref[0, pl.ds(st, 2 * WINDOW), ksl], preferred_element_type=F32)
        else:
            o4 = jnp.dot(p.astype(BF16), v_ref[0, pl.ds(st, WINDOW), ksl], preferred_element_type=F32)
        o4 = o4 * (1.0 / l)
        for pr, psl in enumerate(pair_lanes):
            o2 = jnp.where(first, o4[2 * pr * WINDOW:(2 * pr + 1) * WINDOW],
                           o4[(2 * pr + 1) * WINDOW:(2 * pr + 2) * WINDOW])
            o_ref[rows, psl] = (o2 * g_ref[0, rows, psl].astype(F32)).astype(BF16)

    def block(nl, has_prev):
        pk, pq = positions(nl, has_prev)
        for kvh in range(SWA_KV_HEADS):
            unit(nl, kvh, has_prev, pk, pq)

    pl.when(i == 0)(functools.partial(block, 0, False))
    pl.when(i > 0)(functools.partial(block, 0, True))
    for nl in range(1, nblk):
        block(nl, True)

    y = jnp.dot(om_ref[0], wm_ref[...], preferred_element_type=F32)
    y += jnp.dot(o_ref[...], ws_ref[...], preferred_element_type=F32)
    gate = mod_ref[0][2:3]
    out_ref[0] = _rms(x_ref[0] + gate * y, fg_ref[...])


def _swa_out(sinks, qs, ks, vs, gs, pos_row, om, x, mod3, w_m, w_s, final_gain, tq):
    b, s, d = x.shape
    tok = lambda w: pl.BlockSpec((1, tq, w), lambda bi, i: (bi, i, 0))
    full = lambda w: pl.BlockSpec((1, s, w), lambda bi, i: (bi, 0, 0))
    const = lambda shape: pl.BlockSpec(shape, lambda bi, i: (0,) * len(shape))
    return pl.pallas_call(
        functools.partial(_swa_out_kernel, tq=tq),
        grid=(b, s // tq),
        in_specs=[pl.BlockSpec(memory_space=pltpu.SMEM),
                  tok(SWA_WIDTH), full(2 * SWA_KV_WIDTH), full(2 * SWA_KV_WIDTH), tok(SWA_WIDTH),
                  pl.BlockSpec((1, s // WINDOW, 1, WINDOW), lambda bi, i: (bi, 0, 0, 0)),
                  tok(MLA_WIDTH), tok(d), pl.BlockSpec((1, 3, d), lambda bi, i: (bi, 0, 0)),
                  const(w_m.shape), const(w_s.shape), const((1, d))],
        out_specs=tok(d),
        out_shape=jax.ShapeDtypeStruct((b, s, d), x.dtype),
        scratch_shapes=[pltpu.VMEM((tq, SWA_WIDTH), BF16)],
        compiler_params=pltpu.CompilerParams(dimension_semantics=("parallel", "arbitrary"),
                                             vmem_limit_bytes=VMEM_LIMIT),
        name="swa_out",
    )(sinks, qs, ks, vs, gs, pos_row, om, x, mod3, w_m, w_s, final_gain)


def kernel(x, c, positions, w_ada, b_ada, norm_gain, w_in, q_norm_gain, kv_norm_gain, w_uq, w_ukv, swa_sinks, w_out, final_gain):
    b, s, d = x.shape
    depth = w_ada.shape[0]
    assert depth == 1, "the final rmsnorm is fused into the (single) layer's output projection"
    pos = positions.astype(F32)
    pos_row = pos.reshape(b, s // WINDOW, 1, WINDOW)
    for l in range(depth):
        mod3 = _adaln(c, w_ada, b_ada[l], l).reshape(b, 3, d)
        w_in_p, w_uq_p, w_ukv_p = _pack_weights(w_in[l], w_uq[l], w_ukv[l])
        q, k, v, gm, qs, ks, vs, gs = _inproj(
            x, mod3, pos.reshape(b, 1, s), norm_gain[l].reshape(1, d), q_norm_gain[l].reshape(1, Q_LORA),
            kv_norm_gain[l].reshape(1, KV_LORA), w_in_p, w_uq_p, w_ukv_p, tm=1024)
        om = _mla(q, k, v, gm, tq=512)
        w_o = w_out[l].astype(BF16)
        x = _swa_out(swa_sinks[l].astype(F32), qs, ks, vs, gs, pos_row, om, x, mod3,
                     w_o[:MLA_WIDTH], w_o[MLA_WIDTH:], final_gain.reshape(1, d), tq=1024)
    return x
```

```python
import functools
import math

import jax
import jax.numpy as jnp
import numpy as np
from jax import lax
from jax.experimental import pallas as pl
from jax.experimental.pallas import tpu as pltpu

F32 = jnp.float32
BF16 = jnp.bfloat16

MLA_HEADS = 8
MLA_NOPE = 64
MLA_ROPE = 32
MLA_V = 64
Q_LORA = 384
KV_LORA = 256
MLA_WIDTH = MLA_HEADS * MLA_V
SWA_HEADS = 8
SWA_KV_HEADS = 2
SWA_HEAD_DIM = 64
SWA_GROUP = SWA_HEADS // SWA_KV_HEADS
SWA_WIDTH = SWA_HEADS * SWA_HEAD_DIM
SWA_KV_WIDTH = SWA_KV_HEADS * SWA_HEAD_DIM
WINDOW = 128
ROPE_THETA = 10000.0
EPS = 1e-6
ALIBI_MAX_EXP = 8.0

LANES = 128
HEAD_PAD = LANES
LOG2E = math.log2(math.e)
NEG_BIG = -1e30

C_ZQ = 0
C_KR = C_ZQ + Q_LORA
C_ZKV = C_KR + LANES
C_GM = C_ZKV + KV_LORA
C_QS = C_GM + MLA_WIDTH
C_KS = C_QS + SWA_WIDTH
C_GS = C_KS + 2 * SWA_KV_WIDTH
D_IN_PACKED = C_GS + SWA_WIDTH

V7X_VMEM_BYTES = 64 * 1024 * 1024
VMEM_LIMIT = V7X_VMEM_BYTES * 7 // 8

ROWS_INPROJ = 1024
ROWS_MLA = 512
ROWS_SWA_OUT = 1024
COLS_ADALN = 1024


def _swap_halves(w):
    half = w.shape[-1] // 2
    return jnp.concatenate([w[..., half:], w[..., :half]], axis=-1)


def _pack_weights(w_in, w_uq, w_ukv):
    d = w_in.shape[0]
    s = np.cumsum([0, Q_LORA, KV_LORA, MLA_ROPE, MLA_WIDTH, SWA_WIDTH, SWA_KV_WIDTH, SWA_KV_WIDTH, SWA_WIDTH])
    zq, zkv, kr, gm, qs, ks, vs, gs = [w_in[:, s[i]:s[i + 1]] for i in range(8)]
    kr_blk = jnp.concatenate([jnp.zeros((d, MLA_NOPE), w_in.dtype), kr, _swap_halves(kr)], axis=1)
    w_in_p = jnp.concatenate([zq, kr_blk, zkv, gm, qs, ks, vs, gs], axis=1).astype(BF16)

    uq = w_uq.reshape(Q_LORA, MLA_HEADS, MLA_NOPE + MLA_ROPE)
    uq_p = jnp.concatenate([uq, _swap_halves(uq[..., MLA_NOPE:])], axis=-1)
    w_uq_p = uq_p.reshape(Q_LORA, MLA_HEADS * HEAD_PAD).astype(BF16)

    return w_in_p, w_uq_p, w_ukv.astype(BF16)


def _adaln_kernel(c_ref, w_ref, b_ref, o_ref):
    c = c_ref[...]
    a = c * jax.nn.sigmoid(c)
    a_hi = a.astype(BF16)
    a_lo = (a - a_hi.astype(F32)).astype(BF16)
    w = w_ref[...]
    w_hi = w.astype(BF16)
    w_lo = (w - w_hi.astype(F32)).astype(BF16)
    acc = jnp.dot(a_hi, w_hi, preferred_element_type=F32)
    acc += jnp.dot(a_hi, w_lo, preferred_element_type=F32)
    acc += jnp.dot(a_lo, w_hi, preferred_element_type=F32)
    o_ref[...] = acc + b_ref[...]


def _adaln(c, w_ada, b_ada, layer):
    b, d = c.shape
    n = w_ada.shape[2]
    tn = COLS_ADALN
    return pl.pallas_call(
        _adaln_kernel,
        grid=(n // tn,),
        in_specs=[pl.BlockSpec((b, d), lambda j: (0, 0)),
                  pl.BlockSpec((None, d, tn), lambda j: (layer, 0, j)),
                  pl.BlockSpec((1, tn), lambda j: (0, j))],
        out_specs=pl.BlockSpec((b, tn), lambda j: (0, j)),
        out_shape=jax.ShapeDtypeStruct((b, n), F32),
        compiler_params=pltpu.CompilerParams(dimension_semantics=("arbitrary",), vmem_limit_bytes=VMEM_LIMIT),
        name="adaln",
    )(c, w_ada, b_ada.reshape(1, n))


def _rms(t, gain):
    return t * lax.rsqrt(jnp.mean(t * t, axis=-1, keepdims=True) + EPS) * gain


def _rope_tables(pos_row):
    t = pos_row.shape[1]
    half = MLA_ROPE // 2
    idx = lax.broadcasted_iota(jnp.int32, (half, 1), 0).astype(F32)
    inv = jnp.exp(idx * (-2.0 * math.log(ROPE_THETA) / MLA_ROPE))
    ang = inv * pos_row
    cos, sin = jnp.cos(ang), jnp.sin(ang)
    pad = jnp.zeros((HEAD_PAD - MLA_NOPE - MLA_ROPE, t), F32)
    a_t = jnp.concatenate([jnp.ones((MLA_NOPE, t), F32), cos, cos, pad], axis=0)
    b_t = jnp.concatenate([jnp.zeros((MLA_NOPE, t), F32), -sin, sin, pad], axis=0)
    return a_t.T, b_t.T


def _rope_group(t, a, b):
    return t * a + pltpu.roll(t, LANES - MLA_ROPE, 1) * b


def _dup_halves(t):
    lane = lax.broadcasted_iota(jnp.int32, (1, LANES), 1)
    r = pltpu.roll(t, LANES // 2, 1)
    lo = lane < LANES // 2
    return jnp.concatenate([jnp.where(lo, t, r), jnp.where(lo, r, t)], axis=1)


def _inproj_kernel(x_ref, mod_ref, pos_ref, ng_ref, qg_ref, kvg_ref, win_ref, wuq_ref, wukv_ref,
                   q_ref, k_ref, v_ref, gm_ref, qs_ref, ks_ref, vs_ref, gs_ref):
    x = x_ref[0]
    mod = mod_ref[0]
    shift, scale = mod[0:1], mod[1:2]
    h = _rms(x, ng_ref[...]) * (1.0 + scale) + shift
    hb = h.astype(BF16)

    z = jnp.dot(hb, win_ref[...], preferred_element_type=F32)

    def seg(lo, hi):
        return z[:, lo:hi]

    a, b = _rope_tables(pos_ref[0])
    q_scale = (MLA_NOPE + MLA_ROPE) ** -0.5 * LOG2E

    zqr = seg(C_ZQ, C_ZKV)
    qn = _rms(zqr[:, :Q_LORA], qg_ref[...] * q_scale).astype(BF16)
    q = jnp.dot(qn, wuq_ref[...], preferred_element_type=F32)
    for hd in range(MLA_HEADS):
        sl = slice(hd * HEAD_PAD, (hd + 1) * HEAD_PAD)
        q_ref[0, :, sl] = _rope_group(q[:, sl], a, b).astype(BF16)

    kpe = _rope_group(zqr[:, Q_LORA:], a, b)
    kvn = _rms(seg(C_ZKV, C_GM), kvg_ref[...]).astype(BF16)
    lane = lax.broadcasted_iota(jnp.int32, (1, LANES), 1)
    first = lane < MLA_NOPE
    for pr in range(MLA_HEADS // 2):
        cols = slice(2 * pr * HEAD_PAD, (2 * pr + 2) * HEAD_PAD)
        kv = jnp.dot(kvn, wukv_ref[:, cols], preferred_element_type=F32)
        even, odd = kv[:, :HEAD_PAD], kv[:, HEAD_PAD:]
        k_ref[0, :, cols] = jnp.concatenate([jnp.where(first, even, kpe), jnp.where(first, odd, kpe)],
                                            axis=1).astype(BF16)
        v_ref[0, :, pr * LANES:(pr + 1) * LANES] = jnp.where(first, pltpu.roll(even, MLA_V, 1), odd).astype(BF16)

    g = seg(C_GM, C_QS)
    gm_ref[0] = (g * jax.nn.sigmoid(g)).astype(BF16)
    g = seg(C_GS, D_IN_PACKED)
    gs_ref[0] = (g * jax.nn.sigmoid(g)).astype(BF16)
    qs_ref[0] = (seg(C_QS, C_KS) * (SWA_HEAD_DIM ** -0.5 * LOG2E)).astype(BF16)
    kvs = seg(C_KS, C_GS)
    ks_ref[0] = _dup_halves(kvs[:, :SWA_KV_WIDTH]).astype(BF16)
    vs_ref[0] = _dup_halves(kvs[:, SWA_KV_WIDTH:]).astype(BF16)


def _inproj(x, mod3, pos_row, norm_gain, q_gain, kv_gain, w_in_p, w_uq_p, w_ukv_p, tm):
    b, s, d = x.shape
    const = lambda shape: pl.BlockSpec(shape, lambda bi, i: (0,) * len(shape), pipeline_mode=pl.Buffered(1))
    tok = lambda w: pl.BlockSpec((1, tm, w), lambda bi, i: (bi, i, 0))
    widths = [MLA_HEADS * HEAD_PAD, MLA_HEADS * HEAD_PAD, MLA_WIDTH, MLA_WIDTH,
              SWA_WIDTH, 2 * SWA_KV_WIDTH, 2 * SWA_KV_WIDTH, SWA_WIDTH]
    return pl.pallas_call(
        _inproj_kernel,
        grid=(b, s // tm),
        in_specs=[tok(d),
                  pl.BlockSpec((1, 3, d), lambda bi, i: (bi, 0, 0)),
                  pl.BlockSpec((1, 1, tm), lambda bi, i: (bi, 0, i)),
                  const((1, d)), const((1, Q_LORA)), const((1, KV_LORA)),
                  const(w_in_p.shape), const(w_uq_p.shape), const(w_ukv_p.shape)],
        out_specs=[tok(w) for w in widths],
        out_shape=[jax.ShapeDtypeStruct((b, s, w), BF16) for w in widths],
        compiler_params=pltpu.CompilerParams(dimension_semantics=("parallel", "parallel"),
                                             vmem_limit_bytes=VMEM_LIMIT),
        name="inproj",
    )(x, mod3, pos_row, norm_gain, q_gain, kv_gain, w_in_p, w_uq_p, w_ukv_p)


def _dot_nt(a, b):
    return lax.dot_general(a, b, (((1,), (1,)), ((), ())), preferred_element_type=F32)


def _mla_kernel(q_ref, k_ref, v_ref, g_ref, o_ref, vl_ref, *, tq):
    nq = q_ref.shape[1] // tq
    lane = lax.broadcasted_iota(jnp.int32, (1, LANES), 1)
    first = lane < MLA_V
    row = lax.broadcasted_iota(jnp.int32, (tq, tq), 0)
    col = lax.broadcasted_iota(jnp.int32, (tq, tq), 1)
    causal = col <= row

    v = v_ref[0]
    ones = jnp.ones_like(v)
    vl_ref[0] = jnp.where(first, v, ones)
    vl_ref[1] = jnp.where(first, ones, v)

    def scores(c, kb):
        q = q_ref[0, c * tq:(c + 1) * tq, :]
        k = k_ref[0, kb * tq:(kb + 1) * tq, :]
        return _dot_nt(q[:, :HEAD_PAD], k[:, :HEAD_PAD]), _dot_nt(q[:, HEAD_PAD:], k[:, HEAD_PAD:])

    def pv(p, head, kb):
        return jnp.dot(p.astype(BF16), vl_ref[head, kb * tq:(kb + 1) * tq, :], preferred_element_type=F32)

    tiles = [(c, kb) for c in range(nq) for kb in [c] + list(range(c))]
    nxt = scores(*tiles[0])
    for t, (c, kb) in enumerate(tiles):
        sa, sb = nxt
        nxt = scores(*tiles[t + 1]) if t + 1 < len(tiles) else None
        if kb == c:
            sa = jnp.where(causal, sa, NEG_BIG)
            sb = jnp.where(causal, sb, NEG_BIG)
            ma = jnp.max(sa, axis=-1, keepdims=True)
            mb = jnp.max(sb, axis=-1, keepdims=True)
            acc_a = pv(jnp.exp2(sa - ma), 0, kb)
            acc_b = pv(jnp.exp2(sb - mb), 1, kb)
        else:
            ma_n = jnp.maximum(ma, jnp.max(sa, axis=-1, keepdims=True))
            mb_n = jnp.maximum(mb, jnp.max(sb, axis=-1, keepdims=True))
            acc_a = acc_a * jnp.exp2(ma - ma_n) + pv(jnp.exp2(sa - ma_n), 0, kb)
            acc_b = acc_b * jnp.exp2(mb - mb_n) + pv(jnp.exp2(sb - mb_n), 1, kb)
            ma, mb = ma_n, mb_n
        if t + 1 == len(tiles) or tiles[t + 1][0] != c:
            rows = slice(c * tq, (c + 1) * tq)
            acc = jnp.where(first, acc_a, acc_b)
            den = jnp.where(first, pltpu.roll(acc_a, MLA_V, 1), pltpu.roll(acc_b, MLA_V, 1))
            o_ref[0, rows, :] = (acc / den * g_ref[0, rows, :].astype(F32)).astype(BF16)


def _mla(q, k, v, gm, tq):
    b, s, _ = q.shape
    pairs = MLA_HEADS // 2
    seq = lambda w: pl.BlockSpec((1, s, w), lambda bi, j: (bi, 0, j))
    return pl.pallas_call(
        functools.partial(_mla_kernel, tq=tq),
        grid=(b, pairs),
        in_specs=[seq(2 * HEAD_PAD), seq(2 * HEAD_PAD), seq(2 * MLA_V), seq(2 * MLA_V)],
        out_specs=seq(2 * MLA_V),
        out_shape=jax.ShapeDtypeStruct((b, s, MLA_WIDTH), BF16),
        scratch_shapes=[pltpu.VMEM((2, s, 2 * MLA_V), BF16)],
        compiler_params=pltpu.CompilerParams(dimension_semantics=("parallel", "parallel"),
                                             vmem_limit_bytes=VMEM_LIMIT),
        name="mla",
    )(q, k, v, gm)


def _swa_out_kernel(sink_ref, q_ref, k_ref, v_ref, g_ref, pr_ref, om_ref, x_ref, mod_ref, wm_ref, ws_ref,
                    fg_ref, out_ref, o_ref, *, tq):
    i = pl.program_id(1)
    nblk = tq // WINDOW
    lane = lax.broadcasted_iota(jnp.int32, (1, LANES), 1)
    first = lane < SWA_HEAD_DIM
    row1 = lax.broadcasted_iota(jnp.int32, (WINDOW, WINDOW), 0)
    col1 = lax.broadcasted_iota(jnp.int32, (WINDOW, WINDOW), 1)
    in_cur1 = col1 <= row1
    in_cur = jnp.concatenate([in_cur1] * SWA_GROUP, axis=0)

    def positions(nl, has_prev):
        n = i * nblk + nl
        pk_cur = pr_ref[0, n]
        if has_prev:
            pk_prev = pr_ref[0, n - 1]
            p0 = pk_prev[:, 0:1]
            pk = jnp.where(in_cur1, pk_cur, pk_prev) - p0
        else:
            p0 = pk_cur[:, 0:1]
            pk = jnp.broadcast_to(pk_cur - p0, (WINDOW, WINDOW))
        return pk, jnp.broadcast_to(pk_cur - p0, (WINDOW, WINDOW)).T

    def unit(nl, kvh, has_prev, pk, pq):
        n = i * nblk + nl
        rows = slice(nl * WINDOW, (nl + 1) * WINDOW)
        ksl = slice(kvh * LANES, (kvh + 1) * LANES)
        heads = [kvh * SWA_GROUP + g for g in range(SWA_GROUP)]
        slopes = [LOG2E * 2.0 ** (-ALIBI_MAX_EXP * (h + 1) / SWA_HEADS) for h in heads]
        alibi = jnp.concatenate([sl * pk for sl in slopes], axis=0)
        sink = jnp.concatenate([sink_ref[h] * LOG2E + sl * pq for h, sl in zip(heads, slopes)], axis=0)
        pair_lanes = [slice((heads[0] // 2 + pr) * LANES, (heads[0] // 2 + pr + 1) * LANES)
                      for pr in range(SWA_GROUP // 2)]
        parts = []
        for psl in pair_lanes:
            q2 = q_ref[0, rows, psl]
            zq = jnp.zeros_like(q2)
            parts += [jnp.where(first, q2, zq), jnp.where(first, zq, q2)]
        q4 = jnp.concatenate(parts, axis=0)
        if has_prev:
            st = pl.multiple_of((n - 1) * WINDOW, WINDOW)
            s2 = _dot_nt(q4, k_ref[0, pl.ds(st, 2 * WINDOW), ksl])
            s = jnp.where(in_cur, s2[:, WINDOW:], s2[:, :WINDOW]) + alibi
        else:
            st = pl.multiple_of(n * WINDOW, WINDOW)
            s = _dot_nt(q4, k_ref[0, pl.ds(st, WINDOW), ksl])
            s = jnp.where(in_cur, s + alibi, NEG_BIG)
        m = jnp.maximum(jnp.max(s, axis=-1, keepdims=True), sink)
        p = jnp.exp2(s - m)
        l = jnp.sum(p, axis=-1, keepdims=True) + jnp.exp2(sink - m)
        if has_prev:
            zp = jnp.zeros_like(p)
            p2 = jnp.concatenate([jnp.where(in_cur, zp, p), jnp.where(in_cur, p, zp)], axis=1).astype(BF16)
            o4 = jnp.dot(p2, v_ref[0, pl.ds(st, 2 * WINDOW), ksl], preferred_element_type=F32)
        else:
            o4 = jnp.dot(p.astype(BF16), v_ref[0, pl.ds(st, WINDOW), ksl], preferred_element_type=F32)
        o4 = o4 * (1.0 / l)
        for pr, psl in enumerate(pair_lanes):
            o2 = jnp.where(first, o4[2 * pr * WINDOW:(2 * pr + 1) * WINDOW],
                           o4[(2 * pr + 1) * WINDOW:(2 * pr + 2) * WINDOW])
            o_ref[rows, psl] = (o2 * g_ref[0, rows, psl].astype(F32)).astype(BF16)

    def block(nl, has_prev):
        pk, pq = positions(nl, has_prev)
        for kvh in range(SWA_KV_HEADS):
            unit(nl, kvh, has_prev, pk, pq)

    pl.when(i == 0)(functools.partial(block, 0, False))
    pl.when(i > 0)(functools.partial(block, 0, True))
    for nl in range(1, nblk):
        block(nl, True)

    y = jnp.dot(om_ref[0], wm_ref[...], preferred_element_type=F32)
    y += jnp.dot(o_ref[...], ws_ref[...], preferred_element_type=F32)
    gate = mod_ref[0][2:3]
    out_ref[0] = _rms(x_ref[0] + gate * y, fg_ref[...])


def _swa_out(sinks, qs, ks, vs, gs, pos_row, om, x, mod3, w_m, w_s, final_gain, tq):
    b, s, d = x.shape
    tok = lambda w: pl.BlockSpec((1, tq, w), lambda bi, i: (bi, i, 0))
    full = lambda w: pl.BlockSpec((1, s, w), lambda bi, i: (bi, 0, 0))
    const = lambda shape: pl.BlockSpec(shape, lambda bi, i: (0,) * len(shape))
    return pl.pallas_call(
        functools.partial(_swa_out_kernel, tq=tq),
        grid=(b, s // tq),
        in_specs=[pl.BlockSpec(memory_space=pltpu.SMEM),
                  tok(SWA_WIDTH), full(2 * SWA_KV_WIDTH), full(2 * SWA_KV_WIDTH), tok(SWA_WIDTH),
                  pl.BlockSpec((1, s // WINDOW, 1, WINDOW), lambda bi, i: (bi, 0, 0, 0)),
                  tok(MLA_WIDTH), tok(d), pl.BlockSpec((1, 3, d), lambda bi, i: (bi, 0, 0)),
                  const(w_m.shape), const(w_s.shape), const((1, d))],
        out_specs=tok(d),
        out_shape=jax.ShapeDtypeStruct((b, s, d), x.dtype),
        scratch_shapes=[pltpu.VMEM((tq, SWA_WIDTH), BF16)],
        compiler_params=pltpu.CompilerParams(dimension_semantics=("parallel", "arbitrary"),
                                             vmem_limit_bytes=VMEM_LIMIT),
        name="swa_out",
    )(sinks, qs, ks, vs, gs, pos_row, om, x, mod3, w_m, w_s, final_gain)


def kernel(x, c, positions, w_ada, b_ada, norm_gain, w_in, q_norm_gain, kv_norm_gain, w_uq, w_ukv, swa_sinks, w_out, final_gain):
    b, s, d = x.shape
    depth = w_ada.shape[0]
    assert depth == 1, "the final rmsnorm is fused into the (single) layer's output projection"
    pos = positions.astype(F32)
    pos_row = pos.reshape(b, s // WINDOW, 1, WINDOW)
    for l in range(depth):
        mod3 = _adaln(c, w_ada, b_ada[l], l).reshape(b, 3, d)
        w_in_p, w_uq_p, w_ukv_p = _pack_weights(w_in[l], w_uq[l], w_ukv[l])
        q, k, v, gm, qs, ks, vs, gs = _inproj(
            x, mod3, pos.reshape(b, 1, s), norm_gain[l].reshape(1, d), q_norm_gain[l].reshape(1, Q_LORA),
            kv_norm_gain[l].reshape(1, KV_LORA), w_in_p, w_uq_p, w_ukv_p, tm=ROWS_INPROJ)
        om = _mla(q, k, v, gm, tq=ROWS_MLA)
        w_o = w_out[l].astype(BF16)
        x = _swa_out(swa_sinks[l].astype(F32), qs, ks, vs, gs, pos_row, om, x, mod3,
                     w_o[:MLA_WIDTH], w_o[MLA_WIDTH:], final_gain.reshape(1, d), tq=ROWS_SWA_OUT)
    return x
```

```python
import functools
import math

import jax
import jax.numpy as jnp
import numpy as np
from jax import lax
from jax.experimental import pallas as pl
from jax.experimental.pallas import tpu as pltpu

F32 = jnp.float32
BF16 = jnp.bfloat16

MLA_HEADS = 8
MLA_NOPE = 64
MLA_ROPE = 32
MLA_V = 64
Q_LORA = 384
KV_LORA = 256
MLA_WIDTH = MLA_HEADS * MLA_V
SWA_HEADS = 8
SWA_KV_HEADS = 2
SWA_HEAD_DIM = 64
SWA_GROUP = SWA_HEADS // SWA_KV_HEADS
SWA_WIDTH = SWA_HEADS * SWA_HEAD_DIM
SWA_KV_WIDTH = SWA_KV_HEADS * SWA_HEAD_DIM
WINDOW = 128
ROPE_THETA = 10000.0
EPS = 1e-6
ALIBI_MAX_EXP = 8.0

LANES = 128
HEAD_PAD = LANES
LOG2E = math.log2(math.e)
NEG_BIG = -1e30

C_ZQ = 0
C_KR = C_ZQ + Q_LORA
C_ZKV = C_KR + LANES
C_GM = C_ZKV + KV_LORA
C_QS = C_GM + MLA_WIDTH
C_KS = C_QS + SWA_WIDTH
C_GS = C_KS + 2 * SWA_KV_WIDTH
D_IN_PACKED = C_GS + SWA_WIDTH

V7X_VMEM_BYTES = 64 * 1024 * 1024
VMEM_LIMIT = V7X_VMEM_BYTES * 7 // 8

ROWS_INPROJ = 1024
ROWS_MLA = 512
MLA_PAIRS_PER_STEP = 2
ROWS_SWA_OUT = 1024
COLS_ADALN = 1024


def _swap_halves(w):
    half = w.shape[-1] // 2
    return jnp.concatenate([w[..., half:], w[..., :half]], axis=-1)


def _pack_weights(w_in, w_uq, w_ukv):
    d = w_in.shape[0]
    s = np.cumsum([0, Q_LORA, KV_LORA, MLA_ROPE, MLA_WIDTH, SWA_WIDTH, SWA_KV_WIDTH, SWA_KV_WIDTH, SWA_WIDTH])
    zq, zkv, kr, gm, qs, ks, vs, gs = [w_in[:, s[i]:s[i + 1]] for i in range(8)]
    kr_blk = jnp.concatenate([jnp.zeros((d, MLA_NOPE), w_in.dtype), kr, _swap_halves(kr)], axis=1)
    w_in_p = jnp.concatenate([zq, kr_blk, zkv, gm, qs, ks, vs, gs], axis=1).astype(BF16)

    uq = w_uq.reshape(Q_LORA, MLA_HEADS, MLA_NOPE + MLA_ROPE)
    uq_p = jnp.concatenate([uq, _swap_halves(uq[..., MLA_NOPE:])], axis=-1)
    w_uq_p = uq_p.reshape(Q_LORA, MLA_HEADS * HEAD_PAD).astype(BF16)

    return w_in_p, w_uq_p, w_ukv.astype(BF16)


def _adaln_kernel(c_ref, w_ref, b_ref, o_ref):
    c = c_ref[...]
    a = c * jax.nn.sigmoid(c)
    a_hi = a.astype(BF16)
    a_lo = (a - a_hi.astype(F32)).astype(BF16)
    w = w_ref[...]
    w_hi = w.astype(BF16)
    w_lo = (w - w_hi.astype(F32)).astype(BF16)
    acc = jnp.dot(a_hi, w_hi, preferred_element_type=F32)
    acc += jnp.dot(a_hi, w_lo, preferred_element_type=F32)
    acc += jnp.dot(a_lo, w_hi, preferred_element_type=F32)
    o_ref[...] = acc + b_ref[...]


def _adaln(c, w_ada, b_ada, layer):
    b, d = c.shape
    n = w_ada.shape[2]
    tn = COLS_ADALN
    return pl.pallas_call(
        _adaln_kernel,
        grid=(n // tn,),
        in_specs=[pl.BlockSpec((b, d), lambda j: (0, 0)),
                  pl.BlockSpec((None, d, tn), lambda j: (layer, 0, j)),
                  pl.BlockSpec((1, tn), lambda j: (0, j))],
        out_specs=pl.BlockSpec((b, tn), lambda j: (0, j)),
        out_shape=jax.ShapeDtypeStruct((b, n), F32),
        compiler_params=pltpu.CompilerParams(dimension_semantics=("arbitrary",), vmem_limit_bytes=VMEM_LIMIT),
        name="adaln",
    )(c, w_ada, b_ada.reshape(1, n))


def _rms(t, gain):
    return t * lax.rsqrt(jnp.mean(t * t, axis=-1, keepdims=True) + EPS) * gain


def _rope_tables(pos_row):
    t = pos_row.shape[1]
    half = MLA_ROPE // 2
    idx = lax.broadcasted_iota(jnp.int32, (half, 1), 0).astype(F32)
    inv = jnp.exp(idx * (-2.0 * math.log(ROPE_THETA) / MLA_ROPE))
    ang = inv * pos_row
    cos, sin = jnp.cos(ang), jnp.sin(ang)
    pad = jnp.zeros((HEAD_PAD - MLA_NOPE - MLA_ROPE, t), F32)
    a_t = jnp.concatenate([jnp.ones((MLA_NOPE, t), F32), cos, cos, pad], axis=0)
    b_t = jnp.concatenate([jnp.zeros((MLA_NOPE, t), F32), -sin, sin, pad], axis=0)
    return a_t.T, b_t.T


def _rope_group(t, a, b):
    return t * a + pltpu.roll(t, LANES - MLA_ROPE, 1) * b


def _dup_halves(t):
    lane = lax.broadcasted_iota(jnp.int32, (1, LANES), 1)
    r = pltpu.roll(t, LANES // 2, 1)
    lo = lane < LANES // 2
    return jnp.concatenate([jnp.where(lo, t, r), jnp.where(lo, r, t)], axis=1)


def _inproj_kernel(x_ref, mod_ref, pos_ref, ng_ref, qg_ref, kvg_ref, win_ref, wuq_ref, wukv_ref,
                   q_ref, k_ref, v_ref, gm_ref, qs_ref, ks_ref, vs_ref, gs_ref):
    x = x_ref[0]
    mod = mod_ref[0]
    shift, scale = mod[0:1], mod[1:2]
    h = _rms(x, ng_ref[...]) * (1.0 + scale) + shift
    hb = h.astype(BF16)

    z = jnp.dot(hb, win_ref[...], preferred_element_type=F32)

    def seg(lo, hi):
        return z[:, lo:hi]

    a, b = _rope_tables(pos_ref[0])
    q_scale = (MLA_NOPE + MLA_ROPE) ** -0.5 * LOG2E

    zqr = seg(C_ZQ, C_ZKV)
    qn = _rms(zqr[:, :Q_LORA], qg_ref[...] * q_scale).astype(BF16)
    q = jnp.dot(qn, wuq_ref[...], preferred_element_type=F32)
    for hd in range(MLA_HEADS):
        sl = slice(hd * HEAD_PAD, (hd + 1) * HEAD_PAD)
        q_ref[0, :, sl] = _rope_group(q[:, sl], a, b).astype(BF16)

    kpe = _rope_group(zqr[:, Q_LORA:], a, b)
    kvn = _rms(seg(C_ZKV, C_GM), kvg_ref[...]).astype(BF16)
    lane = lax.broadcasted_iota(jnp.int32, (1, LANES), 1)
    first = lane < MLA_NOPE
    for pr in range(MLA_HEADS // 2):
        cols = slice(2 * pr * HEAD_PAD, (2 * pr + 2) * HEAD_PAD)
        kv = jnp.dot(kvn, wukv_ref[:, cols], preferred_element_type=F32)
        even, odd = kv[:, :HEAD_PAD], kv[:, HEAD_PAD:]
        k_ref[0, :, cols] = jnp.concatenate([jnp.where(first, even, kpe), jnp.where(first, odd, kpe)],
                                            axis=1).astype(BF16)
        v_ref[0, :, pr * LANES:(pr + 1) * LANES] = jnp.where(first, pltpu.roll(even, MLA_V, 1), odd).astype(BF16)

    g = seg(C_GM, C_QS)
    gm_ref[0] = (g * jax.nn.sigmoid(g)).astype(BF16)
    g = seg(C_GS, D_IN_PACKED)
    gs_ref[0] = (g * jax.nn.sigmoid(g)).astype(BF16)
    qs_ref[0] = (seg(C_QS, C_KS) * (SWA_HEAD_DIM ** -0.5 * LOG2E)).astype(BF16)
    kvs = seg(C_KS, C_GS)
    ks_ref[0] = _dup_halves(kvs[:, :SWA_KV_WIDTH]).astype(BF16)
    vs_ref[0] = _dup_halves(kvs[:, SWA_KV_WIDTH:]).astype(BF16)


def _inproj(x, mod3, pos_row, norm_gain, q_gain, kv_gain, w_in_p, w_uq_p, w_ukv_p, tm):
    b, s, d = x.shape
    const = lambda shape: pl.BlockSpec(shape, lambda bi, i: (0,) * len(shape), pipeline_mode=pl.Buffered(1))
    tok = lambda w: pl.BlockSpec((1, tm, w), lambda bi, i: (bi, i, 0))
    widths = [MLA_HEADS * HEAD_PAD, MLA_HEADS * HEAD_PAD, MLA_WIDTH, MLA_WIDTH,
              SWA_WIDTH, 2 * SWA_KV_WIDTH, 2 * SWA_KV_WIDTH, SWA_WIDTH]
    return pl.pallas_call(
        _inproj_kernel,
        grid=(b, s // tm),
        in_specs=[tok(d),
                  pl.BlockSpec((1, 3, d), lambda bi, i: (bi, 0, 0)),
                  pl.BlockSpec((1, 1, tm), lambda bi, i: (bi, 0, i)),
                  const((1, d)), const((1, Q_LORA)), const((1, KV_LORA)),
                  const(w_in_p.shape), const(w_uq_p.shape), const(w_ukv_p.shape)],
        out_specs=[tok(w) for w in widths],
        out_shape=[jax.ShapeDtypeStruct((b, s, w), BF16) for w in widths],
        compiler_params=pltpu.CompilerParams(dimension_semantics=("parallel", "parallel"),
                                             vmem_limit_bytes=VMEM_LIMIT),
        name="inproj",
    )(x, mod3, pos_row, norm_gain, q_gain, kv_gain, w_in_p, w_uq_p, w_ukv_p)


def _dot_nt(a, b):
    return lax.dot_general(a, b, (((1,), (1,)), ((), ())), preferred_element_type=F32)


def _mla_kernel(q_ref, k_ref, v_ref, g_ref, o_ref, vl_ref, *, tq):
    nq = q_ref.shape[1] // tq
    npairs = v_ref.shape[2] // LANES
    lane = lax.broadcasted_iota(jnp.int32, (1, LANES), 1)
    first = lane < MLA_V
    row = lax.broadcasted_iota(jnp.int32, (tq, tq), 0)
    col = lax.broadcasted_iota(jnp.int32, (tq, tq), 1)
    causal = col <= row

    for pr in range(npairs):
        v = v_ref[0, :, pr * LANES:(pr + 1) * LANES]
        ones = jnp.ones_like(v)
        vl_ref[2 * pr] = jnp.where(first, v, ones)
        vl_ref[2 * pr + 1] = jnp.where(first, ones, v)

    def scores(pr, c, kb):
        q = q_ref[0, c * tq:(c + 1) * tq, 2 * pr * HEAD_PAD:(2 * pr + 2) * HEAD_PAD]
        k = k_ref[0, kb * tq:(kb + 1) * tq, 2 * pr * HEAD_PAD:(2 * pr + 2) * HEAD_PAD]
        return _dot_nt(q[:, :HEAD_PAD], k[:, :HEAD_PAD]), _dot_nt(q[:, HEAD_PAD:], k[:, HEAD_PAD:])

    def pv(p, head, kb):
        return jnp.dot(p.astype(BF16), vl_ref[head, kb * tq:(kb + 1) * tq, :], preferred_element_type=F32)

    tiles = [(pr, c, kb) for pr in range(npairs) for c in range(nq) for kb in [c] + list(range(c))]
    nxt = scores(*tiles[0])
    for t, (pr, c, kb) in enumerate(tiles):
        sa, sb = nxt
        nxt = scores(*tiles[t + 1]) if t + 1 < len(tiles) else None
        if kb == c:
            sa = jnp.where(causal, sa, NEG_BIG)
            sb = jnp.where(causal, sb, NEG_BIG)
            ma = jnp.max(sa, axis=-1, keepdims=True)
            mb = jnp.max(sb, axis=-1, keepdims=True)
            acc_a = pv(jnp.exp2(sa - ma), 2 * pr, kb)
            acc_b = pv(jnp.exp2(sb - mb), 2 * pr + 1, kb)
        else:
            ma_n = jnp.maximum(ma, jnp.max(sa, axis=-1, keepdims=True))
            mb_n = jnp.maximum(mb, jnp.max(sb, axis=-1, keepdims=True))
            acc_a = acc_a * jnp.exp2(ma - ma_n) + pv(jnp.exp2(sa - ma_n), 2 * pr, kb)
            acc_b = acc_b * jnp.exp2(mb - mb_n) + pv(jnp.exp2(sb - mb_n), 2 * pr + 1, kb)
            ma, mb = ma_n, mb_n
        if t + 1 == len(tiles) or tiles[t + 1][:2] != (pr, c):
            rows, lanes = slice(c * tq, (c + 1) * tq), slice(pr * LANES, (pr + 1) * LANES)
            acc = jnp.where(first, acc_a, acc_b)
            den = jnp.where(first, pltpu.roll(acc_a, MLA_V, 1), pltpu.roll(acc_b, MLA_V, 1))
            o_ref[0, rows, lanes] = (acc / den * g_ref[0, rows, lanes].astype(F32)).astype(BF16)


def _mla(q, k, v, gm, tq, pairs_per_step):
    b, s, _ = q.shape
    steps = MLA_HEADS // 2 // pairs_per_step
    seq = lambda w: pl.BlockSpec((1, s, pairs_per_step * w), lambda bi, j: (bi, 0, j))
    return pl.pallas_call(
        functools.partial(_mla_kernel, tq=tq),
        grid=(b, steps),
        in_specs=[seq(2 * HEAD_PAD), seq(2 * HEAD_PAD), seq(2 * MLA_V), seq(2 * MLA_V)],
        out_specs=seq(2 * MLA_V),
        out_shape=jax.ShapeDtypeStruct((b, s, MLA_WIDTH), BF16),
        scratch_shapes=[pltpu.VMEM((2 * pairs_per_step, s, 2 * MLA_V), BF16)],
        compiler_params=pltpu.CompilerParams(dimension_semantics=("parallel", "parallel"),
                                             vmem_limit_bytes=VMEM_LIMIT),
        name="mla",
    )(q, k, v, gm)


def _swa_out_kernel(sink_ref, q_ref, k_ref, v_ref, g_ref, pr_ref, om_ref, x_ref, mod_ref, wm_ref, ws_ref,
                    fg_ref, out_ref, o_ref, *, tq):
    i = pl.program_id(1)
    nblk = tq // WINDOW
    lane = lax.broadcasted_iota(jnp.int32, (1, LANES), 1)
    first = lane < SWA_HEAD_DIM
    row1 = lax.broadcasted_iota(jnp.int32, (WINDOW, WINDOW), 0)
    col1 = lax.broadcasted_iota(jnp.int32, (WINDOW, WINDOW), 1)
    in_cur1 = col1 <= row1
    in_cur = jnp.concatenate([in_cur1] * SWA_GROUP, axis=0)

    def positions(nl, has_prev):
        n = i * nblk + nl
        pk_cur = pr_ref[0, n]
        if has_prev:
            pk_prev = pr_ref[0, n - 1]
            p0 = pk_prev[:, 0:1]
            pk = jnp.where(in_cur1, pk_cur, pk_prev) - p0
        else:
            p0 = pk_cur[:, 0:1]
            pk = jnp.broadcast_to(pk_cur - p0, (WINDOW, WINDOW))
        return pk, jnp.broadcast_to(pk_cur - p0, (WINDOW, WINDOW)).T

    def unit(nl, kvh, has_prev, pk, pq):
        n = i * nblk + nl
        rows = slice(nl * WINDOW, (nl + 1) * WINDOW)
        ksl = slice(kvh * LANES, (kvh + 1) * LANES)
        heads = [kvh * SWA_GROUP + g for g in range(SWA_GROUP)]
        slopes = [LOG2E * 2.0 ** (-ALIBI_MAX_EXP * (h + 1) / SWA_HEADS) for h in heads]
        alibi = jnp.concatenate([sl * pk for sl in slopes], axis=0)
        sink = jnp.concatenate([sink_ref[h] * LOG2E + sl * pq for h, sl in zip(heads, slopes)], axis=0)
        pair_lanes = [slice((heads[0] // 2 + pr) * LANES, (heads[0] // 2 + pr + 1) * LANES)
                      for pr in range(SWA_GROUP // 2)]
        parts = []
        for psl in pair_lanes:
            q2 = q_ref[0, rows, psl]
            zq = jnp.zeros_like(q2)
            parts += [jnp.where(first, q2, zq), jnp.where(first, zq, q2)]
        q4 = jnp.concatenate(parts, axis=0)
        if has_prev:
            st = pl.multiple_of((n - 1) * WINDOW, WINDOW)
            s2 = _dot_nt(q4, k_ref[0, pl.ds(st, 2 * WINDOW), ksl])
            s = jnp.where(in_cur, s2[:, WINDOW:], s2[:, :WINDOW]) + alibi
        else:
            st = pl.multiple_of(n * WINDOW, WINDOW)
            s = _dot_nt(q4, k_ref[0, pl.ds(st, WINDOW), ksl])
            s = jnp.where(in_cur, s + alibi, NEG_BIG)
        m = jnp.maximum(jnp.max(s, axis=-1, keepdims=True), sink)
        p = jnp.exp2(s - m)
        l = jnp.sum(p, axis=-1, keepdims=True) + jnp.exp2(sink - m)
        if has_prev:
            zp = jnp.zeros_like(p)
            p2 = jnp.concatenate([jnp.where(in_cur, zp, p), jnp.where(in_cur, p, zp)], axis=1).astype(BF16)
            o4 = jnp.dot(p2, v_ref[0, pl.ds(st, 2 * WINDOW), ksl], preferred_element_type=F32)
        else:
            o4 = jnp.dot(p.astype(BF16), v_ref[0, pl.ds(st, WINDOW), ksl], preferred_element_type=F32)
        o4 = o4 * (1.0 / l)
        for pr, psl in enumerate(pair_lanes):
            o2 = jnp.where(first, o4[2 * pr * WINDOW:(2 * pr + 1) * WINDOW],
                           o4[(2 * pr + 1) * WINDOW:(2 * pr + 2) * WINDOW])
            o_ref[rows, psl] = (o2 * g_ref[0, rows, psl].astype(F32)).astype(BF16)

    def block(nl, has_prev):
        pk, pq = positions(nl, has_prev)
        for kvh in range(SWA_KV_HEADS):
            unit(nl, kvh, has_prev, pk, pq)

    pl.when(i == 0)(functools.partial(block, 0, False))
    pl.when(i > 0)(functools.partial(block, 0, True))
    for nl in range(1, nblk):
        block(nl, True)

    y = jnp.dot(om_ref[0], wm_ref[...], preferred_element_type=F32)
    y += jnp.dot(o_ref[...], ws_ref[...], preferred_element_type=F32)
    gate = mod_ref[0][2:3]
    out_ref[0] = _rms(x_ref[0] + gate * y, fg_ref[...])


def _swa_out(sinks, qs, ks, vs, gs, pos_row, om, x, mod3, w_m, w_s, final_gain, tq):
    b, s, d = x.shape
    tok = lambda w: pl.BlockSpec((1, tq, w), lambda bi, i: (bi, i, 0))
    full = lambda w: pl.BlockSpec((1, s, w), lambda bi, i: (bi, 0, 0))
    const = lambda shape: pl.BlockSpec(shape, lambda bi, i: (0,) * len(shape))
    return pl.pallas_call(
        functools.partial(_swa_out_kernel, tq=tq),
        grid=(b, s // tq),
        in_specs=[pl.BlockSpec(memory_space=pltpu.SMEM),
                  tok(SWA_WIDTH), full(2 * SWA_KV_WIDTH), full(2 * SWA_KV_WIDTH), tok(SWA_WIDTH),
                  pl.BlockSpec((1, s // WINDOW, 1, WINDOW), lambda bi, i: (bi, 0, 0, 0)),
                  tok(MLA_WIDTH), tok(d), pl.BlockSpec((1, 3, d), lambda bi, i: (bi, 0, 0)),
                  const(w_m.shape), const(w_s.shape), const((1, d))],
        out_specs=tok(d),
        out_shape=jax.ShapeDtypeStruct((b, s, d), x.dtype),
        scratch_shapes=[pltpu.VMEM((tq, SWA_WIDTH), BF16)],
        compiler_params=pltpu.CompilerParams(dimension_semantics=("parallel", "arbitrary"),
                                             vmem_limit_bytes=VMEM_LIMIT),
        name="swa_out",
    )(sinks, qs, ks, vs, gs, pos_row, om, x, mod3, w_m, w_s, final_gain)


def kernel(x, c, positions, w_ada, b_ada, norm_gain, w_in, q_norm_gain, kv_norm_gain, w_uq, w_ukv, swa_sinks, w_out, final_gain):
    b, s, d = x.shape
    depth = w_ada.shape[0]
    assert depth == 1, "the final rmsnorm is fused into the (single) layer's output projection"
    pos = positions.astype(F32)
    pos_row = pos.reshape(b, s // WINDOW, 1, WINDOW)
    for l in range(depth):
        mod3 = _adaln(c, w_ada, b_ada[l], l).reshape(b, 3, d)
        w_in_p, w_uq_p, w_ukv_p = _pack_weights(w_in[l], w_uq[l], w_ukv[l])
        q, k, v, gm, qs, ks, vs, gs = _inproj(
            x, mod3, pos.reshape(b, 1, s), norm_gain[l].reshape(1, d), q_norm_gain[l].reshape(1, Q_LORA),
            kv_norm_gain[l].reshape(1, KV_LORA), w_in_p, w_uq_p, w_ukv_p, tm=ROWS_INPROJ)
        om = _mla(q, k, v, gm, tq=ROWS_MLA, pairs_per_step=MLA_PAIRS_PER_STEP)
        w_o = w_out[l].astype(BF16)
        x = _swa_out(swa_sinks[l].astype(F32), qs, ks, vs, gs, pos_row, om, x, mod3,
                     w_o[:MLA_WIDTH], w_o[MLA_WIDTH:], final_gain.reshape(1, d), tq=ROWS_SWA_OUT)
    return x
```

```python
import functools
import math

import jax
import jax.numpy as jnp
import numpy as np
from jax import lax
from jax.experimental import pallas as pl
from jax.experimental.pallas import tpu as pltpu

F32 = jnp.float32
BF16 = jnp.bfloat16

MLA_HEADS = 8
MLA_NOPE = 64
MLA_ROPE = 32
MLA_V = 64
Q_LORA = 384
KV_LORA = 256
MLA_WIDTH = MLA_HEADS * MLA_V
SWA_HEADS = 8
SWA_KV_HEADS = 2
SWA_HEAD_DIM = 64
SWA_GROUP = SWA_HEADS // SWA_KV_HEADS
SWA_WIDTH = SWA_HEADS * SWA_HEAD_DIM
SWA_KV_WIDTH = SWA_KV_HEADS * SWA_HEAD_DIM
WINDOW = 128
ROPE_THETA = 10000.0
EPS = 1e-6
ALIBI_MAX_EXP = 8.0

LANES = 128
HEAD_PAD = LANES
LOG2E = math.log2(math.e)
NEG_BIG = -1e30

C_ZQ = 0
C_KR = C_ZQ + Q_LORA
C_ZKV = C_KR + LANES
C_GM = C_ZKV + KV_LORA
C_QS = C_GM + MLA_WIDTH
C_KS = C_QS + SWA_WIDTH
C_GS = C_KS + 2 * SWA_KV_WIDTH
D_IN_PACKED = C_GS + SWA_WIDTH

V7X_VMEM_BYTES = 64 * 1024 * 1024
VMEM_LIMIT = V7X_VMEM_BYTES * 7 // 8

ROWS_INPROJ = 1024
ROWS_MLA = 512
ROWS_SWA_OUT = 1024
COLS_ADALN = 1024


def _swap_halves(w):
    half = w.shape[-1] // 2
    return jnp.concatenate([w[..., half:], w[..., :half]], axis=-1)


def _pack_weights(w_in, w_uq, w_ukv):
    d = w_in.shape[0]
    s = np.cumsum([0, Q_LORA, KV_LORA, MLA_ROPE, MLA_WIDTH, SWA_WIDTH, SWA_KV_WIDTH, SWA_KV_WIDTH, SWA_WIDTH])
    zq, zkv, kr, gm, qs, ks, vs, gs = [w_in[:, s[i]:s[i + 1]] for i in range(8)]
    kr_blk = jnp.concatenate([jnp.zeros((d, MLA_NOPE), w_in.dtype), kr, _swap_halves(kr)], axis=1)
    w_in_p = jnp.concatenate([zq, kr_blk, zkv, gm, qs, ks, vs, gs], axis=1).astype(BF16)

    uq = w_uq.reshape(Q_LORA, MLA_HEADS, MLA_NOPE + MLA_ROPE)
    uq_p = jnp.concatenate([uq, _swap_halves(uq[..., MLA_NOPE:])], axis=-1)
    w_uq_p = uq_p.reshape(Q_LORA, MLA_HEADS * HEAD_PAD).astype(BF16)

    return w_in_p, w_uq_p, w_ukv.astype(BF16)


def _adaln_kernel(c_ref, w_ref, b_ref, o_ref):
    c = c_ref[...]
    a = c * jax.nn.sigmoid(c)
    a_hi = a.astype(BF16)
    a_lo = (a - a_hi.astype(F32)).astype(BF16)
    w = w_ref[...]
    w_hi = w.astype(BF16)
    w_lo = (w - w_hi.astype(F32)).astype(BF16)
    acc = jnp.dot(a_hi, w_hi, preferred_element_type=F32)
    acc += jnp.dot(a_hi, w_lo, preferred_element_type=F32)
    acc += jnp.dot(a_lo, w_hi, preferred_element_type=F32)
    o_ref[...] = acc + b_ref[...]


def _adaln(c, w_ada, b_ada, layer):
    b, d = c.shape
    n = w_ada.shape[2]
    tn = COLS_ADALN
    return pl.pallas_call(
        _adaln_kernel,
        grid=(n // tn,),
        in_specs=[pl.BlockSpec((b, d), lambda j: (0, 0)),
                  pl.BlockSpec((None, d, tn), lambda j: (layer, 0, j)),
                  pl.BlockSpec((1, tn), lambda j: (0, j))],
        out_specs=pl.BlockSpec((b, tn), lambda j: (0, j)),
        out_shape=jax.ShapeDtypeStruct((b, n), F32),
        compiler_params=pltpu.CompilerParams(dimension_semantics=("arbitrary",), vmem_limit_bytes=VMEM_LIMIT),
        name="adaln",
    )(c, w_ada, b_ada.reshape(1, n))


def _rms(t, gain):
    return t * lax.rsqrt(jnp.mean(t * t, axis=-1, keepdims=True) + EPS) * gain


def _rope_tables(pos_row):
    t = pos_row.shape[1]
    half = MLA_ROPE // 2
    idx = lax.broadcasted_iota(jnp.int32, (half, 1), 0).astype(F32)
    inv = jnp.exp(idx * (-2.0 * math.log(ROPE_THETA) / MLA_ROPE))
    ang = inv * pos_row
    cos, sin = jnp.cos(ang), jnp.sin(ang)
    pad = jnp.zeros((HEAD_PAD - MLA_NOPE - MLA_ROPE, t), F32)
    a_t = jnp.concatenate([jnp.ones((MLA_NOPE, t), F32), cos, cos, pad], axis=0)
    b_t = jnp.concatenate([jnp.zeros((MLA_NOPE, t), F32), -sin, sin, pad], axis=0)
    return a_t.T, b_t.T


def _rope_group(t, a, b):
    return t * a + pltpu.roll(t, LANES - MLA_ROPE, 1) * b


def _dup_halves(t):
    lane = lax.broadcasted_iota(jnp.int32, (1, LANES), 1)
    r = pltpu.roll(t, LANES // 2, 1)
    lo = lane < LANES // 2
    return jnp.concatenate([jnp.where(lo, t, r), jnp.where(lo, r, t)], axis=1)


def _inproj_kernel(x_ref, mod_ref, pos_ref, ng_ref, qg_ref, kvg_ref, win_ref, wuq_ref, wukv_ref,
                   q_ref, k_ref, v_ref, gm_ref, qs_ref, ks_ref, vs_ref, gs_ref):
    x = x_ref[0]
    mod = mod_ref[0]
    shift, scale = mod[0:1], mod[1:2]
    h = _rms(x, ng_ref[...]) * (1.0 + scale) + shift
    hb = h.astype(BF16)

    z = jnp.dot(hb, win_ref[...], preferred_element_type=F32)

    def seg(lo, hi):
        return z[:, lo:hi]

    a, b = _rope_tables(pos_ref[0])
    q_scale = (MLA_NOPE + MLA_ROPE) ** -0.5 * LOG2E

    zqr = seg(C_ZQ, C_ZKV)
    qn = _rms(zqr[:, :Q_LORA], qg_ref[...] * q_scale).astype(BF16)
    q = jnp.dot(qn, wuq_ref[...], preferred_element_type=F32)
    for hd in range(MLA_HEADS):
        sl = slice(hd * HEAD_PAD, (hd + 1) * HEAD_PAD)
        q_ref[0, :, sl] = _rope_group(q[:, sl], a, b).astype(BF16)

    kpe = _rope_group(zqr[:, Q_LORA:], a, b)
    kvn = _rms(seg(C_ZKV, C_GM), kvg_ref[...]).astype(BF16)
    lane = lax.broadcasted_iota(jnp.int32, (1, LANES), 1)
    first = lane < MLA_NOPE
    for pr in range(MLA_HEADS // 2):
        cols = slice(2 * pr * HEAD_PAD, (2 * pr + 2) * HEAD_PAD)
        kv = jnp.dot(kvn, wukv_ref[:, cols], preferred_element_type=F32)
        even, odd = kv[:, :HEAD_PAD], kv[:, HEAD_PAD:]
        k_ref[0, :, cols] = jnp.concatenate([jnp.where(first, even, kpe), jnp.where(first, odd, kpe)],
                                            axis=1).astype(BF16)
        v_ref[0, :, pr * LANES:(pr + 1) * LANES] = jnp.where(first, pltpu.roll(even, MLA_V, 1), odd).astype(BF16)

    g = seg(C_GM, C_QS)
    gm_ref[0] = (g * jax.nn.sigmoid(g)).astype(BF16)
    g = seg(C_GS, D_IN_PACKED)
    gs_ref[0] = (g * jax.nn.sigmoid(g)).astype(BF16)
    qs_ref[0] = (seg(C_QS, C_KS) * (SWA_HEAD_DIM ** -0.5 * LOG2E)).astype(BF16)
    kvs = seg(C_KS, C_GS)
    ks_ref[0] = _dup_halves(kvs[:, :SWA_KV_WIDTH]).astype(BF16)
    for j in range(vs_ref.shape[1]):
        vs_ref[0, j] = kvs[j * WINDOW:(j + 1) * WINDOW, SWA_KV_WIDTH:].T.astype(BF16)


def _inproj(x, mod3, pos_row, norm_gain, q_gain, kv_gain, w_in_p, w_uq_p, w_ukv_p, tm):
    b, s, d = x.shape
    const = lambda shape: pl.BlockSpec(shape, lambda bi, i: (0,) * len(shape), pipeline_mode=pl.Buffered(1))
    tok = lambda w: pl.BlockSpec((1, tm, w), lambda bi, i: (bi, i, 0))
    widths = [MLA_HEADS * HEAD_PAD, MLA_HEADS * HEAD_PAD, MLA_WIDTH, MLA_WIDTH,
              SWA_WIDTH, 2 * SWA_KV_WIDTH, None, SWA_WIDTH]
    vt_spec = pl.BlockSpec((1, tm // WINDOW, SWA_KV_WIDTH, WINDOW), lambda bi, i: (bi, i, 0, 0))
    vt_shape = jax.ShapeDtypeStruct((b, s // WINDOW, SWA_KV_WIDTH, WINDOW), BF16)
    return pl.pallas_call(
        _inproj_kernel,
        grid=(b, s // tm),
        in_specs=[tok(d),
                  pl.BlockSpec((1, 3, d), lambda bi, i: (bi, 0, 0)),
                  pl.BlockSpec((1, 1, tm), lambda bi, i: (bi, 0, i)),
                  const((1, d)), const((1, Q_LORA)), const((1, KV_LORA)),
                  const(w_in_p.shape), const(w_uq_p.shape), const(w_ukv_p.shape)],
        out_specs=[vt_spec if w is None else tok(w) for w in widths],
        out_shape=[vt_shape if w is None else jax.ShapeDtypeStruct((b, s, w), BF16) for w in widths],
        compiler_params=pltpu.CompilerParams(dimension_semantics=("parallel", "parallel"),
                                             vmem_limit_bytes=VMEM_LIMIT),
        name="inproj",
    )(x, mod3, pos_row, norm_gain, q_gain, kv_gain, w_in_p, w_uq_p, w_ukv_p)


def _dot_nt(a, b):
    return lax.dot_general(a, b, (((1,), (1,)), ((), ())), preferred_element_type=F32)


def _mla_kernel(q_ref, k_ref, v_ref, g_ref, o_ref, vl_ref, *, tq):
    nq = q_ref.shape[1] // tq
    lane = lax.broadcasted_iota(jnp.int32, (1, LANES), 1)
    first = lane < MLA_V
    row = lax.broadcasted_iota(jnp.int32, (tq, tq), 0)
    col = lax.broadcasted_iota(jnp.int32, (tq, tq), 1)
    causal = col <= row

    v = v_ref[0]
    ones = jnp.ones_like(v)
    vl_ref[0] = jnp.where(first, v, ones)
    vl_ref[1] = jnp.where(first, ones, v)

    def scores(c, kb):
        q = q_ref[0, c * tq:(c + 1) * tq, :]
        k = k_ref[0, kb * tq:(kb + 1) * tq, :]
        return _dot_nt(q[:, :HEAD_PAD], k[:, :HEAD_PAD]), _dot_nt(q[:, HEAD_PAD:], k[:, HEAD_PAD:])

    def pv(p, head, kb):
        return jnp.dot(p.astype(BF16), vl_ref[head, kb * tq:(kb + 1) * tq, :], preferred_element_type=F32)

    tiles = [(c, kb) for c in range(nq) for kb in [c] + list(range(c))]
    nxt = scores(*tiles[0])
    for t, (c, kb) in enumerate(tiles):
        sa, sb = nxt
        nxt = scores(*tiles[t + 1]) if t + 1 < len(tiles) else None
        if kb == c:
            sa = jnp.where(causal, sa, NEG_BIG)
            sb = jnp.where(causal, sb, NEG_BIG)
            ma = jnp.max(sa, axis=-1, keepdims=True)
            mb = jnp.max(sb, axis=-1, keepdims=True)
            acc_a = pv(jnp.exp2(sa - ma), 0, kb)
            acc_b = pv(jnp.exp2(sb - mb), 1, kb)
        else:
            ma_n = jnp.maximum(ma, jnp.max(sa, axis=-1, keepdims=True))
            mb_n = jnp.maximum(mb, jnp.max(sb, axis=-1, keepdims=True))
            acc_a = acc_a * jnp.exp2(ma - ma_n) + pv(jnp.exp2(sa - ma_n), 0, kb)
            acc_b = acc_b * jnp.exp2(mb - mb_n) + pv(jnp.exp2(sb - mb_n), 1, kb)
            ma, mb = ma_n, mb_n
        if t + 1 == len(tiles) or tiles[t + 1][0] != c:
            rows = slice(c * tq, (c + 1) * tq)
            acc = jnp.where(first, acc_a, acc_b)
            den = jnp.where(first, pltpu.roll(acc_a, MLA_V, 1), pltpu.roll(acc_b, MLA_V, 1))
            o_ref[0, rows, :] = (acc / den * g_ref[0, rows, :].astype(F32)).astype(BF16)


def _mla(q, k, v, gm, tq):
    b, s, _ = q.shape
    pairs = MLA_HEADS // 2
    seq = lambda w: pl.BlockSpec((1, s, w), lambda bi, j: (bi, 0, j))
    return pl.pallas_call(
        functools.partial(_mla_kernel, tq=tq),
        grid=(b, pairs),
        in_specs=[seq(2 * HEAD_PAD), seq(2 * HEAD_PAD), seq(2 * MLA_V), seq(2 * MLA_V)],
        out_specs=seq(2 * MLA_V),
        out_shape=jax.ShapeDtypeStruct((b, s, MLA_WIDTH), BF16),
        scratch_shapes=[pltpu.VMEM((2, s, 2 * MLA_V), BF16)],
        compiler_params=pltpu.CompilerParams(dimension_semantics=("parallel", "parallel"),
                                             vmem_limit_bytes=VMEM_LIMIT),
        name="mla",
    )(q, k, v, gm)


def _swa_out_kernel(sink_ref, q_ref, k_ref, vt_ref, g_ref, pr_ref, om_ref, x_ref, mod_ref, wm_ref, ws_ref,
                    fg_ref, out_ref, o_ref, *, tq):
    i = pl.program_id(1)
    nblk = tq // WINDOW
    lane = lax.broadcasted_iota(jnp.int32, (1, LANES), 1)
    first = lane < SWA_HEAD_DIM
    row1 = lax.broadcasted_iota(jnp.int32, (WINDOW, WINDOW), 0)
    col1 = lax.broadcasted_iota(jnp.int32, (WINDOW, WINDOW), 1)
    in_cur1 = row1 <= col1
    in_cur = jnp.concatenate([in_cur1] * SWA_GROUP, axis=1)

    def positions(nl, has_prev):
        n = i * nblk + nl
        pq_row = pr_ref[0, n]
        if has_prev:
            pk_prev = pr_ref[0, n - 1]
            p0 = pk_prev[:, 0:1]
            pk = jnp.where(col1 <= row1, pq_row, pk_prev) - p0
        else:
            p0 = pq_row[:, 0:1]
            pk = jnp.broadcast_to(pq_row - p0, (WINDOW, WINDOW))
        return pk.T, pq_row - p0

    def unit(nl, kvh, has_prev, pk_t, pq):
        n = i * nblk + nl
        rows = slice(nl * WINDOW, (nl + 1) * WINDOW)
        ksl = slice(kvh * LANES, (kvh + 1) * LANES)
        dsl = slice(kvh * SWA_HEAD_DIM, (kvh + 1) * SWA_HEAD_DIM)
        heads = [kvh * SWA_GROUP + g for g in range(SWA_GROUP)]
        slopes = [LOG2E * 2.0 ** (-ALIBI_MAX_EXP * (h + 1) / SWA_HEADS) for h in heads]
        alibi = jnp.concatenate([sl * pk_t for sl in slopes], axis=1)
        sink = jnp.concatenate([sink_ref[h] * LOG2E + sl * pq for h, sl in zip(heads, slopes)], axis=1)
        pair_lanes = [slice((heads[0] // 2 + pr) * LANES, (heads[0] // 2 + pr + 1) * LANES)
                      for pr in range(SWA_GROUP // 2)]
        parts = []
        for psl in pair_lanes:
            q2 = q_ref[0, rows, psl]
            zq = jnp.zeros_like(q2)
            parts += [jnp.where(first, q2, zq), jnp.where(first, zq, q2)]
        q4 = jnp.concatenate(parts, axis=0)
        if has_prev:
            st = pl.multiple_of((n - 1) * WINDOW, WINDOW)
            s2 = _dot_nt(k_ref[0, pl.ds(st, 2 * WINDOW), ksl], q4)
        else:
            st = pl.multiple_of(n * WINDOW, WINDOW)
            s2 = _dot_nt(k_ref[0, pl.ds(st, WINDOW), ksl], q4)

        def finish():
            if has_prev:
                s = jnp.where(in_cur, s2[WINDOW:], s2[:WINDOW]) + alibi
            else:
                s = jnp.where(in_cur, s2 + alibi, NEG_BIG)
            m = jnp.maximum(jnp.max(s, axis=0, keepdims=True), sink)
            p = jnp.exp2(s - m)
            l = jnp.sum(p, axis=0, keepdims=True) + jnp.exp2(sink - m)
            v_cur = vt_ref[0, n, dsl, :]
            if has_prev:
                zp = jnp.zeros_like(p)
                p2 = jnp.concatenate([jnp.where(in_cur, zp, p), jnp.where(in_cur, p, zp)], axis=0).astype(BF16)
                v2 = jnp.concatenate([vt_ref[0, n - 1, dsl, :], v_cur], axis=1)
                o_t = jnp.dot(v2, p2, preferred_element_type=F32)
            else:
                o_t = jnp.dot(v_cur, p.astype(BF16), preferred_element_type=F32)
            o_t = o_t * (1.0 / l)
            for pr, psl in enumerate(pair_lanes):
                o2 = jnp.concatenate([o_t[:, 2 * pr * WINDOW:(2 * pr + 1) * WINDOW],
                                      o_t[:, (2 * pr + 1) * WINDOW:(2 * pr + 2) * WINDOW]], axis=0).T
                o_ref[rows, psl] = (o2 * g_ref[0, rows, psl].astype(F32)).astype(BF16)

        return finish

    def run(blocks, has_prev):
        pending = []
        for nl in blocks:
            pk_t, pq = positions(nl, has_prev)
            for kvh in range(SWA_KV_HEADS):
                pending.append(unit(nl, kvh, has_prev, pk_t, pq))
                if len(pending) > 3:
                    pending.pop(0)()
        for fin in pending:
            fin()

    pl.when(i == 0)(functools.partial(run, [0], False))
    pl.when(i > 0)(functools.partial(run, [0], True))
    run(range(1, nblk), True)

    y = jnp.dot(om_ref[0], wm_ref[...], preferred_element_type=F32)
    y += jnp.dot(o_ref[...], ws_ref[...], preferred_element_type=F32)
    gate = mod_ref[0][2:3]
    out_ref[0] = _rms(x_ref[0] + gate * y, fg_ref[...])


def _swa_out(sinks, qs, ks, vs, gs, pos_row, om, x, mod3, w_m, w_s, final_gain, tq):
    b, s, d = x.shape
    tok = lambda w: pl.BlockSpec((1, tq, w), lambda bi, i: (bi, i, 0))
    full = lambda w: pl.BlockSpec((1, s, w), lambda bi, i: (bi, 0, 0))
    const = lambda shape: pl.BlockSpec(shape, lambda bi, i: (0,) * len(shape))
    return pl.pallas_call(
        functools.partial(_swa_out_kernel, tq=tq),
        grid=(b, s // tq),
        in_specs=[pl.BlockSpec(memory_space=pltpu.SMEM),
                  tok(SWA_WIDTH), full(2 * SWA_KV_WIDTH),
                  pl.BlockSpec((1, s // WINDOW, SWA_KV_WIDTH, WINDOW), lambda bi, i: (bi, 0, 0, 0)), tok(SWA_WIDTH),
                  pl.BlockSpec((1, s // WINDOW, 1, WINDOW), lambda bi, i: (bi, 0, 0, 0)),
                  tok(MLA_WIDTH), tok(d), pl.BlockSpec((1, 3, d), lambda bi, i: (bi, 0, 0)),
                  const(w_m.shape), const(w_s.shape), const((1, d))],
        out_specs=tok(d),
        out_shape=jax.ShapeDtypeStruct((b, s, d), x.dtype),
        scratch_shapes=[pltpu.VMEM((tq, SWA_WIDTH), BF16)],
        compiler_params=pltpu.CompilerParams(dimension_semantics=("parallel", "arbitrary"),
                                             vmem_limit_bytes=VMEM_LIMIT),
        name="swa_out",
    )(sinks, qs, ks, vs, gs, pos_row, om, x, mod3, w_m, w_s, final_gain)


def kernel(x, c, positions, w_ada, b_ada, norm_gain, w_in, q_norm_gain, kv_norm_gain, w_uq, w_ukv, swa_sinks, w_out, final_gain):
    b, s, d = x.shape
    depth = w_ada.shape[0]
    assert depth == 1, "the final rmsnorm is fused into the (single) layer's output projection"
    pos = positions.astype(F32)
    pos_row = pos.reshape(b, s // WINDOW, 1, WINDOW)
    for l in range(depth):
        mod3 = _adaln(c, w_ada, b_ada[l], l).reshape(b, 3, d)
        w_in_p, w_uq_p, w_ukv_p = _pack_weights(w_in[l], w_uq[l], w_ukv[l])
        q, k, v, gm, qs, ks, vs, gs = _inproj(
            x, mod3, pos.reshape(b, 1, s), norm_gain[l].reshape(1, d), q_norm_gain[l].reshape(1, Q_LORA),
            kv_norm_gain[l].reshape(1, KV_LORA), w_in_p, w_uq_p, w_ukv_p, tm=ROWS_INPROJ)
        om = _mla(q, k, v, gm, tq=ROWS_MLA)
        w_o = w_out[l].astype(BF16)
        x = _swa_out(swa_sinks[l].astype(F32), qs, ks, vs, gs, pos_row, om, x, mod3,
                     w_o[:MLA_WIDTH], w_o[MLA_WIDTH:], final_gain.reshape(1, d), tq=ROWS_SWA_OUT)
    return x
```

```python
import functools
import math

import jax
import jax.numpy as jnp
import numpy as np
from jax import lax
from jax.experimental import pallas as pl
from jax.experimental.pallas import tpu as pltpu

F32 = jnp.float32
BF16 = jnp.bfloat16

MLA_HEADS = 8
MLA_NOPE = 64
MLA_ROPE = 32
MLA_V = 64
Q_LORA = 384
KV_LORA = 256
MLA_WIDTH = MLA_HEADS * MLA_V
SWA_HEADS = 8
SWA_KV_HEADS = 2
SWA_HEAD_DIM = 64
SWA_GROUP = SWA_HEADS // SWA_KV_HEADS
SWA_WIDTH = SWA_HEADS * SWA_HEAD_DIM
SWA_KV_WIDTH = SWA_KV_HEADS * SWA_HEAD_DIM
WINDOW = 128
ROPE_THETA = 10000.0
EPS = 1e-6
ALIBI_MAX_EXP = 8.0

LANES = 128
HEAD_PAD = LANES
LOG2E = math.log2(math.e)
NEG_BIG = -1e30

C_ZQ = 0
C_KR = C_ZQ + Q_LORA
C_ZKV = C_KR + LANES
C_GM = C_ZKV + KV_LORA
C_QS = C_GM + MLA_WIDTH
C_KS = C_QS + SWA_WIDTH
C_GS = C_KS + 2 * SWA_KV_WIDTH
D_IN_PACKED = C_GS + SWA_WIDTH

V7X_VMEM_BYTES = 64 * 1024 * 1024
VMEM_LIMIT = V7X_VMEM_BYTES * 7 // 8

ROWS_INPROJ = 1024
ROWS_MLA = 512
ROWS_SWA_OUT = 1024
OUT_ROWS = 256
SWA_PIPELINE_DEPTH = 3
COLS_ADALN = 1024


def _swap_halves(w):
    half = w.shape[-1] // 2
    return jnp.concatenate([w[..., half:], w[..., :half]], axis=-1)


def _pack_weights(w_in, w_uq, w_ukv):
    d = w_in.shape[0]
    s = np.cumsum([0, Q_LORA, KV_LORA, MLA_ROPE, MLA_WIDTH, SWA_WIDTH, SWA_KV_WIDTH, SWA_KV_WIDTH, SWA_WIDTH])
    zq, zkv, kr, gm, qs, ks, vs, gs = [w_in[:, s[i]:s[i + 1]] for i in range(8)]
    kr_blk = jnp.concatenate([jnp.zeros((d, MLA_NOPE), w_in.dtype), kr, _swap_halves(kr)], axis=1)
    w_in_p = jnp.concatenate([zq, kr_blk, zkv, gm, qs, ks, vs, gs], axis=1).astype(BF16)

    uq = w_uq.reshape(Q_LORA, MLA_HEADS, MLA_NOPE + MLA_ROPE)
    uq_p = jnp.concatenate([uq, _swap_halves(uq[..., MLA_NOPE:])], axis=-1)
    w_uq_p = uq_p.reshape(Q_LORA, MLA_HEADS * HEAD_PAD).astype(BF16)

    return w_in_p, w_uq_p, w_ukv.astype(BF16)


def _adaln_kernel(c_ref, w_ref, b_ref, o_ref):
    c = c_ref[...]
    a = c * jax.nn.sigmoid(c)
    a_hi = a.astype(BF16)
    a_lo = (a - a_hi.astype(F32)).astype(BF16)
    w = w_ref[...]
    w_hi = w.astype(BF16)
    w_lo = (w - w_hi.astype(F32)).astype(BF16)
    acc = jnp.dot(a_hi, w_hi, preferred_element_type=F32)
    acc += jnp.dot(a_hi, w_lo, preferred_element_type=F32)
    acc += jnp.dot(a_lo, w_hi, preferred_element_type=F32)
    o_ref[...] = acc + b_ref[...]


def _adaln(c, w_ada, b_ada, layer):
    b, d = c.shape
    n = w_ada.shape[2]
    tn = COLS_ADALN
    return pl.pallas_call(
        _adaln_kernel,
        grid=(n // tn,),
        in_specs=[pl.BlockSpec((b, d), lambda j: (0, 0)),
                  pl.BlockSpec((None, d, tn), lambda j: (layer, 0, j)),
                  pl.BlockSpec((1, tn), lambda j: (0, j))],
        out_specs=pl.BlockSpec((b, tn), lambda j: (0, j)),
        out_shape=jax.ShapeDtypeStruct((b, n), F32),
        compiler_params=pltpu.CompilerParams(dimension_semantics=("arbitrary",), vmem_limit_bytes=VMEM_LIMIT),
        name="adaln",
    )(c, w_ada, b_ada.reshape(1, n))


def _rms(t, gain):
    return t * lax.rsqrt(jnp.mean(t * t, axis=-1, keepdims=True) + EPS) * gain


def _rope_tables(pos_row):
    t = pos_row.shape[1]
    half = MLA_ROPE // 2
    idx = lax.broadcasted_iota(jnp.int32, (half, 1), 0).astype(F32)
    inv = jnp.exp(idx * (-2.0 * math.log(ROPE_THETA) / MLA_ROPE))
    ang = inv * pos_row
    cos, sin = jnp.cos(ang), jnp.sin(ang)
    pad = jnp.zeros((HEAD_PAD - MLA_NOPE - MLA_ROPE, t), F32)
    a_t = jnp.concatenate([jnp.ones((MLA_NOPE, t), F32), cos, cos, pad], axis=0)
    b_t = jnp.concatenate([jnp.zeros((MLA_NOPE, t), F32), -sin, sin, pad], axis=0)
    return a_t.T, b_t.T


def _rope_group(t, a, b):
    return t * a + pltpu.roll(t, LANES - MLA_ROPE, 1) * b


def _dup_halves(t):
    lane = lax.broadcasted_iota(jnp.int32, (1, LANES), 1)
    r = pltpu.roll(t, LANES // 2, 1)
    lo = lane < LANES // 2
    return jnp.concatenate([jnp.where(lo, t, r), jnp.where(lo, r, t)], axis=1)


def _inproj_kernel(x_ref, mod_ref, pos_ref, ng_ref, qg_ref, kvg_ref, win_ref, wuq_ref, wukv_ref,
                   q_ref, k_ref, v_ref, gm_ref, qs_ref, ks_ref, vs_ref, gs_ref):
    x = x_ref[0]
    mod = mod_ref[0]
    shift, scale = mod[0:1], mod[1:2]
    h = _rms(x, ng_ref[...]) * (1.0 + scale) + shift
    hb = h.astype(BF16)

    z = jnp.dot(hb, win_ref[...], preferred_element_type=F32)

    def seg(lo, hi):
        return z[:, lo:hi]

    a, b = _rope_tables(pos_ref[0])
    q_scale = (MLA_NOPE + MLA_ROPE) ** -0.5 * LOG2E

    zqr = seg(C_ZQ, C_ZKV)
    qn = _rms(zqr[:, :Q_LORA], qg_ref[...] * q_scale).astype(BF16)
    q = jnp.dot(qn, wuq_ref[...], preferred_element_type=F32)
    for hd in range(MLA_HEADS):
        sl = slice(hd * HEAD_PAD, (hd + 1) * HEAD_PAD)
        q_ref[0, :, sl] = _rope_group(q[:, sl], a, b).astype(BF16)

    kpe = _rope_group(zqr[:, Q_LORA:], a, b)
    kvn = _rms(seg(C_ZKV, C_GM), kvg_ref[...]).astype(BF16)
    lane = lax.broadcasted_iota(jnp.int32, (1, LANES), 1)
    first = lane < MLA_NOPE
    for pr in range(MLA_HEADS // 2):
        cols = slice(2 * pr * HEAD_PAD, (2 * pr + 2) * HEAD_PAD)
        kv = jnp.dot(kvn, wukv_ref[:, cols], preferred_element_type=F32)
        even, odd = kv[:, :HEAD_PAD], kv[:, HEAD_PAD:]
        k_ref[0, :, cols] = jnp.concatenate([jnp.where(first, even, kpe), jnp.where(first, odd, kpe)],
                                            axis=1).astype(BF16)
        v_ref[0, :, pr * LANES:(pr + 1) * LANES] = jnp.where(first, pltpu.roll(even, MLA_V, 1), odd).astype(BF16)

    g = seg(C_GM, C_QS)
    gm_ref[0] = (g * jax.nn.sigmoid(g)).astype(BF16)
    g = seg(C_GS, D_IN_PACKED)
    gs_ref[0] = (g * jax.nn.sigmoid(g)).astype(BF16)
    qs_ref[0] = (seg(C_QS, C_KS) * (SWA_HEAD_DIM ** -0.5 * LOG2E)).astype(BF16)
    kvs = seg(C_KS, C_GS)
    ks_ref[0] = _dup_halves(kvs[:, :SWA_KV_WIDTH]).astype(BF16)
    for j in range(vs_ref.shape[1]):
        vs_ref[0, j] = kvs[j * WINDOW:(j + 1) * WINDOW, SWA_KV_WIDTH:].T.astype(BF16)


def _inproj(x, mod3, pos_row, norm_gain, q_gain, kv_gain, w_in_p, w_uq_p, w_ukv_p, tm):
    b, s, d = x.shape
    const = lambda shape: pl.BlockSpec(shape, lambda bi, i: (0,) * len(shape), pipeline_mode=pl.Buffered(1))
    tok = lambda w: pl.BlockSpec((1, tm, w), lambda bi, i: (bi, i, 0))
    widths = [MLA_HEADS * HEAD_PAD, MLA_HEADS * HEAD_PAD, MLA_WIDTH, MLA_WIDTH,
              SWA_WIDTH, 2 * SWA_KV_WIDTH, None, SWA_WIDTH]
    vt_spec = pl.BlockSpec((1, tm // WINDOW, SWA_KV_WIDTH, WINDOW), lambda bi, i: (bi, i, 0, 0))
    vt_shape = jax.ShapeDtypeStruct((b, s // WINDOW, SWA_KV_WIDTH, WINDOW), BF16)
    return pl.pallas_call(
        _inproj_kernel,
        grid=(b, s // tm),
        in_specs=[tok(d),
                  pl.BlockSpec((1, 3, d), lambda bi, i: (bi, 0, 0)),
                  pl.BlockSpec((1, 1, tm), lambda bi, i: (bi, 0, i)),
                  const((1, d)), const((1, Q_LORA)), const((1, KV_LORA)),
                  const(w_in_p.shape), const(w_uq_p.shape), const(w_ukv_p.shape)],
        out_specs=[vt_spec if w is None else tok(w) for w in widths],
        out_shape=[vt_shape if w is None else jax.ShapeDtypeStruct((b, s, w), BF16) for w in widths],
        compiler_params=pltpu.CompilerParams(dimension_semantics=("parallel", "parallel"),
                                             vmem_limit_bytes=VMEM_LIMIT),
        name="inproj",
    )(x, mod3, pos_row, norm_gain, q_gain, kv_gain, w_in_p, w_uq_p, w_ukv_p)


def _dot_nt(a, b):
    return lax.dot_general(a, b, (((1,), (1,)), ((), ())), preferred_element_type=F32)


def _mla_kernel(q_ref, k_ref, v_ref, g_ref, o_ref, vl_ref, *, tq):
    nq = q_ref.shape[1] // tq
    lane = lax.broadcasted_iota(jnp.int32, (1, LANES), 1)
    first = lane < MLA_V
    row = lax.broadcasted_iota(jnp.int32, (tq, tq), 0)
    col = lax.broadcasted_iota(jnp.int32, (tq, tq), 1)
    causal = col <= row

    v = v_ref[0]
    ones = jnp.ones_like(v)
    vl_ref[0] = jnp.where(first, v, ones)
    vl_ref[1] = jnp.where(first, ones, v)

    def scores(c, kb):
        q = q_ref[0, c * tq:(c + 1) * tq, :]
        k = k_ref[0, kb * tq:(kb + 1) * tq, :]
        return _dot_nt(q[:, :HEAD_PAD], k[:, :HEAD_PAD]), _dot_nt(q[:, HEAD_PAD:], k[:, HEAD_PAD:])

    def pv(p, head, kb):
        return jnp.dot(p.astype(BF16), vl_ref[head, kb * tq:(kb + 1) * tq, :], preferred_element_type=F32)

    tiles = [(c, kb) for c in range(nq) for kb in [c] + list(range(c))]
    nxt = scores(*tiles[0])
    for t, (c, kb) in enumerate(tiles):
        sa, sb = nxt
        nxt = scores(*tiles[t + 1]) if t + 1 < len(tiles) else None
        if kb == c:
            sa = jnp.where(causal, sa, NEG_BIG)
            sb = jnp.where(causal, sb, NEG_BIG)
            ma = jnp.max(sa, axis=-1, keepdims=True)
            mb = jnp.max(sb, axis=-1, keepdims=True)
            acc_a = pv(jnp.exp2(sa - ma), 0, kb)
            acc_b = pv(jnp.exp2(sb - mb), 1, kb)
        else:
            ma_n = jnp.maximum(ma, jnp.max(sa, axis=-1, keepdims=True))
            mb_n = jnp.maximum(mb, jnp.max(sb, axis=-1, keepdims=True))
            acc_a = acc_a * jnp.exp2(ma - ma_n) + pv(jnp.exp2(sa - ma_n), 0, kb)
            acc_b = acc_b * jnp.exp2(mb - mb_n) + pv(jnp.exp2(sb - mb_n), 1, kb)
            ma, mb = ma_n, mb_n
        if t + 1 == len(tiles) or tiles[t + 1][0] != c:
            rows = slice(c * tq, (c + 1) * tq)
            acc = jnp.where(first, acc_a, acc_b)
            den = jnp.where(first, pltpu.roll(acc_a, MLA_V, 1), pltpu.roll(acc_b, MLA_V, 1))
            o_ref[0, rows, :] = (acc / den * g_ref[0, rows, :].astype(F32)).astype(BF16)


def _mla(q, k, v, gm, tq):
    b, s, _ = q.shape
    pairs = MLA_HEADS // 2
    seq = lambda w: pl.BlockSpec((1, s, w), lambda bi, j: (bi, 0, j))
    return pl.pallas_call(
        functools.partial(_mla_kernel, tq=tq),
        grid=(b, pairs),
        in_specs=[seq(2 * HEAD_PAD), seq(2 * HEAD_PAD), seq(2 * MLA_V), seq(2 * MLA_V)],
        out_specs=seq(2 * MLA_V),
        out_shape=jax.ShapeDtypeStruct((b, s, MLA_WIDTH), BF16),
        scratch_shapes=[pltpu.VMEM((2, s, 2 * MLA_V), BF16)],
        compiler_params=pltpu.CompilerParams(dimension_semantics=("parallel", "parallel"),
                                             vmem_limit_bytes=VMEM_LIMIT),
        name="mla",
    )(q, k, v, gm)


def _swa_out_kernel(sink_ref, q_ref, k_ref, vt_ref, g_ref, pr_ref, om_ref, x_ref, mod_ref, wm_ref, ws_ref,
                    fg_ref, out_ref, o_ref, *, tq):
    i = pl.program_id(1)
    nblk = tq // WINDOW
    lane = lax.broadcasted_iota(jnp.int32, (1, LANES), 1)
    first = lane < SWA_HEAD_DIM
    row1 = lax.broadcasted_iota(jnp.int32, (WINDOW, WINDOW), 0)
    col1 = lax.broadcasted_iota(jnp.int32, (WINDOW, WINDOW), 1)
    in_cur1 = row1 <= col1
    in_cur = jnp.concatenate([in_cur1] * SWA_GROUP, axis=1)

    def positions(nl):
        n = i * nblk + nl
        n_prev = jnp.maximum(n - 1, 0) if nl == 0 else n - 1
        pq_row = pr_ref[0, n]
        pk_prev = pr_ref[0, n_prev]
        p0 = pk_prev[:, 0:1]
        pk = jnp.where(col1 <= row1, pq_row, pk_prev) - p0
        return n_prev, pk.T, pq_row - p0

    def unit(nl, kvh, n_prev, pk_t, pq):
        n = i * nblk + nl
        rows = slice(nl * WINDOW, (nl + 1) * WINDOW)
        ksl = slice(kvh * LANES, (kvh + 1) * LANES)
        dsl = slice(kvh * SWA_HEAD_DIM, (kvh + 1) * SWA_HEAD_DIM)
        heads = [kvh * SWA_GROUP + g for g in range(SWA_GROUP)]
        slopes = [LOG2E * 2.0 ** (-ALIBI_MAX_EXP * (h + 1) / SWA_HEADS) for h in heads]
        alibi = jnp.concatenate([sl * pk_t for sl in slopes], axis=1)
        sink = jnp.concatenate([sink_ref[h] * LOG2E + sl * pq for h, sl in zip(heads, slopes)], axis=1)
        pair_lanes = [slice((heads[0] // 2 + pr) * LANES, (heads[0] // 2 + pr + 1) * LANES)
                      for pr in range(SWA_GROUP // 2)]
        parts = []
        for psl in pair_lanes:
            q2 = q_ref[0, rows, psl]
            zq = jnp.zeros_like(q2)
            parts += [jnp.where(first, q2, zq), jnp.where(first, zq, q2)]
        q4 = jnp.concatenate(parts, axis=0)
        k2 = jnp.concatenate([k_ref[0, pl.ds(pl.multiple_of(n_prev * WINDOW, WINDOW), WINDOW), ksl],
                              k_ref[0, pl.ds(pl.multiple_of(n * WINDOW, WINDOW), WINDOW), ksl]], axis=0)
        s2 = _dot_nt(k2, q4)

        def finish():
            s_prev = s2[:WINDOW]
            if nl == 0:
                s_prev = jnp.where(i > 0, s_prev, NEG_BIG)
            s = jnp.where(in_cur, s2[WINDOW:], s_prev) + alibi
            m = jnp.maximum(jnp.max(s, axis=0, keepdims=True), sink)
            p = jnp.exp2(s - m)
            l = jnp.sum(p, axis=0, keepdims=True) + jnp.exp2(sink - m)
            zp = jnp.zeros_like(p)
            p2 = jnp.concatenate([jnp.where(in_cur, zp, p), jnp.where(in_cur, p, zp)], axis=0).astype(BF16)
            v2 = jnp.concatenate([vt_ref[0, n_prev, dsl, :], vt_ref[0, n, dsl, :]], axis=1)
            o_t = jnp.dot(v2, p2, preferred_element_type=F32) * (1.0 / l)
            for pr, psl in enumerate(pair_lanes):
                o2 = jnp.concatenate([o_t[:, 2 * pr * WINDOW:(2 * pr + 1) * WINDOW],
                                      o_t[:, (2 * pr + 1) * WINDOW:(2 * pr + 2) * WINDOW]], axis=0).T
                o_ref[rows, psl] = (o2 * g_ref[0, rows, psl].astype(F32)).astype(BF16)

        return finish

    pending = []
    for nl in range(nblk):
        n_prev, pk_t, pq = positions(nl)
        for kvh in range(SWA_KV_HEADS):
            pending.append(unit(nl, kvh, n_prev, pk_t, pq))
            if len(pending) > SWA_PIPELINE_DEPTH:
                pending.pop(0)()
    for fin in pending:
        fin()

    gate = mod_ref[0][2:3]
    for r0 in range(0, tq, OUT_ROWS):
        rows = slice(r0, r0 + OUT_ROWS)
        y = jnp.dot(om_ref[0, rows, :], wm_ref[...], preferred_element_type=F32)
        y += jnp.dot(o_ref[rows, :], ws_ref[...], preferred_element_type=F32)
        out_ref[0, rows, :] = _rms(x_ref[0, rows, :] + gate * y, fg_ref[...])


def _swa_out(sinks, qs, ks, vs, gs, pos_row, om, x, mod3, w_m, w_s, final_gain, tq):
    b, s, d = x.shape
    tok = lambda w: pl.BlockSpec((1, tq, w), lambda bi, i: (bi, i, 0))
    full = lambda w: pl.BlockSpec((1, s, w), lambda bi, i: (bi, 0, 0))
    const = lambda shape: pl.BlockSpec(shape, lambda bi, i: (0,) * len(shape))
    return pl.pallas_call(
        functools.partial(_swa_out_kernel, tq=tq),
        grid=(b, s // tq),
        in_specs=[pl.BlockSpec(memory_space=pltpu.SMEM),
                  tok(SWA_WIDTH), full(2 * SWA_KV_WIDTH),
                  pl.BlockSpec((1, s // WINDOW, SWA_KV_WIDTH, WINDOW), lambda bi, i: (bi, 0, 0, 0)), tok(SWA_WIDTH),
                  pl.BlockSpec((1, s // WINDOW, 1, WINDOW), lambda bi, i: (bi, 0, 0, 0)),
                  tok(MLA_WIDTH), tok(d), pl.BlockSpec((1, 3, d), lambda bi, i: (bi, 0, 0)),
                  const(w_m.shape), const(w_s.shape), const((1, d))],
        out_specs=tok(d),
        out_shape=jax.ShapeDtypeStruct((b, s, d), x.dtype),
        scratch_shapes=[pltpu.VMEM((tq, SWA_WIDTH), BF16)],
        compiler_params=pltpu.CompilerParams(dimension_semantics=("parallel", "arbitrary"),
                                             vmem_limit_bytes=VMEM_LIMIT),
        name="swa_out",
    )(sinks, qs, ks, vs, gs, pos_row, om, x, mod3, w_m, w_s, final_gain)


def kernel(x, c, positions, w_ada, b_ada, norm_gain, w_in, q_norm_gain, kv_norm_gain, w_uq, w_ukv, swa_sinks, w_out, final_gain):
    b, s, d = x.shape
    depth = w_ada.shape[0]
    assert depth == 1, "the final rmsnorm is fused into the (single) layer's output projection"
    pos = positions.astype(F32)
    pos_row = pos.reshape(b, s // WINDOW, 1, WINDOW)
    for l in range(depth):
        mod3 = _adaln(c, w_ada, b_ada[l], l).reshape(b, 3, d)
        w_in_p, w_uq_p, w_ukv_p = _pack_weights(w_in[l], w_uq[l], w_ukv[l])
        q, k, v, gm, qs, ks, vs, gs = _inproj(
            x, mod3, pos.reshape(b, 1, s), norm_gain[l].reshape(1, d), q_norm_gain[l].reshape(1, Q_LORA),
            kv_norm_gain[l].reshape(1, KV_LORA), w_in_p, w_uq_p, w_ukv_p, tm=ROWS_INPROJ)
        om = _mla(q, k, v, gm, tq=ROWS_MLA)
        w_o = w_out[l].astype(BF16)
        x = _swa_out(swa_sinks[l].astype(F32), qs, ks, vs, gs, pos_row, om, x, mod3,
                     w_o[:MLA_WIDTH], w_o[MLA_WIDTH:], final_gain.reshape(1, d), tq=ROWS_SWA_OUT)
    return x
```

```python
import functools
import math

import jax
import jax.numpy as jnp
import numpy as np
from jax import lax
from jax.experimental import pallas as pl
from jax.experimental.pallas import tpu as pltpu

F32 = jnp.float32
BF16 = jnp.bfloat16

MLA_HEADS = 8
MLA_NOPE = 64
MLA_ROPE = 32
MLA_V = 64
Q_LORA = 384
KV_LORA = 256
MLA_WIDTH = MLA_HEADS * MLA_V
SWA_HEADS = 8
SWA_KV_HEADS = 2
SWA_HEAD_DIM = 64
SWA_GROUP = SWA_HEADS // SWA_KV_HEADS
SWA_WIDTH = SWA_HEADS * SWA_HEAD_DIM
SWA_KV_WIDTH = SWA_KV_HEADS * SWA_HEAD_DIM
WINDOW = 128
ROPE_THETA = 10000.0
EPS = 1e-6
ALIBI_MAX_EXP = 8.0

LANES = 128
HEAD_PAD = LANES
LOG2E = math.log2(math.e)
NEG_BIG = -1e30

C_ZQ = 0
C_KR = C_ZQ + Q_LORA
C_ZKV = C_KR + LANES
C_GM = C_ZKV + KV_LORA
C_QS = C_GM + MLA_WIDTH
C_KS = C_QS + SWA_WIDTH
C_GS = C_KS + 2 * SWA_KV_WIDTH
D_IN_PACKED = C_GS + SWA_WIDTH

V7X_VMEM_BYTES = 64 * 1024 * 1024
VMEM_LIMIT = V7X_VMEM_BYTES * 7 // 8

ROWS_INPROJ = 1024
ROWS_MLA = 512
ROWS_SWA_OUT = 1024
OUT_ROWS = 256
SWA_PIPELINE_DEPTH = 3
COLS_ADALN = 1024


def _swap_halves(w):
    half = w.shape[-1] // 2
    return jnp.concatenate([w[..., half:], w[..., :half]], axis=-1)


def _pack_weights(w_in, w_uq, w_ukv):
    d = w_in.shape[0]
    s = np.cumsum([0, Q_LORA, KV_LORA, MLA_ROPE, MLA_WIDTH, SWA_WIDTH, SWA_KV_WIDTH, SWA_KV_WIDTH, SWA_WIDTH])
    zq, zkv, kr, gm, qs, ks, vs, gs = [w_in[:, s[i]:s[i + 1]] for i in range(8)]
    kr_blk = jnp.concatenate([jnp.zeros((d, MLA_NOPE), w_in.dtype), kr, _swap_halves(kr)], axis=1)
    w_in_p = jnp.concatenate([zq, kr_blk, zkv, gm, qs, ks, vs, gs], axis=1).astype(BF16)

    uq = w_uq.reshape(Q_LORA, MLA_HEADS, MLA_NOPE + MLA_ROPE)
    uq_p = jnp.concatenate([uq, _swap_halves(uq[..., MLA_NOPE:])], axis=-1)
    w_uq_p = uq_p.reshape(Q_LORA, MLA_HEADS * HEAD_PAD).astype(BF16)

    return w_in_p, w_uq_p, w_ukv.astype(BF16)


def _adaln_kernel(c_ref, w_ref, b_ref, o_ref):
    c = c_ref[...]
    a = c * jax.nn.sigmoid(c)
    a_hi = a.astype(BF16)
    a_lo = (a - a_hi.astype(F32)).astype(BF16)
    w = w_ref[...]
    w_hi = w.astype(BF16)
    w_lo = (w - w_hi.astype(F32)).astype(BF16)
    acc = jnp.dot(a_hi, w_hi, preferred_element_type=F32)
    acc += jnp.dot(a_hi, w_lo, preferred_element_type=F32)
    acc += jnp.dot(a_lo, w_hi, preferred_element_type=F32)
    o_ref[...] = acc + b_ref[...]


def _adaln(c, w_ada, b_ada, layer):
    b, d = c.shape
    n = w_ada.shape[2]
    tn = COLS_ADALN
    return pl.pallas_call(
        _adaln_kernel,
        grid=(n // tn,),
        in_specs=[pl.BlockSpec((b, d), lambda j: (0, 0)),
                  pl.BlockSpec((None, d, tn), lambda j: (layer, 0, j)),
                  pl.BlockSpec((1, tn), lambda j: (0, j))],
        out_specs=pl.BlockSpec((b, tn), lambda j: (0, j)),
        out_shape=jax.ShapeDtypeStruct((b, n), F32),
        compiler_params=pltpu.CompilerParams(dimension_semantics=("arbitrary",), vmem_limit_bytes=VMEM_LIMIT),
        name="adaln",
    )(c, w_ada, b_ada.reshape(1, n))


def _rms(t, gain):
    return t * lax.rsqrt(jnp.mean(t * t, axis=-1, keepdims=True) + EPS) * gain


def _rope_tables(pos_row):
    t = pos_row.shape[1]
    half = MLA_ROPE // 2
    idx = lax.broadcasted_iota(jnp.int32, (half, 1), 0).astype(F32)
    inv = jnp.exp(idx * (-2.0 * math.log(ROPE_THETA) / MLA_ROPE))
    ang = inv * pos_row
    cos, sin = jnp.cos(ang), jnp.sin(ang)
    pad = jnp.zeros((HEAD_PAD - MLA_NOPE - MLA_ROPE, t), F32)
    a_t = jnp.concatenate([jnp.ones((MLA_NOPE, t), F32), cos, cos, pad], axis=0)
    b_t = jnp.concatenate([jnp.zeros((MLA_NOPE, t), F32), -sin, sin, pad], axis=0)
    return a_t.T, b_t.T


def _rope_group(t, a, b):
    return t * a + pltpu.roll(t, LANES - MLA_ROPE, 1) * b


def _dup_halves(t):
    lane = lax.broadcasted_iota(jnp.int32, (1, LANES), 1)
    r = pltpu.roll(t, LANES // 2, 1)
    lo = lane < LANES // 2
    return jnp.concatenate([jnp.where(lo, t, r), jnp.where(lo, r, t)], axis=1)


def _inproj_kernel(x_ref, mod_ref, pos_ref, ng_ref, qg_ref, kvg_ref, win_ref, wuq_ref, wukv_ref,
                   q_ref, k_ref, v_ref, gm_ref, qs_ref, ks_ref, vs_ref, gs_ref):
    x = x_ref[0]
    mod = mod_ref[0]
    shift, scale = mod[0:1], mod[1:2]
    h = _rms(x, ng_ref[...]) * (1.0 + scale) + shift
    hb = h.astype(BF16)

    z = jnp.dot(hb, win_ref[...], preferred_element_type=F32)

    def seg(lo, hi):
        return z[:, lo:hi]

    a, b = _rope_tables(pos_ref[0])
    q_scale = (MLA_NOPE + MLA_ROPE) ** -0.5 * LOG2E

    zqr = seg(C_ZQ, C_ZKV)
    qn = _rms(zqr[:, :Q_LORA], qg_ref[...] * q_scale).astype(BF16)
    q = jnp.dot(qn, wuq_ref[...], preferred_element_type=F32)
    for hd in range(MLA_HEADS):
        sl = slice(hd * HEAD_PAD, (hd + 1) * HEAD_PAD)
        q_ref[0, :, sl] = _rope_group(q[:, sl], a, b).astype(BF16)

    kpe = _rope_group(zqr[:, Q_LORA:], a, b)
    kvn = _rms(seg(C_ZKV, C_GM), kvg_ref[...]).astype(BF16)
    lane = lax.broadcasted_iota(jnp.int32, (1, LANES), 1)
    first = lane < MLA_NOPE
    for pr in range(MLA_HEADS // 2):
        cols = slice(2 * pr * HEAD_PAD, (2 * pr + 2) * HEAD_PAD)
        kv = jnp.dot(kvn, wukv_ref[:, cols], preferred_element_type=F32)
        even, odd = kv[:, :HEAD_PAD], kv[:, HEAD_PAD:]
        k_ref[0, :, cols] = jnp.concatenate([jnp.where(first, even, kpe), jnp.where(first, odd, kpe)],
                                            axis=1).astype(BF16)
        v_ref[0, :, pr * LANES:(pr + 1) * LANES] = jnp.where(first, pltpu.roll(even, MLA_V, 1), odd).astype(BF16)

    g = seg(C_GM, C_QS)
    gm_ref[0] = (g * jax.nn.sigmoid(g)).astype(BF16)
    g = seg(C_GS, D_IN_PACKED)
    gs_ref[0] = (g * jax.nn.sigmoid(g)).astype(BF16)
    qs_ref[0] = (seg(C_QS, C_KS) * (SWA_HEAD_DIM ** -0.5 * LOG2E)).astype(BF16)
    kvs = seg(C_KS, C_GS)
    ks_ref[0] = _dup_halves(kvs[:, :SWA_KV_WIDTH]).astype(BF16)
    for j in range(vs_ref.shape[1]):
        vs_ref[0, j] = kvs[j * WINDOW:(j + 1) * WINDOW, SWA_KV_WIDTH:].T.astype(BF16)


def _inproj(x, mod3, pos_row, norm_gain, q_gain, kv_gain, w_in_p, w_uq_p, w_ukv_p, tm):
    b, s, d = x.shape
    const = lambda shape: pl.BlockSpec(shape, lambda bi, i: (0,) * len(shape), pipeline_mode=pl.Buffered(1))
    tok = lambda w: pl.BlockSpec((1, tm, w), lambda bi, i: (bi, i, 0))
    widths = [MLA_HEADS * HEAD_PAD, MLA_HEADS * HEAD_PAD, MLA_WIDTH, MLA_WIDTH,
              SWA_WIDTH, 2 * SWA_KV_WIDTH, None, SWA_WIDTH]
    vt_spec = pl.BlockSpec((1, tm // WINDOW, SWA_KV_WIDTH, WINDOW), lambda bi, i: (bi, i, 0, 0))
    vt_shape = jax.ShapeDtypeStruct((b, s // WINDOW, SWA_KV_WIDTH, WINDOW), BF16)
    return pl.pallas_call(
        _inproj_kernel,
        grid=(b, s // tm),
        in_specs=[tok(d),
                  pl.BlockSpec((1, 3, d), lambda bi, i: (bi, 0, 0)),
                  pl.BlockSpec((1, 1, tm), lambda bi, i: (bi, 0, i)),
                  const((1, d)), const((1, Q_LORA)), const((1, KV_LORA)),
                  const(w_in_p.shape), const(w_uq_p.shape), const(w_ukv_p.shape)],
        out_specs=[vt_spec if w is None else tok(w) for w in widths],
        out_shape=[vt_shape if w is None else jax.ShapeDtypeStruct((b, s, w), BF16) for w in widths],
        compiler_params=pltpu.CompilerParams(dimension_semantics=("parallel", "parallel"),
                                             vmem_limit_bytes=VMEM_LIMIT),
        name="inproj",
    )(x, mod3, pos_row, norm_gain, q_gain, kv_gain, w_in_p, w_uq_p, w_ukv_p)


def _dot_nt(a, b):
    return lax.dot_general(a, b, (((1,), (1,)), ((), ())), preferred_element_type=F32)


def _mla_kernel(q_ref, k_ref, v_ref, g_ref, o_ref, vl_ref, *, tq):
    nq = q_ref.shape[1] // tq
    row = lax.broadcasted_iota(jnp.int32, (tq, tq), 0)
    col = lax.broadcasted_iota(jnp.int32, (tq, tq), 1)
    causal = row <= col
    top = lax.broadcasted_iota(jnp.int32, (LANES, 1), 0) < MLA_V

    v_t = v_ref[0].astype(F32).T
    vl_ref[0] = jnp.where(top, v_t, 1.0).astype(BF16)
    vl_ref[1] = jnp.where(top, 1.0, v_t).astype(BF16)

    def scores(c, kb):
        q = q_ref[0, c * tq:(c + 1) * tq, :]
        k = k_ref[0, kb * tq:(kb + 1) * tq, :]
        return _dot_nt(k[:, :HEAD_PAD], q[:, :HEAD_PAD]), _dot_nt(k[:, HEAD_PAD:], q[:, HEAD_PAD:])

    def pv(p, head, kb):
        return jnp.dot(vl_ref[head, :, kb * tq:(kb + 1) * tq], p.astype(BF16), preferred_element_type=F32)

    tiles = [(c, kb) for c in range(nq) for kb in [c] + list(range(c))]
    nxt = scores(*tiles[0])
    for t, (c, kb) in enumerate(tiles):
        sa, sb = nxt
        nxt = scores(*tiles[t + 1]) if t + 1 < len(tiles) else None
        if kb == c:
            sa = jnp.where(causal, sa, NEG_BIG)
            sb = jnp.where(causal, sb, NEG_BIG)
            ma = jnp.max(sa, axis=0, keepdims=True)
            mb = jnp.max(sb, axis=0, keepdims=True)
            acc_a = pv(jnp.exp2(sa - ma), 0, kb)
            acc_b = pv(jnp.exp2(sb - mb), 1, kb)
        else:
            ma_n = jnp.maximum(ma, jnp.max(sa, axis=0, keepdims=True))
            mb_n = jnp.maximum(mb, jnp.max(sb, axis=0, keepdims=True))
            acc_a = acc_a * jnp.exp2(ma - ma_n) + pv(jnp.exp2(sa - ma_n), 0, kb)
            acc_b = acc_b * jnp.exp2(mb - mb_n) + pv(jnp.exp2(sb - mb_n), 1, kb)
            ma, mb = ma_n, mb_n
        if t + 1 == len(tiles) or tiles[t + 1][0] != c:
            rows = slice(c * tq, (c + 1) * tq)
            num = jnp.concatenate([acc_a[:MLA_V], acc_b[MLA_V:]], axis=0)
            den = jnp.concatenate([acc_a[MLA_V:], acc_b[:MLA_V]], axis=0)
            o_ref[0, rows, :] = ((num / den).T * g_ref[0, rows, :].astype(F32)).astype(BF16)


def _mla(q, k, v, gm, tq):
    b, s, _ = q.shape
    pairs = MLA_HEADS // 2
    seq = lambda w: pl.BlockSpec((1, s, w), lambda bi, j: (bi, 0, j))
    return pl.pallas_call(
        functools.partial(_mla_kernel, tq=tq),
        grid=(b, pairs),
        in_specs=[seq(2 * HEAD_PAD), seq(2 * HEAD_PAD), seq(2 * MLA_V), seq(2 * MLA_V)],
        out_specs=seq(2 * MLA_V),
        out_shape=jax.ShapeDtypeStruct((b, s, MLA_WIDTH), BF16),
        scratch_shapes=[pltpu.VMEM((2, 2 * MLA_V, s), BF16)],
        compiler_params=pltpu.CompilerParams(dimension_semantics=("parallel", "parallel"),
                                             vmem_limit_bytes=VMEM_LIMIT),
        name="mla",
    )(q, k, v, gm)


def _swa_out_kernel(sink_ref, q_ref, k_ref, vt_ref, g_ref, pr_ref, om_ref, x_ref, mod_ref, wm_ref, ws_ref,
                    fg_ref, out_ref, o_ref, *, tq):
    i = pl.program_id(1)
    nblk = tq // WINDOW
    lane = lax.broadcasted_iota(jnp.int32, (1, LANES), 1)
    first = lane < SWA_HEAD_DIM
    row1 = lax.broadcasted_iota(jnp.int32, (WINDOW, WINDOW), 0)
    col1 = lax.broadcasted_iota(jnp.int32, (WINDOW, WINDOW), 1)
    in_cur1 = row1 <= col1
    in_cur = jnp.concatenate([in_cur1] * SWA_GROUP, axis=1)

    def positions(nl):
        n = i * nblk + nl
        n_prev = jnp.maximum(n - 1, 0) if nl == 0 else n - 1
        pq_row = pr_ref[0, n]
        pk_prev = pr_ref[0, n_prev]
        p0 = pk_prev[:, 0:1]
        pk = jnp.where(col1 <= row1, pq_row, pk_prev) - p0
        return n_prev, pk.T, pq_row - p0

    def unit(nl, kvh, n_prev, pk_t, pq):
        n = i * nblk + nl
        rows = slice(nl * WINDOW, (nl + 1) * WINDOW)
        ksl = slice(kvh * LANES, (kvh + 1) * LANES)
        dsl = slice(kvh * SWA_HEAD_DIM, (kvh + 1) * SWA_HEAD_DIM)
        heads = [kvh * SWA_GROUP + g for g in range(SWA_GROUP)]
        slopes = [LOG2E * 2.0 ** (-ALIBI_MAX_EXP * (h + 1) / SWA_HEADS) for h in heads]
        alibi = jnp.concatenate([sl * pk_t for sl in slopes], axis=1)
        sink = jnp.concatenate([sink_ref[h] * LOG2E + sl * pq for h, sl in zip(heads, slopes)], axis=1)
        pair_lanes = [slice((heads[0] // 2 + pr) * LANES, (heads[0] // 2 + pr + 1) * LANES)
                      for pr in range(SWA_GROUP // 2)]
        parts = []
        for psl in pair_lanes:
            q2 = q_ref[0, rows, psl]
            zq = jnp.zeros_like(q2)
            parts += [jnp.where(first, q2, zq), jnp.where(first, zq, q2)]
        q4 = jnp.concatenate(parts, axis=0)
        k2 = jnp.concatenate([k_ref[0, pl.ds(pl.multiple_of(n_prev * WINDOW, WINDOW), WINDOW), ksl],
                              k_ref[0, pl.ds(pl.multiple_of(n * WINDOW, WINDOW), WINDOW), ksl]], axis=0)
        s2 = _dot_nt(k2, q4)

        def finish():
            s_prev = s2[:WINDOW]
            if nl == 0:
                s_prev = jnp.where(i > 0, s_prev, NEG_BIG)
            s = jnp.where(in_cur, s2[WINDOW:], s_prev) + alibi
            m = jnp.maximum(jnp.max(s, axis=0, keepdims=True), sink)
            p = jnp.exp2(s - m)
            l = jnp.sum(p, axis=0, keepdims=True) + jnp.exp2(sink - m)
            zp = jnp.zeros_like(p)
            p2 = jnp.concatenate([jnp.where(in_cur, zp, p), jnp.where(in_cur, p, zp)], axis=0).astype(BF16)
            v2 = jnp.concatenate([vt_ref[0, n_prev, dsl, :], vt_ref[0, n, dsl, :]], axis=1)
            o_t = jnp.dot(v2, p2, preferred_element_type=F32) * (1.0 / l)
            for pr, psl in enumerate(pair_lanes):
                o2 = jnp.concatenate([o_t[:, 2 * pr * WINDOW:(2 * pr + 1) * WINDOW],
                                      o_t[:, (2 * pr + 1) * WINDOW:(2 * pr + 2) * WINDOW]], axis=0).T
                o_ref[rows, psl] = (o2 * g_ref[0, rows, psl].astype(F32)).astype(BF16)

        return finish

    pending = []
    for nl in range(nblk):
        n_prev, pk_t, pq = positions(nl)
        for kvh in range(SWA_KV_HEADS):
            pending.append(unit(nl, kvh, n_prev, pk_t, pq))
            if len(pending) > SWA_PIPELINE_DEPTH:
                pending.pop(0)()
    for fin in pending:
        fin()

    gate = mod_ref[0][2:3]
    for r0 in range(0, tq, OUT_ROWS):
        rows = slice(r0, r0 + OUT_ROWS)
        y = jnp.dot(om_ref[0, rows, :], wm_ref[...], preferred_element_type=F32)
        y += jnp.dot(o_ref[rows, :], ws_ref[...], preferred_element_type=F32)
        out_ref[0, rows, :] = _rms(x_ref[0, rows, :] + gate * y, fg_ref[...])


def _swa_out(sinks, qs, ks, vs, gs, pos_row, om, x, mod3, w_m, w_s, final_gain, tq):
    b, s, d = x.shape
    tok = lambda w: pl.BlockSpec((1, tq, w), lambda bi, i: (bi, i, 0))
    full = lambda w: pl.BlockSpec((1, s, w), lambda bi, i: (bi, 0, 0))
    const = lambda shape: pl.BlockSpec(shape, lambda bi, i: (0,) * len(shape))
    return pl.pallas_call(
        functools.partial(_swa_out_kernel, tq=tq),
        grid=(b, s // tq),
        in_specs=[pl.BlockSpec(memory_space=pltpu.SMEM),
                  tok(SWA_WIDTH), full(2 * SWA_KV_WIDTH),
                  pl.BlockSpec((1, s // WINDOW, SWA_KV_WIDTH, WINDOW), lambda bi, i: (bi, 0, 0, 0)), tok(SWA_WIDTH),
                  pl.BlockSpec((1, s // WINDOW, 1, WINDOW), lambda bi, i: (bi, 0, 0, 0)),
                  tok(MLA_WIDTH), tok(d), pl.BlockSpec((1, 3, d), lambda bi, i: (bi, 0, 0)),
                  const(w_m.shape), const(w_s.shape), const((1, d))],
        out_specs=tok(d),
        out_shape=jax.ShapeDtypeStruct((b, s, d), x.dtype),
        scratch_shapes=[pltpu.VMEM((tq, SWA_WIDTH), BF16)],
        compiler_params=pltpu.CompilerParams(dimension_semantics=("parallel", "arbitrary"),
                                             vmem_limit_bytes=VMEM_LIMIT),
        name="swa_out",
    )(sinks, qs, ks, vs, gs, pos_row, om, x, mod3, w_m, w_s, final_gain)


def kernel(x, c, positions, w_ada, b_ada, norm_gain, w_in, q_norm_gain, kv_norm_gain, w_uq, w_ukv, swa_sinks, w_out, final_gain):
    b, s, d = x.shape
    depth = w_ada.shape[0]
    assert depth == 1, "the final rmsnorm is fused into the (single) layer's output projection"
    pos = positions.astype(F32)
    pos_row = pos.reshape(b, s // WINDOW, 1, WINDOW)
    for l in range(depth):
        mod3 = _adaln(c, w_ada, b_ada[l], l).reshape(b, 3, d)
        w_in_p, w_uq_p, w_ukv_p = _pack_weights(w_in[l], w_uq[l], w_ukv[l])
        q, k, v, gm, qs, ks, vs, gs = _inproj(
            x, mod3, pos.reshape(b, 1, s), norm_gain[l].reshape(1, d), q_norm_gain[l].reshape(1, Q_LORA),
            kv_norm_gain[l].reshape(1, KV_LORA), w_in_p, w_uq_p, w_ukv_p, tm=ROWS_INPROJ)
        om = _mla(q, k, v, gm, tq=ROWS_MLA)
        w_o = w_out[l].astype(BF16)
        x = _swa_out(swa_sinks[l].astype(F32), qs, ks, vs, gs, pos_row, om, x, mod3,
                     w_o[:MLA_WIDTH], w_o[MLA_WIDTH:], final_gain.reshape(1, d), tq=ROWS_SWA_OUT)
    return x
```

```python
import functools
import math

import jax
import jax.numpy as jnp
import numpy as np
from jax import lax
from jax.experimental import pallas as pl
from jax.experimental.pallas import tpu as pltpu

F32 = jnp.float32
BF16 = jnp.bfloat16

MLA_HEADS = 8
MLA_NOPE = 64
MLA_ROPE = 32
MLA_V = 64
Q_LORA = 384
KV_LORA = 256
MLA_WIDTH = MLA_HEADS * MLA_V
SWA_HEADS = 8
SWA_KV_HEADS = 2
SWA_HEAD_DIM = 64
SWA_GROUP = SWA_HEADS // SWA_KV_HEADS
SWA_WIDTH = SWA_HEADS * SWA_HEAD_DIM
SWA_KV_WIDTH = SWA_KV_HEADS * SWA_HEAD_DIM
WINDOW = 128
ROPE_THETA = 10000.0
EPS = 1e-6
ALIBI_MAX_EXP = 8.0

LANES = 128
HEAD_PAD = LANES
LOG2E = math.log2(math.e)
NEG_BIG = -1e30

C_ZQ = 0
C_KR = C_ZQ + Q_LORA
C_ZKV = C_KR + LANES
C_GM = C_ZKV + KV_LORA
C_QS = C_GM + MLA_WIDTH
C_KS = C_QS + SWA_WIDTH
C_GS = C_KS + 2 * SWA_KV_WIDTH
D_IN_PACKED = C_GS + SWA_WIDTH

V7X_VMEM_BYTES = 64 * 1024 * 1024
VMEM_LIMIT = V7X_VMEM_BYTES * 7 // 8

ROWS_INPROJ = 1024
ROWS_MLA = 256
ROWS_SWA_OUT = 1024
OUT_ROWS = 256
SWA_PIPELINE_DEPTH = 3
COLS_ADALN = 1024


def _swap_halves(w):
    half = w.shape[-1] // 2
    return jnp.concatenate([w[..., half:], w[..., :half]], axis=-1)


def _pack_weights(w_in, w_uq, w_ukv):
    d = w_in.shape[0]
    s = np.cumsum([0, Q_LORA, KV_LORA, MLA_ROPE, MLA_WIDTH, SWA_WIDTH, SWA_KV_WIDTH, SWA_KV_WIDTH, SWA_WIDTH])
    zq, zkv, kr, gm, qs, ks, vs, gs = [w_in[:, s[i]:s[i + 1]] for i in range(8)]
    kr_blk = jnp.concatenate([jnp.zeros((d, MLA_NOPE), w_in.dtype), kr, _swap_halves(kr)], axis=1)
    w_in_p = jnp.concatenate([zq, kr_blk, zkv, gm, qs, ks, vs, gs], axis=1).astype(BF16)

    uq = w_uq.reshape(Q_LORA, MLA_HEADS, MLA_NOPE + MLA_ROPE)
    uq_p = jnp.concatenate([uq, _swap_halves(uq[..., MLA_NOPE:])], axis=-1)
    w_uq_p = uq_p.reshape(Q_LORA, MLA_HEADS * HEAD_PAD).astype(BF16)

    return w_in_p, w_uq_p, w_ukv.astype(BF16)


def _adaln_kernel(c_ref, w_ref, b_ref, o_ref):
    c = c_ref[...]
    a = c * jax.nn.sigmoid(c)
    a_hi = a.astype(BF16)
    a_lo = (a - a_hi.astype(F32)).astype(BF16)
    w = w_ref[...]
    w_hi = w.astype(BF16)
    w_lo = (w - w_hi.astype(F32)).astype(BF16)
    acc = jnp.dot(a_hi, w_hi, preferred_element_type=F32)
    acc += jnp.dot(a_hi, w_lo, preferred_element_type=F32)
    acc += jnp.dot(a_lo, w_hi, preferred_element_type=F32)
    o_ref[...] = acc + b_ref[...]


def _adaln(c, w_ada, b_ada, layer):
    b, d = c.shape
    n = w_ada.shape[2]
    tn = COLS_ADALN
    return pl.pallas_call(
        _adaln_kernel,
        grid=(n // tn,),
        in_specs=[pl.BlockSpec((b, d), lambda j: (0, 0)),
                  pl.BlockSpec((None, d, tn), lambda j: (layer, 0, j)),
                  pl.BlockSpec((1, tn), lambda j: (0, j))],
        out_specs=pl.BlockSpec((b, tn), lambda j: (0, j)),
        out_shape=jax.ShapeDtypeStruct((b, n), F32),
        compiler_params=pltpu.CompilerParams(dimension_semantics=("arbitrary",), vmem_limit_bytes=VMEM_LIMIT),
        name="adaln",
    )(c, w_ada, b_ada.reshape(1, n))


def _rms(t, gain):
    return t * lax.rsqrt(jnp.mean(t * t, axis=-1, keepdims=True) + EPS) * gain


def _rope_tables(pos_row):
    t = pos_row.shape[1]
    half = MLA_ROPE // 2
    idx = lax.broadcasted_iota(jnp.int32, (half, 1), 0).astype(F32)
    inv = jnp.exp(idx * (-2.0 * math.log(ROPE_THETA) / MLA_ROPE))
    ang = inv * pos_row
    cos, sin = jnp.cos(ang), jnp.sin(ang)
    pad = jnp.zeros((HEAD_PAD - MLA_NOPE - MLA_ROPE, t), F32)
    a_t = jnp.concatenate([jnp.ones((MLA_NOPE, t), F32), cos, cos, pad], axis=0)
    b_t = jnp.concatenate([jnp.zeros((MLA_NOPE, t), F32), -sin, sin, pad], axis=0)
    return a_t.T, b_t.T


def _rope_group(t, a, b):
    return t * a + pltpu.roll(t, LANES - MLA_ROPE, 1) * b


def _dup_halves(t):
    lane = lax.broadcasted_iota(jnp.int32, (1, LANES), 1)
    r = pltpu.roll(t, LANES // 2, 1)
    lo = lane < LANES // 2
    return jnp.concatenate([jnp.where(lo, t, r), jnp.where(lo, r, t)], axis=1)


def _inproj_kernel(x_ref, mod_ref, pos_ref, ng_ref, qg_ref, kvg_ref, win_ref, wuq_ref, wukv_ref,
                   q_ref, k_ref, v_ref, gm_ref, qs_ref, ks_ref, vs_ref, gs_ref):
    x = x_ref[0]
    mod = mod_ref[0]
    shift, scale = mod[0:1], mod[1:2]
    h = _rms(x, ng_ref[...]) * (1.0 + scale) + shift
    hb = h.astype(BF16)

    z = jnp.dot(hb, win_ref[...], preferred_element_type=F32)

    def seg(lo, hi):
        return z[:, lo:hi]

    a, b = _rope_tables(pos_ref[0])
    q_scale = (MLA_NOPE + MLA_ROPE) ** -0.5 * LOG2E

    zqr = seg(C_ZQ, C_ZKV)
    qn = _rms(zqr[:, :Q_LORA], qg_ref[...] * q_scale).astype(BF16)
    q = jnp.dot(qn, wuq_ref[...], preferred_element_type=F32)
    for hd in range(MLA_HEADS):
        sl = slice(hd * HEAD_PAD, (hd + 1) * HEAD_PAD)
        q_ref[0, :, sl] = _rope_group(q[:, sl], a, b).astype(BF16)

    kpe = _rope_group(zqr[:, Q_LORA:], a, b)
    kvn = _rms(seg(C_ZKV, C_GM), kvg_ref[...]).astype(BF16)
    lane = lax.broadcasted_iota(jnp.int32, (1, LANES), 1)
    first = lane < MLA_NOPE
    for pr in range(MLA_HEADS // 2):
        cols = slice(2 * pr * HEAD_PAD, (2 * pr + 2) * HEAD_PAD)
        kv = jnp.dot(kvn, wukv_ref[:, cols], preferred_element_type=F32)
        even, odd = kv[:, :HEAD_PAD], kv[:, HEAD_PAD:]
        k_ref[0, :, cols] = jnp.concatenate([jnp.where(first, even, kpe), jnp.where(first, odd, kpe)],
                                            axis=1).astype(BF16)
        v_ref[0, :, pr * LANES:(pr + 1) * LANES] = jnp.where(first, pltpu.roll(even, MLA_V, 1), odd).astype(BF16)

    g = seg(C_GM, C_QS)
    gm_ref[0] = (g * jax.nn.sigmoid(g)).astype(BF16)
    g = seg(C_GS, D_IN_PACKED)
    gs_ref[0] = (g * jax.nn.sigmoid(g)).astype(BF16)
    qs_ref[0] = (seg(C_QS, C_KS) * (SWA_HEAD_DIM ** -0.5 * LOG2E)).astype(BF16)
    kvs = seg(C_KS, C_GS)
    ks_ref[0] = _dup_halves(kvs[:, :SWA_KV_WIDTH]).astype(BF16)
    for j in range(vs_ref.shape[1]):
        vs_ref[0, j] = kvs[j * WINDOW:(j + 1) * WINDOW, SWA_KV_WIDTH:].T.astype(BF16)


def _inproj(x, mod3, pos_row, norm_gain, q_gain, kv_gain, w_in_p, w_uq_p, w_ukv_p, tm):
    b, s, d = x.shape
    const = lambda shape: pl.BlockSpec(shape, lambda bi, i: (0,) * len(shape), pipeline_mode=pl.Buffered(1))
    tok = lambda w: pl.BlockSpec((1, tm, w), lambda bi, i: (bi, i, 0))
    widths = [MLA_HEADS * HEAD_PAD, MLA_HEADS * HEAD_PAD, MLA_WIDTH, MLA_WIDTH,
              SWA_WIDTH, 2 * SWA_KV_WIDTH, None, SWA_WIDTH]
    vt_spec = pl.BlockSpec((1, tm // WINDOW, SWA_KV_WIDTH, WINDOW), lambda bi, i: (bi, i, 0, 0))
    vt_shape = jax.ShapeDtypeStruct((b, s // WINDOW, SWA_KV_WIDTH, WINDOW), BF16)
    return pl.pallas_call(
        _inproj_kernel,
        grid=(b, s // tm),
        in_specs=[tok(d),
                  pl.BlockSpec((1, 3, d), lambda bi, i: (bi, 0, 0)),
                  pl.BlockSpec((1, 1, tm), lambda bi, i: (bi, 0, i)),
                  const((1, d)), const((1, Q_LORA)), const((1, KV_LORA)),
                  const(w_in_p.shape), const(w_uq_p.shape), const(w_ukv_p.shape)],
        out_specs=[vt_spec if w is None else tok(w) for w in widths],
        out_shape=[vt_shape if w is None else jax.ShapeDtypeStruct((b, s, w), BF16) for w in widths],
        compiler_params=pltpu.CompilerParams(dimension_semantics=("parallel", "parallel"),
                                             vmem_limit_bytes=VMEM_LIMIT),
        name="inproj",
    )(x, mod3, pos_row, norm_gain, q_gain, kv_gain, w_in_p, w_uq_p, w_ukv_p)


def _dot_nt(a, b):
    return lax.dot_general(a, b, (((1,), (1,)), ((), ())), preferred_element_type=F32)


def _mla_kernel(q_ref, k_ref, v_ref, g_ref, o_ref, vl_ref, *, tq):
    nq = q_ref.shape[1] // tq
    row = lax.broadcasted_iota(jnp.int32, (tq, tq), 0)
    col = lax.broadcasted_iota(jnp.int32, (tq, tq), 1)
    causal = row <= col
    top = lax.broadcasted_iota(jnp.int32, (LANES, 1), 0) < MLA_V

    v_t = v_ref[0].astype(F32).T
    vl_ref[0] = jnp.where(top, v_t, 1.0).astype(BF16)
    vl_ref[1] = jnp.where(top, 1.0, v_t).astype(BF16)

    def scores(c, kb):
        q = q_ref[0, c * tq:(c + 1) * tq, :]
        k = k_ref[0, kb * tq:(kb + 1) * tq, :]
        return _dot_nt(k[:, :HEAD_PAD], q[:, :HEAD_PAD]), _dot_nt(k[:, HEAD_PAD:], q[:, HEAD_PAD:])

    def pv(p, head, kb):
        return jnp.dot(vl_ref[head, :, kb * tq:(kb + 1) * tq], p.astype(BF16), preferred_element_type=F32)

    tiles = [(c, kb) for c in range(nq) for kb in [c] + list(range(c))]
    ahead = [scores(*tiles[0]), scores(*tiles[1])]
    for t, (c, kb) in enumerate(tiles):
        sa, sb = ahead.pop(0)
        if t + 2 < len(tiles):
            ahead.append(scores(*tiles[t + 2]))
        if kb == c:
            sa = jnp.where(causal, sa, NEG_BIG)
            sb = jnp.where(causal, sb, NEG_BIG)
            ma = jnp.max(sa, axis=0, keepdims=True)
            mb = jnp.max(sb, axis=0, keepdims=True)
            acc_a = pv(jnp.exp2(sa - ma), 0, kb)
            acc_b = pv(jnp.exp2(sb - mb), 1, kb)
        else:
            ma_n = jnp.maximum(ma, jnp.max(sa, axis=0, keepdims=True))
            mb_n = jnp.maximum(mb, jnp.max(sb, axis=0, keepdims=True))
            acc_a = acc_a * jnp.exp2(ma - ma_n) + pv(jnp.exp2(sa - ma_n), 0, kb)
            acc_b = acc_b * jnp.exp2(mb - mb_n) + pv(jnp.exp2(sb - mb_n), 1, kb)
            ma, mb = ma_n, mb_n
        if t + 1 == len(tiles) or tiles[t + 1][0] != c:
            rows = slice(c * tq, (c + 1) * tq)
            num = jnp.concatenate([acc_a[:MLA_V], acc_b[MLA_V:]], axis=0)
            den = jnp.concatenate([acc_a[MLA_V:], acc_b[:MLA_V]], axis=0)
            o_ref[0, rows, :] = ((num / den).T * g_ref[0, rows, :].astype(F32)).astype(BF16)


def _mla(q, k, v, gm, tq):
    b, s, _ = q.shape
    pairs = MLA_HEADS // 2
    seq = lambda w: pl.BlockSpec((1, s, w), lambda bi, j: (bi, 0, j))
    return pl.pallas_call(
        functools.partial(_mla_kernel, tq=tq),
        grid=(b, pairs),
        in_specs=[seq(2 * HEAD_PAD), seq(2 * HEAD_PAD), seq(2 * MLA_V), seq(2 * MLA_V)],
        out_specs=seq(2 * MLA_V),
        out_shape=jax.ShapeDtypeStruct((b, s, MLA_WIDTH), BF16),
        scratch_shapes=[pltpu.VMEM((2, 2 * MLA_V, s), BF16)],
        compiler_params=pltpu.CompilerParams(dimension_semantics=("parallel", "parallel"),
                                             vmem_limit_bytes=VMEM_LIMIT),
        name="mla",
    )(q, k, v, gm)


def _swa_out_kernel(sink_ref, q_ref, k_ref, vt_ref, g_ref, pr_ref, om_ref, x_ref, mod_ref, wm_ref, ws_ref,
                    fg_ref, out_ref, o_ref, *, tq):
    i = pl.program_id(1)
    nblk = tq // WINDOW
    lane = lax.broadcasted_iota(jnp.int32, (1, LANES), 1)
    first = lane < SWA_HEAD_DIM
    row1 = lax.broadcasted_iota(jnp.int32, (WINDOW, WINDOW), 0)
    col1 = lax.broadcasted_iota(jnp.int32, (WINDOW, WINDOW), 1)
    in_cur1 = row1 <= col1
    in_cur = jnp.concatenate([in_cur1] * SWA_GROUP, axis=1)

    def positions(nl):
        n = i * nblk + nl
        n_prev = jnp.maximum(n - 1, 0) if nl == 0 else n - 1
        pq_row = pr_ref[0, n]
        pk_prev = pr_ref[0, n_prev]
        p0 = pk_prev[:, 0:1]
        pk = jnp.where(col1 <= row1, pq_row, pk_prev) - p0
        return n_prev, pk.T, pq_row - p0

    def unit(nl, kvh, n_prev, pk_t, pq):
        n = i * nblk + nl
        rows = slice(nl * WINDOW, (nl + 1) * WINDOW)
        ksl = slice(kvh * LANES, (kvh + 1) * LANES)
        dsl = slice(kvh * SWA_HEAD_DIM, (kvh + 1) * SWA_HEAD_DIM)
        heads = [kvh * SWA_GROUP + g for g in range(SWA_GROUP)]
        slopes = [LOG2E * 2.0 ** (-ALIBI_MAX_EXP * (h + 1) / SWA_HEADS) for h in heads]
        alibi = jnp.concatenate([sl * pk_t for sl in slopes], axis=1)
        sink = jnp.concatenate([sink_ref[h] * LOG2E + sl * pq for h, sl in zip(heads, slopes)], axis=1)
        pair_lanes = [slice((heads[0] // 2 + pr) * LANES, (heads[0] // 2 + pr + 1) * LANES)
                      for pr in range(SWA_GROUP // 2)]
        parts = []
        for psl in pair_lanes:
            q2 = q_ref[0, rows, psl]
            zq = jnp.zeros_like(q2)
            parts += [jnp.where(first, q2, zq), jnp.where(first, zq, q2)]
        q4 = jnp.concatenate(parts, axis=0)
        k2 = jnp.concatenate([k_ref[0, pl.ds(pl.multiple_of(n_prev * WINDOW, WINDOW), WINDOW), ksl],
                              k_ref[0, pl.ds(pl.multiple_of(n * WINDOW, WINDOW), WINDOW), ksl]], axis=0)
        s2 = _dot_nt(k2, q4)

        def finish():
            s_prev = s2[:WINDOW]
            if nl == 0:
                s_prev = jnp.where(i > 0, s_prev, NEG_BIG)
            s = jnp.where(in_cur, s2[WINDOW:], s_prev) + alibi
            m = jnp.maximum(jnp.max(s, axis=0, keepdims=True), sink)
            p = jnp.exp2(s - m)
            l = jnp.sum(p, axis=0, keepdims=True) + jnp.exp2(sink - m)
            zp = jnp.zeros_like(p)
            p2 = jnp.concatenate([jnp.where(in_cur, zp, p), jnp.where(in_cur, p, zp)], axis=0).astype(BF16)
            v2 = jnp.concatenate([vt_ref[0, n_prev, dsl, :], vt_ref[0, n, dsl, :]], axis=1)
            o_t = jnp.dot(v2, p2, preferred_element_type=F32) * (1.0 / l)
            for pr, psl in enumerate(pair_lanes):
                o2 = jnp.concatenate([o_t[:, 2 * pr * WINDOW:(2 * pr + 1) * WINDOW],
                                      o_t[:, (2 * pr + 1) * WINDOW:(2 * pr + 2) * WINDOW]], axis=0).T
                o_ref[rows, psl] = (o2 * g_ref[0, rows, psl].astype(F32)).astype(BF16)

        return finish

    pending = []
    for nl in range(nblk):
        n_prev, pk_t, pq = positions(nl)
        for kvh in range(SWA_KV_HEADS):
            pending.append(unit(nl, kvh, n_prev, pk_t, pq))
            if len(pending) > SWA_PIPELINE_DEPTH:
                pending.pop(0)()
    for fin in pending:
        fin()

    gate = mod_ref[0][2:3]
    for r0 in range(0, tq, OUT_ROWS):
        rows = slice(r0, r0 + OUT_ROWS)
        y = jnp.dot(om_ref[0, rows, :], wm_ref[...], preferred_element_type=F32)
        y += jnp.dot(o_ref[rows, :], ws_ref[...], preferred_element_type=F32)
        out_ref[0, rows, :] = _rms(x_ref[0, rows, :] + gate * y, fg_ref[...])


def _swa_out(sinks, qs, ks, vs, gs, pos_row, om, x, mod3, w_m, w_s, final_gain, tq):
    b, s, d = x.shape
    tok = lambda w: pl.BlockSpec((1, tq, w), lambda bi, i: (bi, i, 0))
    full = lambda w: pl.BlockSpec((1, s, w), lambda bi, i: (bi, 0, 0))
    const = lambda shape: pl.BlockSpec(shape, lambda bi, i: (0,) * len(shape))
    return pl.pallas_call(
        functools.partial(_swa_out_kernel, tq=tq),
        grid=(b, s // tq),
        in_specs=[pl.BlockSpec(memory_space=pltpu.SMEM),
                  tok(SWA_WIDTH), full(2 * SWA_KV_WIDTH),
                  pl.BlockSpec((1, s // WINDOW, SWA_KV_WIDTH, WINDOW), lambda bi, i: (bi, 0, 0, 0)), tok(SWA_WIDTH),
                  pl.BlockSpec((1, s // WINDOW, 1, WINDOW), lambda bi, i: (bi, 0, 0, 0)),
                  tok(MLA_WIDTH), tok(d), pl.BlockSpec((1, 3, d), lambda bi, i: (bi, 0, 0)),
                  const(w_m.shape), const(w_s.shape), const((1, d))],
        out_specs=tok(d),
        out_shape=jax.ShapeDtypeStruct((b, s, d), x.dtype),
        scratch_shapes=[pltpu.VMEM((tq, SWA_WIDTH), BF16)],
        compiler_params=pltpu.CompilerParams(dimension_semantics=("parallel", "arbitrary"),
                                             vmem_limit_bytes=VMEM_LIMIT),
        name="swa_out",
    )(sinks, qs, ks, vs, gs, pos_row, om, x, mod3, w_m, w_s, final_gain)


def kernel(x, c, positions, w_ada, b_ada, norm_gain, w_in, q_norm_gain, kv_norm_gain, w_uq, w_ukv, swa_sinks, w_out, final_gain):
    b, s, d = x.shape
    depth = w_ada.shape[0]
    assert depth == 1, "the final rmsnorm is fused into the (single) layer's output projection"
    pos = positions.astype(F32)
    pos_row = pos.reshape(b, s // WINDOW, 1, WINDOW)
    for l in range(depth):
        mod3 = _adaln(c, w_ada, b_ada[l], l).reshape(b, 3, d)
        w_in_p, w_uq_p, w_ukv_p = _pack_weights(w_in[l], w_uq[l], w_ukv[l])
        q, k, v, gm, qs, ks, vs, gs = _inproj(
            x, mod3, pos.reshape(b, 1, s), norm_gain[l].reshape(1, d), q_norm_gain[l].reshape(1, Q_LORA),
            kv_norm_gain[l].reshape(1, KV_LORA), w_in_p, w_uq_p, w_ukv_p, tm=ROWS_INPROJ)
        om = _mla(q, k, v, gm, tq=ROWS_MLA)
        w_o = w_out[l].astype(BF16)
        x = _swa_out(swa_sinks[l].astype(F32), qs, ks, vs, gs, pos_row, om, x, mod3,
                     w_o[:MLA_WIDTH], w_o[MLA_WIDTH:], final_gain.reshape(1, d), tq=ROWS_SWA_OUT)
    return x
```

```python
import functools
import math

import jax
import jax.numpy as jnp
import numpy as np
from jax import lax
from jax.experimental import pallas as pl
from jax.experimental.pallas import tpu as pltpu

F32 = jnp.float32
BF16 = jnp.bfloat16

MLA_HEADS = 8
MLA_NOPE = 64
MLA_ROPE = 32
MLA_V = 64
Q_LORA = 384
KV_LORA = 256
MLA_WIDTH = MLA_HEADS * MLA_V
SWA_HEADS = 8
SWA_KV_HEADS = 2
SWA_HEAD_DIM = 64
SWA_GROUP = SWA_HEADS // SWA_KV_HEADS
SWA_WIDTH = SWA_HEADS * SWA_HEAD_DIM
SWA_KV_WIDTH = SWA_KV_HEADS * SWA_HEAD_DIM
WINDOW = 128
ROPE_THETA = 10000.0
EPS = 1e-6
ALIBI_MAX_EXP = 8.0

LANES = 128
HEAD_PAD = LANES
LOG2E = math.log2(math.e)
NEG_BIG = -1e30

C_ZQ = 0
C_KR = C_ZQ + Q_LORA
C_ZKV = C_KR + LANES
C_GM = C_ZKV + KV_LORA
C_QS = C_GM + MLA_WIDTH
C_KS = C_QS + SWA_WIDTH
C_GS = C_KS + 2 * SWA_KV_WIDTH
D_IN_PACKED = C_GS + SWA_WIDTH

V7X_VMEM_BYTES = 64 * 1024 * 1024
VMEM_LIMIT = V7X_VMEM_BYTES * 7 // 8

ROWS_INPROJ = 1024
ROWS_MLA = 256
ROWS_SWA_OUT = 2048
OUT_ROWS = 256
SWA_PIPELINE_DEPTH = 3
COLS_ADALN = 1024


def _swap_halves(w):
    half = w.shape[-1] // 2
    return jnp.concatenate([w[..., half:], w[..., :half]], axis=-1)


def _pack_weights(w_in, w_uq, w_ukv):
    d = w_in.shape[0]
    s = np.cumsum([0, Q_LORA, KV_LORA, MLA_ROPE, MLA_WIDTH, SWA_WIDTH, SWA_KV_WIDTH, SWA_KV_WIDTH, SWA_WIDTH])
    zq, zkv, kr, gm, qs, ks, vs, gs = [w_in[:, s[i]:s[i + 1]] for i in range(8)]
    kr_blk = jnp.concatenate([jnp.zeros((d, MLA_NOPE), w_in.dtype), kr, _swap_halves(kr)], axis=1)
    w_in_p = jnp.concatenate([zq, kr_blk, zkv, gm, qs, ks, vs, gs], axis=1).astype(BF16)

    uq = w_uq.reshape(Q_LORA, MLA_HEADS, MLA_NOPE + MLA_ROPE)
    uq_p = jnp.concatenate([uq, _swap_halves(uq[..., MLA_NOPE:])], axis=-1)
    w_uq_p = uq_p.reshape(Q_LORA, MLA_HEADS * HEAD_PAD).astype(BF16)

    return w_in_p, w_uq_p, w_ukv.astype(BF16)


def _adaln_kernel(c_ref, w_ref, b_ref, o_ref):
    c = c_ref[...]
    a = c * jax.nn.sigmoid(c)
    a_hi = a.astype(BF16)
    a_lo = (a - a_hi.astype(F32)).astype(BF16)
    w = w_ref[...]
    w_hi = w.astype(BF16)
    w_lo = (w - w_hi.astype(F32)).astype(BF16)
    acc = jnp.dot(a_hi, w_hi, preferred_element_type=F32)
    acc += jnp.dot(a_hi, w_lo, preferred_element_type=F32)
    acc += jnp.dot(a_lo, w_hi, preferred_element_type=F32)
    o_ref[...] = acc + b_ref[...]


def _adaln(c, w_ada, b_ada, layer):
    b, d = c.shape
    n = w_ada.shape[2]
    tn = COLS_ADALN
    return pl.pallas_call(
        _adaln_kernel,
        grid=(n // tn,),
        in_specs=[pl.BlockSpec((b, d), lambda j: (0, 0)),
                  pl.BlockSpec((None, d, tn), lambda j: (layer, 0, j)),
                  pl.BlockSpec((1, tn), lambda j: (0, j))],
        out_specs=pl.BlockSpec((b, tn), lambda j: (0, j)),
        out_shape=jax.ShapeDtypeStruct((b, n), F32),
        compiler_params=pltpu.CompilerParams(dimension_semantics=("arbitrary",), vmem_limit_bytes=VMEM_LIMIT),
        name="adaln",
    )(c, w_ada, b_ada.reshape(1, n))


def _rms(t, gain):
    return t * lax.rsqrt(jnp.mean(t * t, axis=-1, keepdims=True) + EPS) * gain


def _rope_tables(pos_row):
    t = pos_row.shape[1]
    half = MLA_ROPE // 2
    idx = lax.broadcasted_iota(jnp.int32, (half, 1), 0).astype(F32)
    inv = jnp.exp(idx * (-2.0 * math.log(ROPE_THETA) / MLA_ROPE))
    ang = inv * pos_row
    cos, sin = jnp.cos(ang), jnp.sin(ang)
    pad = jnp.zeros((HEAD_PAD - MLA_NOPE - MLA_ROPE, t), F32)
    a_t = jnp.concatenate([jnp.ones((MLA_NOPE, t), F32), cos, cos, pad], axis=0)
    b_t = jnp.concatenate([jnp.zeros((MLA_NOPE, t), F32), -sin, sin, pad], axis=0)
    return a_t.T, b_t.T


def _rope_group(t, a, b):
    return t * a + pltpu.roll(t, LANES - MLA_ROPE, 1) * b


def _dup_halves(t):
    lane = lax.broadcasted_iota(jnp.int32, (1, LANES), 1)
    r = pltpu.roll(t, LANES // 2, 1)
    lo = lane < LANES // 2
    return jnp.concatenate([jnp.where(lo, t, r), jnp.where(lo, r, t)], axis=1)


def _inproj_kernel(x_ref, mod_ref, pos_ref, ng_ref, qg_ref, kvg_ref, win_ref, wuq_ref, wukv_ref,
                   q_ref, k_ref, v_ref, gm_ref, qs_ref, ks_ref, vs_ref, gs_ref):
    x = x_ref[0]
    mod = mod_ref[0]
    shift, scale = mod[0:1], mod[1:2]
    h = _rms(x, ng_ref[...]) * (1.0 + scale) + shift
    hb = h.astype(BF16)

    z = jnp.dot(hb, win_ref[...], preferred_element_type=F32)

    def seg(lo, hi):
        return z[:, lo:hi]

    a, b = _rope_tables(pos_ref[0])
    q_scale = (MLA_NOPE + MLA_ROPE) ** -0.5 * LOG2E

    zqr = seg(C_ZQ, C_ZKV)
    qn = _rms(zqr[:, :Q_LORA], qg_ref[...] * q_scale).astype(BF16)
    q = jnp.dot(qn, wuq_ref[...], preferred_element_type=F32)
    for hd in range(MLA_HEADS):
        sl = slice(hd * HEAD_PAD, (hd + 1) * HEAD_PAD)
        q_ref[0, :, sl] = _rope_group(q[:, sl], a, b).astype(BF16)

    kpe = _rope_group(zqr[:, Q_LORA:], a, b)
    kvn = _rms(seg(C_ZKV, C_GM), kvg_ref[...]).astype(BF16)
    lane = lax.broadcasted_iota(jnp.int32, (1, LANES), 1)
    first = lane < MLA_NOPE
    for pr in range(MLA_HEADS // 2):
        cols = slice(2 * pr * HEAD_PAD, (2 * pr + 2) * HEAD_PAD)
        kv = jnp.dot(kvn, wukv_ref[:, cols], preferred_element_type=F32)
        even, odd = kv[:, :HEAD_PAD], kv[:, HEAD_PAD:]
        k_ref[0, :, cols] = jnp.concatenate([jnp.where(first, even, kpe), jnp.where(first, odd, kpe)],
                                            axis=1).astype(BF16)
        v_ref[0, :, pr * LANES:(pr + 1) * LANES] = jnp.where(first, pltpu.roll(even, MLA_V, 1), odd).astype(BF16)

    g = seg(C_GM, C_QS)
    gm_ref[0] = (g * jax.nn.sigmoid(g)).astype(BF16)
    g = seg(C_GS, D_IN_PACKED)
    gs_ref[0] = (g * jax.nn.sigmoid(g)).astype(BF16)
    qs_ref[0] = (seg(C_QS, C_KS) * (SWA_HEAD_DIM ** -0.5 * LOG2E)).astype(BF16)
    kvs = seg(C_KS, C_GS)
    ks_ref[0] = _dup_halves(kvs[:, :SWA_KV_WIDTH]).astype(BF16)
    for j in range(vs_ref.shape[1]):
        vs_ref[0, j] = kvs[j * WINDOW:(j + 1) * WINDOW, SWA_KV_WIDTH:].T.astype(BF16)


def _inproj(x, mod3, pos_row, norm_gain, q_gain, kv_gain, w_in_p, w_uq_p, w_ukv_p, tm):
    b, s, d = x.shape
    const = lambda shape: pl.BlockSpec(shape, lambda bi, i: (0,) * len(shape), pipeline_mode=pl.Buffered(1))
    tok = lambda w: pl.BlockSpec((1, tm, w), lambda bi, i: (bi, i, 0))
    widths = [MLA_HEADS * HEAD_PAD, MLA_HEADS * HEAD_PAD, MLA_WIDTH, MLA_WIDTH,
              SWA_WIDTH, 2 * SWA_KV_WIDTH, None, SWA_WIDTH]
    vt_spec = pl.BlockSpec((1, tm // WINDOW, SWA_KV_WIDTH, WINDOW), lambda bi, i: (bi, i, 0, 0))
    vt_shape = jax.ShapeDtypeStruct((b, s // WINDOW, SWA_KV_WIDTH, WINDOW), BF16)
    return pl.pallas_call(
        _inproj_kernel,
        grid=(b, s // tm),
        in_specs=[tok(d),
                  pl.BlockSpec((1, 3, d), lambda bi, i: (bi, 0, 0)),
                  pl.BlockSpec((1, 1, tm), lambda bi, i: (bi, 0, i)),
                  const((1, d)), const((1, Q_LORA)), const((1, KV_LORA)),
                  const(w_in_p.shape), const(w_uq_p.shape), const(w_ukv_p.shape)],
        out_specs=[vt_spec if w is None else tok(w) for w in widths],
        out_shape=[vt_shape if w is None else jax.ShapeDtypeStruct((b, s, w), BF16) for w in widths],
        compiler_params=pltpu.CompilerParams(dimension_semantics=("parallel", "parallel"),
                                             vmem_limit_bytes=VMEM_LIMIT),
        name="inproj",
    )(x, mod3, pos_row, norm_gain, q_gain, kv_gain, w_in_p, w_uq_p, w_ukv_p)


def _dot_nt(a, b):
    return lax.dot_general(a, b, (((1,), (1,)), ((), ())), preferred_element_type=F32)


def _mla_kernel(q_ref, k_ref, v_ref, g_ref, o_ref, vl_ref, *, tq):
    nq = q_ref.shape[1] // tq
    row = lax.broadcasted_iota(jnp.int32, (tq, tq), 0)
    col = lax.broadcasted_iota(jnp.int32, (tq, tq), 1)
    causal = row <= col
    top = lax.broadcasted_iota(jnp.int32, (LANES, 1), 0) < MLA_V

    v_t = v_ref[0].astype(F32).T
    vl_ref[0] = jnp.where(top, v_t, 1.0).astype(BF16)
    vl_ref[1] = jnp.where(top, 1.0, v_t).astype(BF16)

    def scores(c, kb):
        q = q_ref[0, c * tq:(c + 1) * tq, :]
        k = k_ref[0, kb * tq:(kb + 1) * tq, :]
        return _dot_nt(k[:, :HEAD_PAD], q[:, :HEAD_PAD]), _dot_nt(k[:, HEAD_PAD:], q[:, HEAD_PAD:])

    def pv(p, head, kb):
        return jnp.dot(vl_ref[head, :, kb * tq:(kb + 1) * tq], p.astype(BF16), preferred_element_type=F32)

    tiles = [(c, kb) for c in range(nq) for kb in [c] + list(range(c))]
    ahead = [scores(*tiles[0]), scores(*tiles[1])]
    for t, (c, kb) in enumerate(tiles):
        sa, sb = ahead.pop(0)
        if t + 2 < len(tiles):
            ahead.append(scores(*tiles[t + 2]))
        if kb == c:
            sa = jnp.where(causal, sa, NEG_BIG)
            sb = jnp.where(causal, sb, NEG_BIG)
            ma = jnp.max(sa, axis=0, keepdims=True)
            mb = jnp.max(sb, axis=0, keepdims=True)
            acc_a = pv(jnp.exp2(sa - ma), 0, kb)
            acc_b = pv(jnp.exp2(sb - mb), 1, kb)
        else:
            ma_n = jnp.maximum(ma, jnp.max(sa, axis=0, keepdims=True))
            mb_n = jnp.maximum(mb, jnp.max(sb, axis=0, keepdims=True))
            acc_a = acc_a * jnp.exp2(ma - ma_n) + pv(jnp.exp2(sa - ma_n), 0, kb)
            acc_b = acc_b * jnp.exp2(mb - mb_n) + pv(jnp.exp2(sb - mb_n), 1, kb)
            ma, mb = ma_n, mb_n
        if t + 1 == len(tiles) or tiles[t + 1][0] != c:
            rows = slice(c * tq, (c + 1) * tq)
            num = jnp.concatenate([acc_a[:MLA_V], acc_b[MLA_V:]], axis=0)
            den = jnp.concatenate([acc_a[MLA_V:], acc_b[:MLA_V]], axis=0)
            o_ref[0, rows, :] = ((num / den).T * g_ref[0, rows, :].astype(F32)).astype(BF16)


def _mla(q, k, v, gm, tq):
    b, s, _ = q.shape
    pairs = MLA_HEADS // 2
    seq = lambda w: pl.BlockSpec((1, s, w), lambda bi, j: (bi, 0, j))
    return pl.pallas_call(
        functools.partial(_mla_kernel, tq=tq),
        grid=(b, pairs),
        in_specs=[seq(2 * HEAD_PAD), seq(2 * HEAD_PAD), seq(2 * MLA_V), seq(2 * MLA_V)],
        out_specs=seq(2 * MLA_V),
        out_shape=jax.ShapeDtypeStruct((b, s, MLA_WIDTH), BF16),
        scratch_shapes=[pltpu.VMEM((2, 2 * MLA_V, s), BF16)],
        compiler_params=pltpu.CompilerParams(dimension_semantics=("parallel", "parallel"),
                                             vmem_limit_bytes=VMEM_LIMIT),
        name="mla",
    )(q, k, v, gm)


def _swa_out_kernel(sink_ref, q_ref, k_ref, vt_ref, g_ref, pr_ref, om_ref, x_ref, mod_ref, wm_ref, ws_ref,
                    fg_ref, out_ref, o_ref, *, tq):
    i = pl.program_id(1)
    nblk = tq // WINDOW
    lane = lax.broadcasted_iota(jnp.int32, (1, LANES), 1)
    first = lane < SWA_HEAD_DIM
    row1 = lax.broadcasted_iota(jnp.int32, (WINDOW, WINDOW), 0)
    col1 = lax.broadcasted_iota(jnp.int32, (WINDOW, WINDOW), 1)
    in_cur1 = row1 <= col1
    in_cur = jnp.concatenate([in_cur1] * SWA_GROUP, axis=1)

    def positions(nl):
        n = i * nblk + nl
        n_prev = jnp.maximum(n - 1, 0) if nl == 0 else n - 1
        pq_row = pr_ref[0, n]
        pk_prev = pr_ref[0, n_prev]
        p0 = pk_prev[:, 0:1]
        pk = jnp.where(col1 <= row1, pq_row, pk_prev) - p0
        return n_prev, pk.T, pq_row - p0

    def unit(nl, kvh, n_prev, pk_t, pq):
        n = i * nblk + nl
        rows = slice(nl * WINDOW, (nl + 1) * WINDOW)
        ksl = slice(kvh * LANES, (kvh + 1) * LANES)
        dsl = slice(kvh * SWA_HEAD_DIM, (kvh + 1) * SWA_HEAD_DIM)
        heads = [kvh * SWA_GROUP + g for g in range(SWA_GROUP)]
        slopes = [LOG2E * 2.0 ** (-ALIBI_MAX_EXP * (h + 1) / SWA_HEADS) for h in heads]
        alibi = jnp.concatenate([sl * pk_t for sl in slopes], axis=1)
        sink = jnp.concatenate([sink_ref[h] * LOG2E + sl * pq for h, sl in zip(heads, slopes)], axis=1)
        pair_lanes = [slice((heads[0] // 2 + pr) * LANES, (heads[0] // 2 + pr + 1) * LANES)
                      for pr in range(SWA_GROUP // 2)]
        parts = []
        for psl in pair_lanes:
            q2 = q_ref[0, rows, psl]
            zq = jnp.zeros_like(q2)
            parts += [jnp.where(first, q2, zq), jnp.where(first, zq, q2)]
        q4 = jnp.concatenate(parts, axis=0)
        k2 = jnp.concatenate([k_ref[0, pl.ds(pl.multiple_of(n_prev * WINDOW, WINDOW), WINDOW), ksl],
                              k_ref[0, pl.ds(pl.multiple_of(n * WINDOW, WINDOW), WINDOW), ksl]], axis=0)
        s2 = _dot_nt(k2, q4)

        def finish():
            s_prev = s2[:WINDOW]
            if nl == 0:
                s_prev = jnp.where(i > 0, s_prev, NEG_BIG)
            s = jnp.where(in_cur, s2[WINDOW:], s_prev) + alibi
            m = jnp.maximum(jnp.max(s, axis=0, keepdims=True), sink)
            p = jnp.exp2(s - m)
            l = jnp.sum(p, axis=0, keepdims=True) + jnp.exp2(sink - m)
            zp = jnp.zeros_like(p)
            p2 = jnp.concatenate([jnp.where(in_cur, zp, p), jnp.where(in_cur, p, zp)], axis=0).astype(BF16)
            v2 = jnp.concatenate([vt_ref[0, n_prev, dsl, :], vt_ref[0, n, dsl, :]], axis=1)
            o_t = jnp.dot(v2, p2, preferred_element_type=F32) * (1.0 / l)
            for pr, psl in enumerate(pair_lanes):
                o2 = jnp.concatenate([o_t[:, 2 * pr * WINDOW:(2 * pr + 1) * WINDOW],
                                      o_t[:, (2 * pr + 1) * WINDOW:(2 * pr + 2) * WINDOW]], axis=0).T
                o_ref[rows, psl] = (o2 * g_ref[0, rows, psl].astype(F32)).astype(BF16)

        return finish

    pending = []
    for nl in range(nblk):
        n_prev, pk_t, pq = positions(nl)
        for kvh in range(SWA_KV_HEADS):
            pending.append(unit(nl, kvh, n_prev, pk_t, pq))
            if len(pending) > SWA_PIPELINE_DEPTH:
                pending.pop(0)()
    for fin in pending:
        fin()

    gate = mod_ref[0][2:3]
    for r0 in range(0, tq, OUT_ROWS):
        rows = slice(r0, r0 + OUT_ROWS)
        y = jnp.dot(om_ref[0, rows, :], wm_ref[...], preferred_element_type=F32)
        y += jnp.dot(o_ref[rows, :], ws_ref[...], preferred_element_type=F32)
        out_ref[0, rows, :] = _rms(x_ref[0, rows, :] + gate * y, fg_ref[...])


def _swa_out(sinks, qs, ks, vs, gs, pos_row, om, x, mod3, w_m, w_s, final_gain, tq):
    b, s, d = x.shape
    tok = lambda w: pl.BlockSpec((1, tq, w), lambda bi, i: (bi, i, 0))
    full = lambda w: pl.BlockSpec((1, s, w), lambda bi, i: (bi, 0, 0))
    const = lambda shape: pl.BlockSpec(shape, lambda bi, i: (0,) * len(shape))
    return pl.pallas_call(
        functools.partial(_swa_out_kernel, tq=tq),
        grid=(b, s // tq),
        in_specs=[pl.BlockSpec(memory_space=pltpu.SMEM),
                  tok(SWA_WIDTH), full(2 * SWA_KV_WIDTH),
                  pl.BlockSpec((1, s // WINDOW, SWA_KV_WIDTH, WINDOW), lambda bi, i: (bi, 0, 0, 0)), tok(SWA_WIDTH),
                  pl.BlockSpec((1, s // WINDOW, 1, WINDOW), lambda bi, i: (bi, 0, 0, 0)),
                  tok(MLA_WIDTH), tok(d), pl.BlockSpec((1, 3, d), lambda bi, i: (bi, 0, 0)),
                  const(w_m.shape), const(w_s.shape), const((1, d))],
        out_specs=tok(d),
        out_shape=jax.ShapeDtypeStruct((b, s, d), x.dtype),
        scratch_shapes=[pltpu.VMEM((tq, SWA_WIDTH), BF16)],
        compiler_params=pltpu.CompilerParams(dimension_semantics=("parallel", "arbitrary"),
                                             vmem_limit_bytes=VMEM_LIMIT),
        name="swa_out",
    )(sinks, qs, ks, vs, gs, pos_row, om, x, mod3, w_m, w_s, final_gain)


def kernel(x, c, positions, w_ada, b_ada, norm_gain, w_in, q_norm_gain, kv_norm_gain, w_uq, w_ukv, swa_sinks, w_out, final_gain):
    b, s, d = x.shape
    depth = w_ada.shape[0]
    assert depth == 1, "the final rmsnorm is fused into the (single) layer's output projection"
    pos = positions.astype(F32)
    pos_row = pos.reshape(b, s // WINDOW, 1, WINDOW)
    for l in range(depth):
        mod3 = _adaln(c, w_ada, b_ada[l], l).reshape(b, 3, d)
        w_in_p, w_uq_p, w_ukv_p = _pack_weights(w_in[l], w_uq[l], w_ukv[l])
        q, k, v, gm, qs, ks, vs, gs = _inproj(
            x, mod3, pos.reshape(b, 1, s), norm_gain[l].reshape(1, d), q_norm_gain[l].reshape(1, Q_LORA),
            kv_norm_gain[l].reshape(1, KV_LORA), w_in_p, w_uq_p, w_ukv_p, tm=ROWS_INPROJ)
        om = _mla(q, k, v, gm, tq=ROWS_MLA)
        w_o = w_out[l].astype(BF16)
        x = _swa_out(swa_sinks[l].astype(F32), qs, ks, vs, gs, pos_row, om, x, mod3,
                     w_o[:MLA_WIDTH], w_o[MLA_WIDTH:], final_gain.reshape(1, d), tq=ROWS_SWA_OUT)
    return x
```

```python
import functools
import math

import jax
import jax.numpy as jnp
import numpy as np
from jax import lax
from jax.experimental import pallas as pl
from jax.experimental.pallas import tpu as pltpu

F32 = jnp.float32
BF16 = jnp.bfloat16

MLA_HEADS = 8
MLA_NOPE = 64
MLA_ROPE = 32
MLA_V = 64
Q_LORA = 384
KV_LORA = 256
MLA_WIDTH = MLA_HEADS * MLA_V
SWA_HEADS = 8
SWA_KV_HEADS = 2
SWA_HEAD_DIM = 64
SWA_GROUP = SWA_HEADS // SWA_KV_HEADS
SWA_WIDTH = SWA_HEADS * SWA_HEAD_DIM
SWA_KV_WIDTH = SWA_KV_HEADS * SWA_HEAD_DIM
WINDOW = 128
ROPE_THETA = 10000.0
EPS = 1e-6
ALIBI_MAX_EXP = 8.0

LANES = 128
HEAD_PAD = LANES
LOG2E = math.log2(math.e)
NEG_BIG = -1e30

C_ZQ = 0
C_KR = C_ZQ + Q_LORA
C_ZKV = C_KR + LANES
C_GM = C_ZKV + KV_LORA
C_QS = C_GM + MLA_WIDTH
C_KS = C_QS + SWA_WIDTH
C_GS = C_KS + 2 * SWA_KV_WIDTH
D_IN_PACKED = C_GS + SWA_WIDTH

V7X_VMEM_BYTES = 64 * 1024 * 1024
VMEM_LIMIT = V7X_VMEM_BYTES * 7 // 8

ROWS_INPROJ = 1024
ROWS_MLA = 256
MLA_PAIRS_PER_STEP = 2
ROWS_SWA_OUT = 2048
OUT_ROWS = 256
SWA_PIPELINE_DEPTH = 3
COLS_ADALN = 1024


def _swap_halves(w):
    half = w.shape[-1] // 2
    return jnp.concatenate([w[..., half:], w[..., :half]], axis=-1)


def _pack_weights(w_in, w_uq, w_ukv):
    d = w_in.shape[0]
    s = np.cumsum([0, Q_LORA, KV_LORA, MLA_ROPE, MLA_WIDTH, SWA_WIDTH, SWA_KV_WIDTH, SWA_KV_WIDTH, SWA_WIDTH])
    zq, zkv, kr, gm, qs, ks, vs, gs = [w_in[:, s[i]:s[i + 1]] for i in range(8)]
    kr_blk = jnp.concatenate([jnp.zeros((d, MLA_NOPE), w_in.dtype), kr, _swap_halves(kr)], axis=1)
    w_in_p = jnp.concatenate([zq, kr_blk, zkv, gm, qs, ks, vs, gs], axis=1).astype(BF16)

    uq = w_uq.reshape(Q_LORA, MLA_HEADS, MLA_NOPE + MLA_ROPE)
    uq_p = jnp.concatenate([uq, _swap_halves(uq[..., MLA_NOPE:])], axis=-1)
    w_uq_p = uq_p.reshape(Q_LORA, MLA_HEADS * HEAD_PAD).astype(BF16)

    return w_in_p, w_uq_p, w_ukv.astype(BF16)


def _adaln_kernel(c_ref, w_ref, b_ref, o_ref):
    c = c_ref[...]
    a = c * jax.nn.sigmoid(c)
    a_hi = a.astype(BF16)
    a_lo = (a - a_hi.astype(F32)).astype(BF16)
    w = w_ref[...]
    w_hi = w.astype(BF16)
    w_lo = (w - w_hi.astype(F32)).astype(BF16)
    acc = jnp.dot(a_hi, w_hi, preferred_element_type=F32)
    acc += jnp.dot(a_hi, w_lo, preferred_element_type=F32)
    acc += jnp.dot(a_lo, w_hi, preferred_element_type=F32)
    o_ref[...] = acc + b_ref[...]


def _adaln(c, w_ada, b_ada, layer):
    b, d = c.shape
    n = w_ada.shape[2]
    tn = COLS_ADALN
    return pl.pallas_call(
        _adaln_kernel,
        grid=(n // tn,),
        in_specs=[pl.BlockSpec((b, d), lambda j: (0, 0)),
                  pl.BlockSpec((None, d, tn), lambda j: (layer, 0, j)),
                  pl.BlockSpec((1, tn), lambda j: (0, j))],
        out_specs=pl.BlockSpec((b, tn), lambda j: (0, j)),
        out_shape=jax.ShapeDtypeStruct((b, n), F32),
        compiler_params=pltpu.CompilerParams(dimension_semantics=("arbitrary",), vmem_limit_bytes=VMEM_LIMIT),
        name="adaln",
    )(c, w_ada, b_ada.reshape(1, n))


def _rms(t, gain):
    return t * lax.rsqrt(jnp.mean(t * t, axis=-1, keepdims=True) + EPS) * gain


def _rope_tables(pos_row):
    t = pos_row.shape[1]
    half = MLA_ROPE // 2
    idx = lax.broadcasted_iota(jnp.int32, (half, 1), 0).astype(F32)
    inv = jnp.exp(idx * (-2.0 * math.log(ROPE_THETA) / MLA_ROPE))
    ang = inv * pos_row
    cos, sin = jnp.cos(ang), jnp.sin(ang)
    pad = jnp.zeros((HEAD_PAD - MLA_NOPE - MLA_ROPE, t), F32)
    a_t = jnp.concatenate([jnp.ones((MLA_NOPE, t), F32), cos, cos, pad], axis=0)
    b_t = jnp.concatenate([jnp.zeros((MLA_NOPE, t), F32), -sin, sin, pad], axis=0)
    return a_t.T, b_t.T


def _rope_group(t, a, b):
    return t * a + pltpu.roll(t, LANES - MLA_ROPE, 1) * b


def _dup_halves(t):
    lane = lax.broadcasted_iota(jnp.int32, (1, LANES), 1)
    r = pltpu.roll(t, LANES // 2, 1)
    lo = lane < LANES // 2
    return jnp.concatenate([jnp.where(lo, t, r), jnp.where(lo, r, t)], axis=1)


def _inproj_kernel(x_ref, mod_ref, pos_ref, ng_ref, qg_ref, kvg_ref, win_ref, wuq_ref, wukv_ref,
                   q_ref, k_ref, v_ref, gm_ref, qs_ref, ks_ref, vs_ref, gs_ref):
    x = x_ref[0]
    mod = mod_ref[0]
    shift, scale = mod[0:1], mod[1:2]
    h = _rms(x, ng_ref[...]) * (1.0 + scale) + shift
    hb = h.astype(BF16)

    z = jnp.dot(hb, win_ref[...], preferred_element_type=F32)

    def seg(lo, hi):
        return z[:, lo:hi]

    a, b = _rope_tables(pos_ref[0])
    q_scale = (MLA_NOPE + MLA_ROPE) ** -0.5 * LOG2E

    zqr = seg(C_ZQ, C_ZKV)
    qn = _rms(zqr[:, :Q_LORA], qg_ref[...] * q_scale).astype(BF16)
    q = jnp.dot(qn, wuq_ref[...], preferred_element_type=F32)
    for hd in range(MLA_HEADS):
        sl = slice(hd * HEAD_PAD, (hd + 1) * HEAD_PAD)
        q_ref[0, :, sl] = _rope_group(q[:, sl], a, b).astype(BF16)

    kpe = _rope_group(zqr[:, Q_LORA:], a, b)
    kvn = _rms(seg(C_ZKV, C_GM), kvg_ref[...]).astype(BF16)
    lane = lax.broadcasted_iota(jnp.int32, (1, LANES), 1)
    first = lane < MLA_NOPE
    for pr in range(MLA_HEADS // 2):
        cols = slice(2 * pr * HEAD_PAD, (2 * pr + 2) * HEAD_PAD)
        kv = jnp.dot(kvn, wukv_ref[:, cols], preferred_element_type=F32)
        even, odd = kv[:, :HEAD_PAD], kv[:, HEAD_PAD:]
        k_ref[0, :, cols] = jnp.concatenate([jnp.where(first, even, kpe), jnp.where(first, odd, kpe)],
                                            axis=1).astype(BF16)
        v_ref[0, :, pr * LANES:(pr + 1) * LANES] = jnp.where(first, pltpu.roll(even, MLA_V, 1), odd).astype(BF16)

    g = seg(C_GM, C_QS)
    gm_ref[0] = (g * jax.nn.sigmoid(g)).astype(BF16)
    g = seg(C_GS, D_IN_PACKED)
    gs_ref[0] = (g * jax.nn.sigmoid(g)).astype(BF16)
    qs_ref[0] = (seg(C_QS, C_KS) * (SWA_HEAD_DIM ** -0.5 * LOG2E)).astype(BF16)
    kvs = seg(C_KS, C_GS)
    ks_ref[0] = _dup_halves(kvs[:, :SWA_KV_WIDTH]).astype(BF16)
    for j in range(vs_ref.shape[1]):
        vs_ref[0, j] = kvs[j * WINDOW:(j + 1) * WINDOW, SWA_KV_WIDTH:].T.astype(BF16)


def _inproj(x, mod3, pos_row, norm_gain, q_gain, kv_gain, w_in_p, w_uq_p, w_ukv_p, tm):
    b, s, d = x.shape
    const = lambda shape: pl.BlockSpec(shape, lambda bi, i: (0,) * len(shape), pipeline_mode=pl.Buffered(1))
    tok = lambda w: pl.BlockSpec((1, tm, w), lambda bi, i: (bi, i, 0))
    widths = [MLA_HEADS * HEAD_PAD, MLA_HEADS * HEAD_PAD, MLA_WIDTH, MLA_WIDTH,
              SWA_WIDTH, 2 * SWA_KV_WIDTH, None, SWA_WIDTH]
    vt_spec = pl.BlockSpec((1, tm // WINDOW, SWA_KV_WIDTH, WINDOW), lambda bi, i: (bi, i, 0, 0))
    vt_shape = jax.ShapeDtypeStruct((b, s // WINDOW, SWA_KV_WIDTH, WINDOW), BF16)
    return pl.pallas_call(
        _inproj_kernel,
        grid=(b, s // tm),
        in_specs=[tok(d),
                  pl.BlockSpec((1, 3, d), lambda bi, i: (bi, 0, 0)),
                  pl.BlockSpec((1, 1, tm), lambda bi, i: (bi, 0, i)),
                  const((1, d)), const((1, Q_LORA)), const((1, KV_LORA)),
                  const(w_in_p.shape), const(w_uq_p.shape), const(w_ukv_p.shape)],
        out_specs=[vt_spec if w is None else tok(w) for w in widths],
        out_shape=[vt_shape if w is None else jax.ShapeDtypeStruct((b, s, w), BF16) for w in widths],
        compiler_params=pltpu.CompilerParams(dimension_semantics=("parallel", "parallel"),
                                             vmem_limit_bytes=VMEM_LIMIT),
        name="inproj",
    )(x, mod3, pos_row, norm_gain, q_gain, kv_gain, w_in_p, w_uq_p, w_ukv_p)


def _dot_nt(a, b):
    return lax.dot_general(a, b, (((1,), (1,)), ((), ())), preferred_element_type=F32)


def _mla_kernel(q_ref, k_ref, v_ref, g_ref, o_ref, vl_ref, *, tq):
    nq = q_ref.shape[1] // tq
    row = lax.broadcasted_iota(jnp.int32, (tq, tq), 0)
    col = lax.broadcasted_iota(jnp.int32, (tq, tq), 1)
    causal = row <= col
    top = lax.broadcasted_iota(jnp.int32, (LANES, 1), 0) < MLA_V

    npairs = v_ref.shape[2] // LANES
    for pr in range(npairs):
        v_t = v_ref[0, :, pr * LANES:(pr + 1) * LANES].astype(F32).T
        vl_ref[2 * pr] = jnp.where(top, v_t, 1.0).astype(BF16)
        vl_ref[2 * pr + 1] = jnp.where(top, 1.0, v_t).astype(BF16)

    def scores(pr, c, kb):
        q = q_ref[0, c * tq:(c + 1) * tq, 2 * pr * HEAD_PAD:(2 * pr + 2) * HEAD_PAD]
        k = k_ref[0, kb * tq:(kb + 1) * tq, 2 * pr * HEAD_PAD:(2 * pr + 2) * HEAD_PAD]
        return _dot_nt(k[:, :HEAD_PAD], q[:, :HEAD_PAD]), _dot_nt(k[:, HEAD_PAD:], q[:, HEAD_PAD:])

    def pv(p, head, kb):
        return jnp.dot(vl_ref[head, :, kb * tq:(kb + 1) * tq], p.astype(BF16), preferred_element_type=F32)

    tiles = [(pr, c, kb) for pr in range(npairs) for c in range(nq) for kb in [c] + list(range(c))]
    ahead = [scores(*tiles[0]), scores(*tiles[1])]
    for t, (pr, c, kb) in enumerate(tiles):
        sa, sb = ahead.pop(0)
        if t + 2 < len(tiles):
            ahead.append(scores(*tiles[t + 2]))
        if kb == c:
            sa = jnp.where(causal, sa, NEG_BIG)
            sb = jnp.where(causal, sb, NEG_BIG)
            ma = jnp.max(sa, axis=0, keepdims=True)
            mb = jnp.max(sb, axis=0, keepdims=True)
            acc_a = pv(jnp.exp2(sa - ma), 2 * pr, kb)
            acc_b = pv(jnp.exp2(sb - mb), 2 * pr + 1, kb)
        else:
            ma_n = jnp.maximum(ma, jnp.max(sa, axis=0, keepdims=True))
            mb_n = jnp.maximum(mb, jnp.max(sb, axis=0, keepdims=True))
            acc_a = acc_a * jnp.exp2(ma - ma_n) + pv(jnp.exp2(sa - ma_n), 2 * pr, kb)
            acc_b = acc_b * jnp.exp2(mb - mb_n) + pv(jnp.exp2(sb - mb_n), 2 * pr + 1, kb)
            ma, mb = ma_n, mb_n
        if t + 1 == len(tiles) or tiles[t + 1][:2] != (pr, c):
            rows, lanes = slice(c * tq, (c + 1) * tq), slice(pr * LANES, (pr + 1) * LANES)
            num = jnp.concatenate([acc_a[:MLA_V], acc_b[MLA_V:]], axis=0)
            den = jnp.concatenate([acc_a[MLA_V:], acc_b[:MLA_V]], axis=0)
            o_ref[0, rows, lanes] = ((num / den).T * g_ref[0, rows, lanes].astype(F32)).astype(BF16)


def _mla(q, k, v, gm, tq, pairs_per_step):
    b, s, _ = q.shape
    pairs = MLA_HEADS // 2 // pairs_per_step
    seq = lambda w: pl.BlockSpec((1, s, pairs_per_step * w), lambda bi, j: (bi, 0, j))
    return pl.pallas_call(
        functools.partial(_mla_kernel, tq=tq),
        grid=(b, pairs),
        in_specs=[seq(2 * HEAD_PAD), seq(2 * HEAD_PAD), seq(2 * MLA_V), seq(2 * MLA_V)],
        out_specs=seq(2 * MLA_V),
        out_shape=jax.ShapeDtypeStruct((b, s, MLA_WIDTH), BF16),
        scratch_shapes=[pltpu.VMEM((2 * pairs_per_step, 2 * MLA_V, s), BF16)],
        compiler_params=pltpu.CompilerParams(dimension_semantics=("parallel", "parallel"),
                                             vmem_limit_bytes=VMEM_LIMIT),
        name="mla",
    )(q, k, v, gm)


def _swa_out_kernel(sink_ref, q_ref, k_ref, vt_ref, g_ref, pr_ref, om_ref, x_ref, mod_ref, wm_ref, ws_ref,
                    fg_ref, out_ref, o_ref, *, tq):
    i = pl.program_id(1)
    nblk = tq // WINDOW
    lane = lax.broadcasted_iota(jnp.int32, (1, LANES), 1)
    first = lane < SWA_HEAD_DIM
    row1 = lax.broadcasted_iota(jnp.int32, (WINDOW, WINDOW), 0)
    col1 = lax.broadcasted_iota(jnp.int32, (WINDOW, WINDOW), 1)
    in_cur1 = row1 <= col1
    in_cur = jnp.concatenate([in_cur1] * SWA_GROUP, axis=1)

    def positions(nl):
        n = i * nblk + nl
        n_prev = jnp.maximum(n - 1, 0) if nl == 0 else n - 1
        pq_row = pr_ref[0, n]
        pk_prev = pr_ref[0, n_prev]
        p0 = pk_prev[:, 0:1]
        pk = jnp.where(col1 <= row1, pq_row, pk_prev) - p0
        return n_prev, pk.T, pq_row - p0

    def unit(nl, kvh, n_prev, pk_t, pq):
        n = i * nblk + nl
        rows = slice(nl * WINDOW, (nl + 1) * WINDOW)
        ksl = slice(kvh * LANES, (kvh + 1) * LANES)
        dsl = slice(kvh * SWA_HEAD_DIM, (kvh + 1) * SWA_HEAD_DIM)
        heads = [kvh * SWA_GROUP + g for g in range(SWA_GROUP)]
        slopes = [LOG2E * 2.0 ** (-ALIBI_MAX_EXP * (h + 1) / SWA_HEADS) for h in heads]
        alibi = jnp.concatenate([sl * pk_t for sl in slopes], axis=1)
        sink = jnp.concatenate([sink_ref[h] * LOG2E + sl * pq for h, sl in zip(heads, slopes)], axis=1)
        pair_lanes = [slice((heads[0] // 2 + pr) * LANES, (heads[0] // 2 + pr + 1) * LANES)
                      for pr in range(SWA_GROUP // 2)]
        parts = []
        for psl in pair_lanes:
            q2 = q_ref[0, rows, psl]
            zq = jnp.zeros_like(q2)
            parts += [jnp.where(first, q2, zq), jnp.where(first, zq, q2)]
        q4 = jnp.concatenate(parts, axis=0)
        k2 = jnp.concatenate([k_ref[0, pl.ds(pl.multiple_of(n_prev * WINDOW, WINDOW), WINDOW), ksl],
                              k_ref[0, pl.ds(pl.multiple_of(n * WINDOW, WINDOW), WINDOW), ksl]], axis=0)
        s2 = _dot_nt(k2, q4)

        def finish():
            s_prev = s2[:WINDOW]
            if nl == 0:
                s_prev = jnp.where(i > 0, s_prev, NEG_BIG)
            s = jnp.where(in_cur, s2[WINDOW:], s_prev) + alibi
            m = jnp.maximum(jnp.max(s, axis=0, keepdims=True), sink)
            p = jnp.exp2(s - m)
            l = jnp.sum(p, axis=0, keepdims=True) + jnp.exp2(sink - m)
            zp = jnp.zeros_like(p)
            p2 = jnp.concatenate([jnp.where(in_cur, zp, p), jnp.where(in_cur, p, zp)], axis=0).astype(BF16)
            v2 = jnp.concatenate([vt_ref[0, n_prev, dsl, :], vt_ref[0, n, dsl, :]], axis=1)
            o_t = jnp.dot(v2, p2, preferred_element_type=F32) * (1.0 / l)
            for pr, psl in enumerate(pair_lanes):
                o2 = jnp.concatenate([o_t[:, 2 * pr * WINDOW:(2 * pr + 1) * WINDOW],
                                      o_t[:, (2 * pr + 1) * WINDOW:(2 * pr + 2) * WINDOW]], axis=0).T
                o_ref[rows, psl] = (o2 * g_ref[0, rows, psl].astype(F32)).astype(BF16)

        return finish

    pending = []
    for nl in range(nblk):
        n_prev, pk_t, pq = positions(nl)
        for kvh in range(SWA_KV_HEADS):
            pending.append(unit(nl, kvh, n_prev, pk_t, pq))
            if len(pending) > SWA_PIPELINE_DEPTH:
                pending.pop(0)()
    for fin in pending:
        fin()

    gate = mod_ref[0][2:3]
    for r0 in range(0, tq, OUT_ROWS):
        rows = slice(r0, r0 + OUT_ROWS)
        y = jnp.dot(om_ref[0, rows, :], wm_ref[...], preferred_element_type=F32)
        y += jnp.dot(o_ref[rows, :], ws_ref[...], preferred_element_type=F32)
        out_ref[0, rows, :] = _rms(x_ref[0, rows, :] + gate * y, fg_ref[...])


def _swa_out(sinks, qs, ks, vs, gs, pos_row, om, x, mod3, w_m, w_s, final_gain, tq):
    b, s, d = x.shape
    tok = lambda w: pl.BlockSpec((1, tq, w), lambda bi, i: (bi, i, 0))
    full = lambda w: pl.BlockSpec((1, s, w), lambda bi, i: (bi, 0, 0))
    const = lambda shape: pl.BlockSpec(shape, lambda bi, i: (0,) * len(shape))
    return pl.pallas_call(
        functools.partial(_swa_out_kernel, tq=tq),
        grid=(b, s // tq),
        in_specs=[pl.BlockSpec(memory_space=pltpu.SMEM),
                  tok(SWA_WIDTH), full(2 * SWA_KV_WIDTH),
                  pl.BlockSpec((1, s // WINDOW, SWA_KV_WIDTH, WINDOW), lambda bi, i: (bi, 0, 0, 0)), tok(SWA_WIDTH),
                  pl.BlockSpec((1, s // WINDOW, 1, WINDOW), lambda bi, i: (bi, 0, 0, 0)),
                  tok(MLA_WIDTH), tok(d), pl.BlockSpec((1, 3, d), lambda bi, i: (bi, 0, 0)),
                  const(w_m.shape), const(w_s.shape), const((1, d))],
        out_specs=tok(d),
        out_shape=jax.ShapeDtypeStruct((b, s, d), x.dtype),
        scratch_shapes=[pltpu.VMEM((tq, SWA_WIDTH), BF16)],
        compiler_params=pltpu.CompilerParams(dimension_semantics=("parallel", "arbitrary"),
                                             vmem_limit_bytes=VMEM_LIMIT),
        name="swa_out",
    )(sinks, qs, ks, vs, gs, pos_row, om, x, mod3, w_m, w_s, final_gain)


def kernel(x, c, positions, w_ada, b_ada, norm_gain, w_in, q_norm_gain, kv_norm_gain, w_uq, w_ukv, swa_sinks, w_out, final_gain):
    b, s, d = x.shape
    depth = w_ada.shape[0]
    assert depth == 1, "the final rmsnorm is fused into the (single) layer's output projection"
    pos = positions.astype(F32)
    pos_row = pos.reshape(b, s // WINDOW, 1, WINDOW)
    for l in range(depth):
        mod3 = _adaln(c, w_ada, b_ada[l], l).reshape(b, 3, d)
        w_in_p, w_uq_p, w_ukv_p = _pack_weights(w_in[l], w_uq[l], w_ukv[l])
        q, k, v, gm, qs, ks, vs, gs = _inproj(
            x, mod3, pos.reshape(b, 1, s), norm_gain[l].reshape(1, d), q_norm_gain[l].reshape(1, Q_LORA),
            kv_norm_gain[l].reshape(1, KV_LORA), w_in_p, w_uq_p, w_ukv_p, tm=ROWS_INPROJ)
        om = _mla(q, k, v, gm, tq=ROWS_MLA, pairs_per_step=MLA_PAIRS_PER_STEP)
        w_o = w_out[l].astype(BF16)
        x = _swa_out(swa_sinks[l].astype(F32), qs, ks, vs, gs, pos_row, om, x, mod3,
                     w_o[:MLA_WIDTH], w_o[MLA_WIDTH:], final_gain.reshape(1, d), tq=ROWS_SWA_OUT)
    return x
```
